```python
import jax, jax.numpy as jnp
from jax import lax
import numpy as np

D_MODEL = 1024
BATCH = 2
SEQ = 8192
DEPTH = 1
DEC_BATCH = 32
DEC_SEQ = 1
PAST_LEN = 16384
PAGE_SIZE = 128

N_HEADS = 8
HEAD_DIM = 64
ATTN_WIDTH = N_HEADS * HEAD_DIM
MOBA_BLOCK = 256
MOBA_TOP_K = 3
Q_CHUNK = 64
ATTN_SCALE = HEAD_DIM ** -0.5
POOL_WINDOWS = (2, 4, 8, 16)
N_POOL_GROUPS = len(POOL_WINDOWS)
POOL_WIDTH = D_MODEL // 2
POOL_GROUP = POOL_WIDTH // N_POOL_GROUPS
POOL_HIST = max(POOL_WINDOWS) - 1
D_FF = 4 * D_MODEL
PLE_DIM = 256
IN_WIDTH = 3 * ATTN_WIDTH + POOL_WIDTH + 2 * D_MODEL
SPLIT_AT = [ATTN_WIDTH, 2 * ATTN_WIDTH, 3 * ATTN_WIDTH, 3 * ATTN_WIDTH + POOL_WIDTH, 3 * ATTN_WIDTH + POOL_WIDTH + D_MODEL]
EPS = 1e-6
NEG_INF = -1e30

kernel_name = 'moba_pool_hybrid_step'


def rms_norm(x, g):
    xf = x.astype(jnp.float32)
    y = xf * lax.rsqrt(jnp.mean(xf * xf, axis=-1, keepdims=True) + EPS) * g.astype(jnp.float32)
    return y.astype(x.dtype)


def mixer_inputs(x, ln, w_in, q_norm, k_norm):
    B, S, _ = x.shape
    z = rms_norm(x, ln) @ w_in
    q, k, v, u, ga, gb = jnp.split(z, SPLIT_AT, axis=-1)
    heads = lambda t: t.reshape(B, S, N_HEADS, HEAD_DIM)
    return rms_norm(heads(q), q_norm), rms_norm(heads(k), k_norm), heads(v), u, ga, gb


def moba_attention(q, k, v, q_pos):
    B, S, H, hd = q.shape
    L = k.shape[1]
    nb = -(-L // MOBA_BLOCK)
    pad = nb * MOBA_BLOCK - L
    blocks = lambda t: jnp.pad(t, ((0, 0), (0, pad), (0, 0), (0, 0))).reshape(B, nb, MOBA_BLOCK, H, hd).transpose(0, 3, 1, 2, 4)
    kb, vb = blocks(k), blocks(v)
    k_mean = jnp.mean(kb.astype(jnp.float32), axis=3)
    top = min(MOBA_TOP_K, nb)
    qc = Q_CHUNK if S % Q_CHUNK == 0 else S
    n_chunks = S // qc
    q_chunks = q.reshape(B, n_chunks, qc, H, hd).transpose(1, 0, 3, 2, 4)
    pos_chunks = q_pos.reshape(n_chunks, qc)
    b_idx = jnp.arange(B)[:, None, None, None]
    h_idx = jnp.arange(H)[None, :, None, None]
    blk_range = jnp.arange(nb)
    in_block = jnp.arange(MOBA_BLOCK)

    def chunk(args):
        qq, pp = args
        own = pp // MOBA_BLOCK
        gate = jnp.einsum('bhqd,bhnd->bhqn', qq.astype(jnp.float32), k_mean)
        gate = jnp.where(blk_range[None, :] < own[:, None], gate, NEG_INF)
        _, idx = lax.top_k(gate, top)
        own_b = jnp.broadcast_to(own[None, None, :, None], (B, H, qc, 1))
        sel = jnp.concatenate([idx, own_b], axis=-1)
        slot_ok = jnp.concatenate([idx < own[:, None], jnp.ones((B, H, qc, 1), bool)], axis=-1)
        kg = kb[b_idx, h_idx, sel]
        vg = vb[b_idx, h_idx, sel]
        key_pos = sel[..., None] * MOBA_BLOCK + in_block
        mask = slot_ok[..., None] & (key_pos <= pp[:, None, None])
        s = jnp.einsum('bhqd,bhqnkd->bhqnk', qq, kg, preferred_element_type=jnp.float32) * ATTN_SCALE
        s = jnp.where(mask, s, NEG_INF).reshape(B, H, qc, (top + 1) * MOBA_BLOCK)
        w = jax.nn.softmax(s, axis=-1).astype(vg.dtype).reshape(B, H, qc, top + 1, MOBA_BLOCK)
        return jnp.einsum('bhqnk,bhqnkd->bhqd', w, vg)

    out = lax.map(chunk, (q_chunks, pos_chunks))
    return out.transpose(1, 0, 3, 2, 4).reshape(B, S, H * hd)


def pool_branch(u, hist, pos0, pool_w, pool_scale):
    B, S, C = u.shape
    e = jnp.concatenate([hist, u], axis=1)
    ef = e.astype(jnp.float32)
    cs = jnp.concatenate([jnp.zeros((B, 1, C), jnp.float32), jnp.cumsum(ef, axis=1)], axis=1)
    pos = pos0 + jnp.arange(S)
    means = []
    for g, w in enumerate(POOL_WINDOWS):
        c0, c1 = g * POOL_GROUP, (g + 1) * POOL_GROUP
        win = cs[:, POOL_HIST + 1:, c0:c1] - cs[:, POOL_HIST + 1 - w:POOL_HIST + 1 - w + S, c0:c1]
        cnt = jnp.minimum(w, pos + 1).astype(jnp.float32)[None, :, None]
        means.append(win / cnt)
    d = (jnp.concatenate(means, axis=-1) - ef[:, POOL_HIST:]).astype(u.dtype)
    d = d.reshape(B, S, N_POOL_GROUPS, POOL_GROUP)
    y = jnp.einsum('bsgc,gcd->bsgd', d, pool_w).reshape(B, S, C) * pool_scale
    return y, e[:, -POOL_HIST:]


def merge_and_channel_mix(x, a, b, ga, gb, p, w_attn_out, w_pool_out, w_out, ln_mlp, w_up, w_down, ln_ple, w_ple_gate, w_ple_proj):
    m = jax.nn.sigmoid(ga) * (a @ w_attn_out) + jax.nn.sigmoid(gb) * (b @ w_pool_out)
    x = x + m @ w_out
    hid = jax.nn.relu(rms_norm(x, ln_mlp) @ w_up)
    x = x + (hid * hid) @ w_down
    gate = jax.nn.sigmoid(rms_norm(x, ln_ple) @ w_ple_gate)
    return x + gate * (p @ w_ple_proj)


def setup_inputs(seed: int = 0) -> dict:
    key = jax.random.key(seed)
    ks = jax.random.split(key, 24)
    f32 = jnp.float32
    n_pages = PAST_LEN // PAGE_SIZE
    n_used = DEC_BATCH * n_pages
    n_phys = n_used + max(1, n_used // 4)
    nrm = lambda k, shape, scale=1.0: jax.random.normal(k, shape, f32) * scale
    gain = lambda k, shape: 1.0 + 0.05 * jax.random.normal(k, shape, f32)
    page_table = jax.random.permutation(ks[5], n_phys)[:n_used].reshape(DEC_BATCH, n_pages).astype(jnp.int32)
    return {
        'x_prompt': nrm(ks[0], (BATCH, SEQ, D_MODEL)),
        'x_sample': nrm(ks[1], (DEC_BATCH, DEC_SEQ, D_MODEL)),
        'cache_k': nrm(ks[2], (DEPTH, n_phys, PAGE_SIZE, N_HEADS, HEAD_DIM)),
        'cache_v': nrm(ks[3], (DEPTH, n_phys, PAGE_SIZE, N_HEADS, HEAD_DIM)),
        'state_pool': nrm(ks[4], (DEPTH, DEC_BATCH, POOL_HIST, POOL_WIDTH)),
        'page_table': page_table,
        'p_prompt': nrm(ks[6], (DEPTH, BATCH, SEQ, PLE_DIM)),
        'p_sample': nrm(ks[7], (DEPTH, DEC_BATCH, DEC_SEQ, PLE_DIM)),
        'ln_mix': gain(ks[8], (DEPTH, D_MODEL)),
        'w_in': nrm(ks[9], (DEPTH, D_MODEL, IN_WIDTH), D_MODEL ** -0.5),
        'q_norm': gain(ks[10], (DEPTH, HEAD_DIM)),
        'k_norm': gain(ks[11], (DEPTH, HEAD_DIM)),
        'pool_w': nrm(ks[12], (DEPTH, N_POOL_GROUPS, POOL_GROUP, POOL_GROUP), POOL_GROUP ** -0.5),
        'pool_scale': gain(ks[13], (DEPTH, POOL_WIDTH)),
        'w_attn_out': nrm(ks[14], (DEPTH, ATTN_WIDTH, D_MODEL), ATTN_WIDTH ** -0.5),
        'w_pool_out': nrm(ks[15], (DEPTH, POOL_WIDTH, D_MODEL), POOL_WIDTH ** -0.5),
        'w_out': nrm(ks[16], (DEPTH, D_MODEL, D_MODEL), D_MODEL ** -0.5),
        'ln_mlp': gain(ks[17], (DEPTH, D_MODEL)),
        'w_up': nrm(ks[18], (DEPTH, D_MODEL, D_FF), D_MODEL ** -0.5),
        'w_down': nrm(ks[19], (DEPTH, D_FF, D_MODEL), D_FF ** -0.5),
        'ln_ple': gain(ks[20], (DEPTH, D_MODEL)),
        'w_ple_gate': nrm(ks[21], (DEPTH, D_MODEL, D_MODEL), D_MODEL ** -0.5),
        'w_ple_proj': nrm(ks[22], (DEPTH, PLE_DIM, D_MODEL), PLE_DIM ** -0.5),
    }


def reference(x_prompt, x_sample, cache_k, cache_v, state_pool, page_table, p_prompt, p_sample, ln_mix, w_in, q_norm, k_norm, pool_w, pool_scale, w_attn_out, w_pool_out, w_out, ln_mlp, w_up, w_down, ln_ple, w_ple_gate, w_ple_proj):
    n_seq, n_pages = page_table.shape
    past_len = n_pages * PAGE_SIZE
    bp, sp, _ = x_prompt.shape
    ss = x_sample.shape[1]
    xp, xs = x_prompt, x_sample
    kp_l, vp_l, hp_l, ks_l, vs_l, hs_l = [], [], [], [], [], []
    for l in range(DEPTH):
        tail = (w_attn_out[l], w_pool_out[l], w_out[l], ln_mlp[l], w_up[l], w_down[l], ln_ple[l], w_ple_gate[l], w_ple_proj[l])
        q, k, v, u, ga, gb = mixer_inputs(xp, ln_mix[l], w_in[l], q_norm[l], k_norm[l])
        a = moba_attention(q, k, v, jnp.arange(sp))
        b, hist_p = pool_branch(u, jnp.zeros((bp, POOL_HIST, POOL_WIDTH), u.dtype), 0, pool_w[l], pool_scale[l])
        xp = merge_and_channel_mix(xp, a, b, ga, gb, p_prompt[l], *tail)
        kp_l.append(k); vp_l.append(v); hp_l.append(hist_p)
        q, k, v, u, ga, gb = mixer_inputs(xs, ln_mix[l], w_in[l], q_norm[l], k_norm[l])
        k_past = cache_k[l, page_table].reshape(n_seq, past_len, N_HEADS, HEAD_DIM)
        v_past = cache_v[l, page_table].reshape(n_seq, past_len, N_HEADS, HEAD_DIM)
        a = moba_attention(q, jnp.concatenate([k_past, k], axis=1), jnp.concatenate([v_past, v], axis=1), past_len + jnp.arange(ss))
        b, hist_s = pool_branch(u, state_pool[l], past_len, pool_w[l], pool_scale[l])
        xs = merge_and_channel_mix(xs, a, b, ga, gb, p_sample[l], *tail)
        ks_l.append(k); vs_l.append(v); hs_l.append(hist_s)
    return (xp, xs, jnp.stack(kp_l), jnp.stack(vp_l), jnp.stack(hp_l), jnp.stack(ks_l), jnp.stack(vs_l), jnp.stack(hs_l))
```

```python
import functools

import jax
import jax.numpy as jnp
from jax import lax
from jax.experimental import pallas as pl
from jax.experimental.pallas import tpu as pltpu

D_MODEL = 1024
N_HEADS = 8
HEAD_DIM = 64
ATTN_WIDTH = N_HEADS * HEAD_DIM
MOBA_BLOCK = 256
MOBA_TOP_K = 3
PAGE_SIZE = 128
PAGES_PER_BLOCK = MOBA_BLOCK // PAGE_SIZE
ATTN_SCALE = HEAD_DIM ** -0.5
POOL_WINDOWS = (2, 4, 8, 16)
POOL_GROUP = 128
POOL_WIDTH = 512
POOL_HIST = 15
HIST_ROWS = 16
D_FF = 4 * D_MODEL
PLE_DIM = 256
EPS = 1e-6
NEG_INF = -1e30
LOWEST = -3e38

LANES = 128
VMEM_LIMIT = 56 * 1024 * 1024

F32 = jnp.float32
BF16 = jnp.bfloat16

_NT = (((1,), (1,)), ((), ()))
_TN = (((0,), (0,)), ((), ()))


def _rms_rows(x, g):
    ms = jnp.mean(x * x, axis=-1, keepdims=True)
    return x * lax.rsqrt(ms + EPS) * g


def _split_bf16(a):
    hi = a.astype(BF16)
    lo = (a - hi.astype(F32)).astype(BF16)
    return hi, lo


def _const_spec(shape):
    zeros = (0,) * len(shape)
    return pl.BlockSpec(shape, lambda *_: zeros, pipeline_mode=pl.Buffered(1))


def _pool_means(e_ref, tm, pos0):
    pos = pos0 + lax.broadcasted_iota(jnp.int32, (tm, 1), 0)
    out = []
    for g, w in enumerate(POOL_WINDOWS):
        c0 = g * POOL_GROUP
        cur = e_ref[HIST_ROWS:HIST_ROWS + tm, c0:c0 + POOL_GROUP]
        acc = cur
        for k in range(1, w):
            acc = acc + e_ref[HIST_ROWS - k:HIST_ROWS - k + tm, c0:c0 + POOL_GROUP]
        cnt = jnp.minimum(w, pos + 1).astype(F32)
        out.append(acc / cnt - cur)
    return out


def _prompt_in_kernel(x_ref, ln_ref, wqkvT_ref, wrest_ref, qn_ref, kn_ref, poolw_ref, pscale_ref,
                      kT_ref, vT_ref, qTb_ref, kTb_ref, vTb_ref, bias_ref, b_ref, sga_ref, sgb_ref,
                      hist_ref, e_scr, km_scr, *, tm, n_blocks):
    i = pl.program_id(1)
    n_i = pl.num_programs(1)
    blocks_per_tile = tm // MOBA_BLOCK

    @pl.when(i == 0)
    def _():
        e_scr[0:HIST_ROWS, :] = jnp.zeros((HIST_ROWS, POOL_WIDTH), F32)
        km_scr[...] = jnp.zeros(km_scr.shape, F32)

    xn = _rms_rows(x_ref[...], ln_ref[...]).astype(BF16)

    def proj_t(r0):
        t = lax.dot_general(wqkvT_ref[r0:r0 + ATTN_WIDTH, :], xn, _NT, preferred_element_type=F32)
        return t.reshape(N_HEADS, HEAD_DIM, tm)

    def head_norm(t, g_col):
        ms = jnp.mean(t * t, axis=1, keepdims=True)
        return t * lax.rsqrt(ms + EPS) * g_col[None]

    q = head_norm(proj_t(0), qn_ref[...])
    k = head_norm(proj_t(ATTN_WIDTH), kn_ref[...])
    v = proj_t(2 * ATTN_WIDTH)
    kT_ref[...] = k
    vT_ref[...] = v
    qTb_ref[...] = (q * ATTN_SCALE).astype(BF16)
    kTb_ref[...] = k.astype(BF16)
    vTb_ref[...] = v.astype(BF16)

    lane = lax.broadcasted_iota(jnp.int32, (1, 1, LANES), 2)
    km = km_scr[...]
    for h in range(blocks_per_tile):
        ks = jnp.sum(k[:, :, h * MOBA_BLOCK:(h + 1) * MOBA_BLOCK], axis=2, keepdims=True) * (1.0 / MOBA_BLOCK)
        km = jnp.where(lane == i * blocks_per_tile + h, ks, km)
    km_scr[...] = km

    pos = i * tm + lax.broadcasted_iota(jnp.int32, (1, tm), 1)
    own = pos // MOBA_BLOCK
    blk = lax.broadcasted_iota(jnp.int32, (n_blocks, 1), 0)
    for h in range(N_HEADS):
        km_hi, km_lo = _split_bf16(km[h])
        q_hi, q_lo = _split_bf16(q[h])
        gate = (lax.dot_general(km_hi, q_hi, _TN, preferred_element_type=F32)
                + lax.dot_general(km_hi, q_lo, _TN, preferred_element_type=F32)
                + lax.dot_general(km_lo, q_hi, _TN, preferred_element_type=F32))[0:n_blocks]
        g = jnp.where(blk < own, gate, LOWEST)
        sel = blk == own
        for _ in range(MOBA_TOP_K):
            m = jnp.max(g, axis=0, keepdims=True)
            idx = jnp.min(jnp.where(g == m, blk, n_blocks), axis=0, keepdims=True)
            pick = (blk == idx) & (m > LOWEST)
            sel = sel | pick
            g = jnp.where(pick, LOWEST, g)
        bias_ref[h] = jnp.where(sel, 0.0, NEG_INF).astype(F32)

    u = jnp.dot(xn, wrest_ref[:, 0:POOL_WIDTH], preferred_element_type=F32)
    e_scr[HIST_ROWS:HIST_ROWS + tm, :] = u
    d = _pool_means(e_scr, tm, i * tm)
    for g_i in range(len(POOL_WINDOWS)):
        c0 = g_i * POOL_GROUP
        y = jnp.dot(d[g_i].astype(BF16), poolw_ref[g_i], preferred_element_type=F32)
        b_ref[:, c0:c0 + POOL_GROUP] = (y * pscale_ref[:, c0:c0 + POOL_GROUP]).astype(BF16)
    tail = e_scr[tm:tm + HIST_ROWS, :]
    e_scr[0:HIST_ROWS, :] = tail

    @pl.when(i == n_i - 1)
    def _():
        hist_ref[...] = tail

    ga = jnp.dot(xn, wrest_ref[:, POOL_WIDTH:POOL_WIDTH + D_MODEL], preferred_element_type=F32)
    sga_ref[...] = jax.nn.sigmoid(ga).astype(BF16)
    gb = jnp.dot(xn, wrest_ref[:, POOL_WIDTH + D_MODEL:POOL_WIDTH + 2 * D_MODEL], preferred_element_type=F32)
    sgb_ref[...] = jax.nn.sigmoid(gb).astype(BF16)


def _prompt_in(x, ln, wqkvT, wrest, qn_col, kn_col, poolw, pscale, tm=512):
    bsz, seq, _ = x.shape
    n_blocks = seq // MOBA_BLOCK
    grid = (bsz, seq // tm)
    hT = lambda dt: jax.ShapeDtypeStruct((bsz, N_HEADS, HEAD_DIM, seq), dt)
    hT_spec = pl.BlockSpec((None, N_HEADS, HEAD_DIM, tm), lambda b, i: (b, 0, 0, i))
    row_spec = lambda w: pl.BlockSpec((None, tm, w), lambda b, i: (b, i, 0))
    out_shape = (
        hT(F32), hT(F32), hT(BF16), hT(BF16), hT(BF16),
        jax.ShapeDtypeStruct((bsz, N_HEADS, n_blocks, seq), F32),
        jax.ShapeDtypeStruct((bsz, seq, POOL_WIDTH), BF16),
        jax.ShapeDtypeStruct((bsz, seq, D_MODEL), BF16),
        jax.ShapeDtypeStruct((bsz, seq, D_MODEL), BF16),
        jax.ShapeDtypeStruct((bsz, HIST_ROWS, POOL_WIDTH), F32),
    )
    out_specs = (
        hT_spec, hT_spec, hT_spec, hT_spec, hT_spec,
        pl.BlockSpec((None, N_HEADS, n_blocks, tm), lambda b, i: (b, 0, 0, i)),
        row_spec(POOL_WIDTH), row_spec(D_MODEL), row_spec(D_MODEL),
        pl.BlockSpec((None, HIST_ROWS, POOL_WIDTH), lambda b, i: (b, 0, 0)),
    )
    in_specs = [
        row_spec(D_MODEL),
        _const_spec(ln.shape), _const_spec(wqkvT.shape), _const_spec(wrest.shape),
        _const_spec(qn_col.shape), _const_spec(kn_col.shape),
        _const_spec(poolw.shape), _const_spec(pscale.shape),
    ]
    return pl.pallas_call(
        functools.partial(_prompt_in_kernel, tm=tm, n_blocks=n_blocks),
        grid=grid, in_specs=in_specs, out_specs=out_specs, out_shape=out_shape,
        scratch_shapes=[pltpu.VMEM((HIST_ROWS + tm, POOL_WIDTH), F32),
                        pltpu.VMEM((N_HEADS, HEAD_DIM, LANES), F32)],
        compiler_params=pltpu.CompilerParams(
            dimension_semantics=("arbitrary", "arbitrary"), vmem_limit_bytes=VMEM_LIMIT),
        name="prompt_in",
    )(x, ln, wqkvT, wrest, qn_col, kn_col, poolw, pscale)


def _prompt_attn_kernel(qT_ref, kT_ref, vT_ref, bias_ref, o_ref, *, tq):
    t = pl.program_id(2)
    blocks_per_tile = tq // MOBA_BLOCK
    q = qT_ref[...]
    key_row = lax.broadcasted_iota(jnp.int32, (MOBA_BLOCK, 1), 0)
    q_pos = t * tq + lax.broadcasted_iota(jnp.int32, (1, tq), 1)

    def step(j, carry, causal):
        m, l, acc = carry
        c0 = pl.multiple_of(j * MOBA_BLOCK, MOBA_BLOCK)
        kj = kT_ref[:, pl.ds(c0, MOBA_BLOCK)]
        vj = vT_ref[:, pl.ds(c0, MOBA_BLOCK)]
        s = lax.dot_general(kj, q, _TN, preferred_element_type=F32)
        s = s + bias_ref[pl.ds(j, 1), :]
        if causal:
            s = jnp.where(j * MOBA_BLOCK + key_row <= q_pos, s, NEG_INF)
        m_new = jnp.maximum(m, jnp.max(s, axis=0, keepdims=True))
        alpha = jnp.exp(m - m_new)
        p = jnp.exp(s - m_new)
        l = alpha * l + jnp.sum(p, axis=0, keepdims=True)
        acc = alpha * acc + jnp.dot(vj, p.astype(BF16), preferred_element_type=F32)
        return m_new, l, acc

    carry = (jnp.full((1, tq), NEG_INF, F32), jnp.zeros((1, tq), F32), jnp.zeros((HEAD_DIM, tq), F32))
    carry = lax.fori_loop(0, t * blocks_per_tile, functools.partial(step, causal=False), carry)
    for h in range(blocks_per_tile):
        carry = step(t * blocks_per_tile + h, carry, True)
    _, l, acc = carry
    o_ref[...] = (acc / l).astype(o_ref.dtype)


def _prompt_attn(qT, kT, vT, bias, tq=512):
    bsz, _, _, seq = qT.shape
    n_blocks = seq // MOBA_BLOCK
    grid = (bsz, N_HEADS, seq // tq)
    tile = pl.BlockSpec((None, None, HEAD_DIM, tq), lambda b, h, t: (b, h, 0, t))
    full = pl.BlockSpec((None, None, HEAD_DIM, seq), lambda b, h, t: (b, h, 0, 0))
    return pl.pallas_call(
        functools.partial(_prompt_attn_kernel, tq=tq),
        grid=grid,
        in_specs=[tile, full, full,
                  pl.BlockSpec((None, None, n_blocks, tq), lambda b, h, t: (b, h, 0, t))],
        out_specs=tile,
        out_shape=jax.ShapeDtypeStruct((bsz, N_HEADS, HEAD_DIM, seq), BF16),
        compiler_params=pltpu.CompilerParams(
            dimension_semantics=("arbitrary", "arbitrary", "arbitrary"), vmem_limit_bytes=VMEM_LIMIT),
        name="prompt_attn",
    )(qT, kT, vT, bias)


def _tail_kernel(x_ref, aT_ref, b_ref, sga_ref, sgb_ref, p_ref, wao_ref, wpo_ref, wo_ref, lnm_ref,
                 wup_ref, wdn_ref, lnp_ref, wpg_ref, wpp_ref, y_ref, *, ff_chunk):
    a_proj = lax.dot_general(aT_ref[...], wao_ref[...], _TN, preferred_element_type=F32)
    b_proj = jnp.dot(b_ref[...], wpo_ref[...], preferred_element_type=F32)
    m = sga_ref[...].astype(F32) * a_proj + sgb_ref[...].astype(F32) * b_proj
    x1 = x_ref[...] + jnp.dot(m.astype(BF16), wo_ref[...], preferred_element_type=F32)
    xn = _rms_rows(x1, lnm_ref[...]).astype(BF16)
    acc = x1
    for c in range(D_FF // ff_chunk):
        hid = jnp.maximum(jnp.dot(xn, wup_ref[:, c * ff_chunk:(c + 1) * ff_chunk],
                                  preferred_element_type=F32), 0.0)
        acc = acc + jnp.dot((hid * hid).astype(BF16), wdn_ref[c * ff_chunk:(c + 1) * ff_chunk, :],
                            preferred_element_type=F32)
    x2 = acc
    xg = _rms_rows(x2, lnp_ref[...]).astype(BF16)
    gate = jax.nn.sigmoid(jnp.dot(xg, wpg_ref[...], preferred_element_type=F32))
    emb = jnp.dot(p_ref[...].astype(BF16), wpp_ref[...], preferred_element_type=F32)
    y_ref[...] = x2 + gate * emb


def _tail(x, aT, b, sga, sgb, p, wao, wpo, wo, lnm, wup, wdn, lnp, wpg, wpp, tm, ff_chunk=1024):
    bsz, seq, _ = x.shape
    grid = (bsz, seq // tm)
    row_spec = lambda w: pl.BlockSpec((None, tm, w), lambda b_, i: (b_, i, 0))
    in_specs = [
        row_spec(D_MODEL),
        pl.BlockSpec((None, ATTN_WIDTH, tm), lambda b_, i: (b_, 0, i)),
        row_spec(POOL_WIDTH), row_spec(D_MODEL), row_spec(D_MODEL), row_spec(PLE_DIM),
    ] + [_const_spec(w.shape) for w in (wao, wpo, wo, lnm, wup, wdn, lnp, wpg, wpp)]
    return pl.pallas_call(
        functools.partial(_tail_kernel, ff_chunk=ff_chunk),
        grid=grid, in_specs=in_specs, out_specs=row_spec(D_MODEL),
        out_shape=jax.ShapeDtypeStruct((bsz, seq, D_MODEL), F32),
        compiler_params=pltpu.CompilerParams(
            dimension_semantics=("arbitrary", "arbitrary"), vmem_limit_bytes=VMEM_LIMIT),
        name="tail",
    )(x, aT, b, sga, sgb, p, wao, wpo, wo, lnm, wup, wdn, lnp, wpg, wpp)


def _sample_in_kernel(x_ref, ln_ref, win_ref, qn_ref, kn_ref, hist_ref, poolw_ref, pscale_ref,
                      q_ref, k_ref, v_ref, b_ref, sga_ref, sgb_ref, hist_out_ref, *, past_len):
    xn = _rms_rows(x_ref[...], ln_ref[...]).astype(BF16)
    z = jnp.dot(xn, win_ref[...], preferred_element_type=F32)
    for h in range(N_HEADS):
        c0 = h * HEAD_DIM
        q_ref[:, c0:c0 + HEAD_DIM] = _rms_rows(z[:, c0:c0 + HEAD_DIM], qn_ref[...])
        k_ref[:, c0:c0 + HEAD_DIM] = _rms_rows(
            z[:, ATTN_WIDTH + c0:ATTN_WIDTH + c0 + HEAD_DIM], kn_ref[...])
    v_ref[...] = z[:, 2 * ATTN_WIDTH:3 * ATTN_WIDTH]
    u0 = 3 * ATTN_WIDTH
    u = z[:, u0:u0 + POOL_WIDTH]
    for g_i, w in enumerate(POOL_WINDOWS):
        c0 = g_i * POOL_GROUP
        cur = u[:, c0:c0 + POOL_GROUP]
        acc = cur
        for k in range(1, w):
            acc = acc + hist_ref[POOL_HIST - k, :, c0:c0 + POOL_GROUP]
        cnt = float(min(w, past_len + 1))
        d = acc / cnt - cur
        y = jnp.dot(d.astype(BF16), poolw_ref[g_i], preferred_element_type=F32)
        b_ref[:, c0:c0 + POOL_GROUP] = (y * pscale_ref[:, c0:c0 + POOL_GROUP]).astype(BF16)
    for t in range(POOL_HIST - 1):
        hist_out_ref[t] = hist_ref[t + 1]
    hist_out_ref[POOL_HIST - 1] = u
    g0 = u0 + POOL_WIDTH
    sga_ref[...] = jax.nn.sigmoid(z[:, g0:g0 + D_MODEL]).astype(BF16)
    sgb_ref[...] = jax.nn.sigmoid(z[:, g0 + D_MODEL:g0 + 2 * D_MODEL]).astype(BF16)


def _sample_in(x, ln, win, qn_row, kn_row, hist_t, poolw, pscale, past_len):
    n = x.shape[0]
    sds = jax.ShapeDtypeStruct
    out_shape = (sds((n, ATTN_WIDTH), F32), sds((n, ATTN_WIDTH), F32), sds((n, ATTN_WIDTH), F32),
                 sds((n, POOL_WIDTH), BF16), sds((n, D_MODEL), BF16), sds((n, D_MODEL), BF16),
                 sds(hist_t.shape, F32))
    return pl.pallas_call(
        functools.partial(_sample_in_kernel, past_len=past_len),
        out_shape=out_shape,
        compiler_params=pltpu.CompilerParams(vmem_limit_bytes=VMEM_LIMIT),
        name="sample_in",
    )(x, ln, win, qn_row, kn_row, hist_t, poolw, pscale)


_GATE_BUFS = 4


def _sample_gate_kernel(pt_ref, q_ref, kc_ref, sel_ref, kbuf, sem, *, n_pages):
    b = pl.program_id(0)
    n_b = pl.num_programs(0)
    n_blocks = n_pages // PAGES_PER_BLOCK

    def page_copy(seq_i, blk_i, slot, pg):
        page = pt_ref[seq_i * n_pages + blk_i * PAGES_PER_BLOCK + pg]
        return pltpu.make_async_copy(kc_ref.at[page], kbuf.at[slot, pg], sem.at[slot, pg])

    def start(seq_i, blk_i, slot):
        for pg in range(PAGES_PER_BLOCK):
            page_copy(seq_i, blk_i, slot, pg).start()

    @pl.when(b == 0)
    def _():
        for j in range(_GATE_BUFS):
            start(0, j, j)

    qb = jnp.broadcast_to(q_ref[...], (N_HEADS, HEAD_DIM, LANES))
    lane = lax.broadcasted_iota(jnp.int32, (1, LANES), 1)

    def body(j, gate):
        slot = j % _GATE_BUFS
        for pg in range(PAGES_PER_BLOCK):
            page_copy(b, j, slot, pg).wait()
        s = jnp.sum(kbuf[slot, 0] * qb, axis=1)
        for pg in range(1, PAGES_PER_BLOCK):
            s = s + jnp.sum(kbuf[slot, pg] * qb, axis=1)
        ssum = jnp.sum(s, axis=1, keepdims=True) * (1.0 / MOBA_BLOCK)
        gate = jnp.where(lane == j, ssum, gate)
        nxt = j + _GATE_BUFS

        @pl.when(nxt < n_blocks)
        def _():
            start(b, nxt, slot)

        @pl.when((nxt >= n_blocks) & (b + 1 < n_b))
        def _():
            start(b + 1, nxt - n_blocks, slot)

        return gate

    gate = lax.fori_loop(0, n_blocks, body, jnp.zeros((N_HEADS, LANES), F32))
    g = jnp.where(lane < n_blocks, gate, LOWEST)
    sel = jnp.zeros((N_HEADS, LANES), jnp.int32)
    for r in range(MOBA_TOP_K):
        m = jnp.max(g, axis=1, keepdims=True)
        idx = jnp.min(jnp.where(g == m, lane, LANES), axis=1, keepdims=True)
        sel = jnp.where(lane == r, idx, sel)
        g = jnp.where(lane == idx, LOWEST, g)
    sel_ref[...] = sel


def _sample_gate(page_table_flat, q_col, cache_kT, n_seq, n_pages):
    grid_spec = pltpu.PrefetchScalarGridSpec(
        num_scalar_prefetch=1,
        grid=(n_seq,),
        in_specs=[pl.BlockSpec((None, N_HEADS, HEAD_DIM, 1), lambda b, pt: (b, 0, 0, 0)),
                  pl.BlockSpec(memory_space=pl.ANY)],
        out_specs=pl.BlockSpec((None, N_HEADS, LANES), lambda b, pt: (b, 0, 0)),
        scratch_shapes=[pltpu.VMEM((_GATE_BUFS, PAGES_PER_BLOCK, N_HEADS, HEAD_DIM, PAGE_SIZE), F32),
                        pltpu.SemaphoreType.DMA((_GATE_BUFS, PAGES_PER_BLOCK))],
    )
    return pl.pallas_call(
        functools.partial(_sample_gate_kernel, n_pages=n_pages),
        grid_spec=grid_spec,
        out_shape=jax.ShapeDtypeStruct((n_seq, N_HEADS, LANES), jnp.int32),
        compiler_params=pltpu.CompilerParams(
            dimension_semantics=("arbitrary",), vmem_limit_bytes=VMEM_LIMIT),
        name="sample_gate",
    )(page_table_flat, q_col, cache_kT)


def _sample_attn_kernel(pt_ref, sel_ref, q_ref, kn_ref, vn_ref, kc_ref, vc_ref, o_ref,
                        kbuf, vbuf, sem, *, n_pages):
    b = pl.program_id(0)
    n_b = pl.num_programs(0)
    n_slots = MOBA_TOP_K * PAGES_PER_BLOCK

    def copies(seq_i, buf_i):
        out = []
        for h in range(N_HEADS):
            for r in range(MOBA_TOP_K):
                blk = sel_ref[(seq_i * N_HEADS + h) * MOBA_TOP_K + r]
                for pg in range(PAGES_PER_BLOCK):
                    page = pt_ref[seq_i * n_pages + blk * PAGES_PER_BLOCK + pg]
                    s_i = r * PAGES_PER_BLOCK + pg
                    out.append(pltpu.make_async_copy(kc_ref.at[page, h], kbuf.at[buf_i, h, s_i], sem.at[buf_i, 0]))
                    out.append(pltpu.make_async_copy(vc_ref.at[page, h], vbuf.at[buf_i, h, s_i], sem.at[buf_i, 1]))
        return out

    @pl.when(b == 0)
    def _():
        for c in copies(0, 0):
            c.start()

    cur = b % 2

    @pl.when(b + 1 < n_b)
    def _():
        for c in copies(b + 1, 1 - cur):
            c.start()

    for c in copies(b, cur):
        c.wait()

    for h in range(N_HEADS):
        qh = q_ref[h]
        kb = kbuf[cur, h]
        vb = vbuf[cur, h]
        s = jnp.sum(kb * qh[None], axis=1) * ATTN_SCALE
        s_new = jnp.sum(qh * kn_ref[h], axis=0, keepdims=True) * ATTN_SCALE
        m = jnp.maximum(jnp.max(jnp.max(s, axis=1, keepdims=True), axis=0, keepdims=True), s_new)
        p = jnp.exp(s - m)
        p_new = jnp.exp(s_new - m)
        l = jnp.sum(jnp.sum(p, axis=1, keepdims=True), axis=0, keepdims=True) + p_new
        pv = vb[0] * p[0:1]
        for s_i in range(1, n_slots):
            pv = pv + vb[s_i] * p[s_i:s_i + 1]
        o = jnp.sum(pv, axis=1, keepdims=True) + p_new * vn_ref[h]
        o_ref[h] = o / l


def _sample_attn(page_table_flat, sel_flat, q_col, kn_col, vn_col, cache_kT, cache_vT, n_seq, n_pages):
    col = pl.BlockSpec((None, N_HEADS, HEAD_DIM, 1), lambda b, pt, sel: (b, 0, 0, 0))
    n_slots = MOBA_TOP_K * PAGES_PER_BLOCK
    buf = pltpu.VMEM((2, N_HEADS, n_slots, HEAD_DIM, PAGE_SIZE), F32)
    grid_spec = pltpu.PrefetchScalarGridSpec(
        num_scalar_prefetch=2,
        grid=(n_seq,),
        in_specs=[col, col, col, pl.BlockSpec(memory_space=pl.ANY), pl.BlockSpec(memory_space=pl.ANY)],
        out_specs=col,
        scratch_shapes=[buf, buf, pltpu.SemaphoreType.DMA((2, 2))],
    )
    return pl.pallas_call(
        functools.partial(_sample_attn_kernel, n_pages=n_pages),
        grid_spec=grid_spec,
        out_shape=jax.ShapeDtypeStruct((n_seq, N_HEADS, HEAD_DIM, 1), F32),
        compiler_params=pltpu.CompilerParams(
            dimension_semantics=("arbitrary",), vmem_limit_bytes=VMEM_LIMIT),
        name="sample_attn",
    )(page_table_flat, sel_flat, q_col, kn_col, vn_col, cache_kT, cache_vT)


def kernel(x_prompt, x_sample, cache_k, cache_v, state_pool, page_table, p_prompt, p_sample, ln_mix, w_in,
           q_norm, k_norm, pool_w, pool_scale, w_attn_out, w_pool_out, w_out, ln_mlp, w_up, w_down, ln_ple,
           w_ple_gate, w_ple_proj):
    depth = w_in.shape[0]
    assert depth == 1
    n_seq, n_pages = page_table.shape
    past_len = n_pages * PAGE_SIZE
    assert x_sample.shape[1] == 1 and past_len % MOBA_BLOCK == 0 and past_len // MOBA_BLOCK >= MOBA_TOP_K
    bsz, seq, _ = x_prompt.shape
    l = 0

    w_in_b = w_in[l].astype(BF16)
    wqkvT = w_in_b[:, :3 * ATTN_WIDTH].T
    wrest = w_in_b[:, 3 * ATTN_WIDTH:]
    poolw = pool_w[l].astype(BF16)
    tail_w = (w_attn_out[l].astype(BF16), w_pool_out[l].astype(BF16), w_out[l].astype(BF16), ln_mlp,
              w_up[l].astype(BF16), w_down[l].astype(BF16), ln_ple, w_ple_gate[l].astype(BF16),
              w_ple_proj[l].astype(BF16))

    (kT, vT, qTb, kTb, vTb, bias, b_p, sga, sgb, hist_p) = _prompt_in(
        x_prompt, ln_mix, wqkvT, wrest, q_norm.reshape(HEAD_DIM, 1), k_norm.reshape(HEAD_DIM, 1),
        poolw, pool_scale)
    aT = _prompt_attn(qTb, kTb, vTb, bias).reshape(bsz, ATTN_WIDTH, seq)
    y_prompt = _tail(x_prompt, aT, b_p, sga, sgb, p_prompt[l], *tail_w, tm=512)
    k_prompt = kT.transpose(0, 3, 1, 2)[None]
    v_prompt = vT.transpose(0, 3, 1, 2)[None]
    pool_prompt = hist_p[:, HIST_ROWS - POOL_HIST:][None]

    xs = x_sample.reshape(n_seq, D_MODEL)
    hist_t = state_pool[l].transpose(1, 0, 2)
    q_s, k_s, v_s, b_s, sga_s, sgb_s, hist_s = _sample_in(
        xs, ln_mix, w_in_b, q_norm, k_norm, hist_t, poolw, pool_scale, past_len)
    cache_kT = cache_k[l].transpose(0, 2, 3, 1)
    cache_vT = cache_v[l].transpose(0, 2, 3, 1)
    pt_flat = page_table.reshape(-1)
    q_col = q_s.reshape(n_seq, N_HEADS, HEAD_DIM, 1)
    sel = _sample_gate(pt_flat, q_col, cache_kT, n_seq, n_pages)
    sel_flat = sel[:, :, :MOBA_TOP_K].reshape(-1)
    a_s = _sample_attn(pt_flat, sel_flat, q_col, k_s.reshape(n_seq, N_HEADS, HEAD_DIM, 1),
                       v_s.reshape(n_seq, N_HEADS, HEAD_DIM, 1), cache_kT, cache_vT, n_seq, n_pages)
    aT_s = a_s.reshape(n_seq, ATTN_WIDTH).T.astype(BF16)[None]
    y_sample = _tail(xs[None], aT_s, b_s[None], sga_s[None], sgb_s[None], p_sample[l].reshape(1, n_seq, PLE_DIM),
                     *tail_w, tm=n_seq)
    y_sample = y_sample.reshape(n_seq, 1, D_MODEL)
    k_sample = k_s.reshape(1, n_seq, 1, N_HEADS, HEAD_DIM)
    v_sample = v_s.reshape(1, n_seq, 1, N_HEADS, HEAD_DIM)
    pool_sample = hist_s.transpose(1, 0, 2)[None]
    return (y_prompt, y_sample, k_prompt, v_prompt, pool_prompt, k_sample, v_sample, pool_sample)
```

```python
import functools

import jax
import jax.numpy as jnp
from jax import lax
from jax.experimental import pallas as pl
from jax.experimental.pallas import tpu as pltpu

D_MODEL = 1024
N_HEADS = 8
HEAD_DIM = 64
ATTN_WIDTH = N_HEADS * HEAD_DIM
MOBA_BLOCK = 256
MOBA_TOP_K = 3
PAGE_SIZE = 128
PAGES_PER_BLOCK = MOBA_BLOCK // PAGE_SIZE
ATTN_SCALE = HEAD_DIM ** -0.5
POOL_WINDOWS = (2, 4, 8, 16)
POOL_GROUP = 128
POOL_WIDTH = 512
POOL_HIST = 15
HIST_ROWS = 16
D_FF = 4 * D_MODEL
PLE_DIM = 256
EPS = 1e-6
NEG_INF = -1e30
LOWEST = -3e38
LOG2E = 1.4426950408889634

LANES = 128
VMEM_LIMIT = 56 * 1024 * 1024

F32 = jnp.float32
BF16 = jnp.bfloat16

_NT = (((1,), (1,)), ((), ()))
_TN = (((0,), (0,)), ((), ()))


def _rms_rows(x, g):
    ms = jnp.mean(x * x, axis=-1, keepdims=True)
    return x * lax.rsqrt(ms + EPS) * g


def _split_bf16(a):
    hi = a.astype(BF16)
    lo = (a - hi.astype(F32)).astype(BF16)
    return hi, lo


def _const_spec(shape):
    zeros = (0,) * len(shape)
    return pl.BlockSpec(shape, lambda *_: zeros, pipeline_mode=pl.Buffered(1))


def _pool_means(e_ref, tm, pos0):
    pos = pos0 + lax.broadcasted_iota(jnp.int32, (tm, 1), 0)
    out = []
    for g, w in enumerate(POOL_WINDOWS):
        c0 = g * POOL_GROUP
        cur = e_ref[HIST_ROWS:HIST_ROWS + tm, c0:c0 + POOL_GROUP]
        acc = cur
        for k in range(1, w):
            acc = acc + e_ref[HIST_ROWS - k:HIST_ROWS - k + tm, c0:c0 + POOL_GROUP]
        cnt = jnp.minimum(w, pos + 1).astype(F32)
        out.append(acc / cnt - cur)
    return out


def _prompt_in_kernel(x_ref, ln_ref, wqkvT_ref, wrest_ref, qn_ref, kn_ref, poolw_ref, pscale_ref,
                      kT_ref, vT_ref, qa_ref, ka_ref, vTb_ref, b_ref, sga_ref, sgb_ref,
                      hist_ref, e_scr, km_scr, *, tm, n_blocks):
    i = pl.program_id(1)
    n_i = pl.num_programs(1)
    blocks_per_tile = tm // MOBA_BLOCK

    @pl.when(i == 0)
    def _():
        e_scr[0:HIST_ROWS, :] = jnp.zeros((HIST_ROWS, POOL_WIDTH), F32)
        km_scr[...] = jnp.zeros(km_scr.shape, F32)

    xn = _rms_rows(x_ref[...], ln_ref[...]).astype(BF16)

    def proj_t(r0):
        t = lax.dot_general(wqkvT_ref[r0:r0 + ATTN_WIDTH, :], xn, _NT, preferred_element_type=F32)
        return t.reshape(N_HEADS, HEAD_DIM, tm)

    def head_norm(t, g_col):
        ms = jnp.mean(t * t, axis=1, keepdims=True)
        return t * lax.rsqrt(ms + EPS) * g_col[None]

    q = head_norm(proj_t(0), qn_ref[...])
    k = head_norm(proj_t(ATTN_WIDTH), kn_ref[...])
    v = proj_t(2 * ATTN_WIDTH)
    kT_ref[...] = k
    vT_ref[...] = v
    vTb_ref[...] = v.astype(BF16)
    qa_ref[:, 0:HEAD_DIM, :] = (q * (ATTN_SCALE * LOG2E)).astype(BF16)
    key_blk = (i * tm + lax.broadcasted_iota(jnp.int32, (tm, 1), 0)) // MOBA_BLOCK
    onehot = jnp.where(key_blk == lax.broadcasted_iota(jnp.int32, (1, n_blocks), 1), 1.0, 0.0).astype(BF16)
    for h in range(N_HEADS):
        ka_ref[h, :, 0:HEAD_DIM] = k[h].T.astype(BF16)
        ka_ref[h, :, HEAD_DIM:HEAD_DIM + n_blocks] = onehot

    lane = lax.broadcasted_iota(jnp.int32, (1, 1, LANES), 2)
    km = km_scr[...]
    for h in range(blocks_per_tile):
        ks = jnp.sum(k[:, :, h * MOBA_BLOCK:(h + 1) * MOBA_BLOCK], axis=2, keepdims=True) * (1.0 / MOBA_BLOCK)
        km = jnp.where(lane == i * blocks_per_tile + h, ks, km)
    km_scr[...] = km

    pos = i * tm + lax.broadcasted_iota(jnp.int32, (1, tm), 1)
    own = pos // MOBA_BLOCK
    blk = lax.broadcasted_iota(jnp.int32, (n_blocks, 1), 0)
    for h in range(N_HEADS):
        km_hi, km_lo = _split_bf16(km[h])
        q_hi, q_lo = _split_bf16(q[h])
        gate = (lax.dot_general(km_hi, q_hi, _TN, preferred_element_type=F32)
                + lax.dot_general(km_hi, q_lo, _TN, preferred_element_type=F32)
                + lax.dot_general(km_lo, q_hi, _TN, preferred_element_type=F32))[0:n_blocks]
        g = jnp.where(blk < own, gate, LOWEST)
        sel = blk == own
        for _ in range(MOBA_TOP_K):
            m = jnp.max(g, axis=0, keepdims=True)
            idx = jnp.min(jnp.where(g == m, blk, n_blocks), axis=0, keepdims=True)
            pick = (blk == idx) & (m > LOWEST)
            sel = sel | pick
            g = jnp.where(pick, LOWEST, g)
        qa_ref[h, HEAD_DIM:HEAD_DIM + n_blocks, :] = jnp.where(sel, 0.0, NEG_INF).astype(BF16)

    u = jnp.dot(xn, wrest_ref[:, 0:POOL_WIDTH], preferred_element_type=F32)
    e_scr[HIST_ROWS:HIST_ROWS + tm, :] = u
    d = _pool_means(e_scr, tm, i * tm)
    for g_i in range(len(POOL_WINDOWS)):
        c0 = g_i * POOL_GROUP
        y = jnp.dot(d[g_i].astype(BF16), poolw_ref[g_i], preferred_element_type=F32)
        b_ref[:, c0:c0 + POOL_GROUP] = (y * pscale_ref[:, c0:c0 + POOL_GROUP]).astype(BF16)
    tail = e_scr[tm:tm + HIST_ROWS, :]
    e_scr[0:HIST_ROWS, :] = tail

    @pl.when(i == n_i - 1)
    def _():
        hist_ref[...] = tail

    ga = jnp.dot(xn, wrest_ref[:, POOL_WIDTH:POOL_WIDTH + D_MODEL], preferred_element_type=F32)
    sga_ref[...] = jax.nn.sigmoid(ga).astype(BF16)
    gb = jnp.dot(xn, wrest_ref[:, POOL_WIDTH + D_MODEL:POOL_WIDTH + 2 * D_MODEL], preferred_element_type=F32)
    sgb_ref[...] = jax.nn.sigmoid(gb).astype(BF16)


def _prompt_in(x, ln, wqkvT, wrest, qn_col, kn_col, poolw, pscale, tm=512):
    bsz, seq, _ = x.shape
    n_blocks = seq // MOBA_BLOCK
    grid = (bsz, seq // tm)
    hT = lambda dt: jax.ShapeDtypeStruct((bsz, N_HEADS, HEAD_DIM, seq), dt)
    hT_spec = pl.BlockSpec((None, N_HEADS, HEAD_DIM, tm), lambda b, i: (b, 0, 0, i))
    row_spec = lambda w: pl.BlockSpec((None, tm, w), lambda b, i: (b, i, 0))
    aug = HEAD_DIM + n_blocks
    out_shape = (
        hT(F32), hT(F32),
        jax.ShapeDtypeStruct((bsz, N_HEADS, aug, seq), BF16),
        jax.ShapeDtypeStruct((bsz, N_HEADS, seq, aug), BF16),
        hT(BF16),
        jax.ShapeDtypeStruct((bsz, seq, POOL_WIDTH), BF16),
        jax.ShapeDtypeStruct((bsz, seq, D_MODEL), BF16),
        jax.ShapeDtypeStruct((bsz, seq, D_MODEL), BF16),
        jax.ShapeDtypeStruct((bsz, HIST_ROWS, POOL_WIDTH), F32),
    )
    out_specs = (
        hT_spec, hT_spec,
        pl.BlockSpec((None, N_HEADS, aug, tm), lambda b, i: (b, 0, 0, i)),
        pl.BlockSpec((None, N_HEADS, tm, aug), lambda b, i: (b, 0, i, 0)),
        hT_spec,
        row_spec(POOL_WIDTH), row_spec(D_MODEL), row_spec(D_MODEL),
        pl.BlockSpec((None, HIST_ROWS, POOL_WIDTH), lambda b, i: (b, 0, 0)),
    )
    in_specs = [
        row_spec(D_MODEL),
        _const_spec(ln.shape), _const_spec(wqkvT.shape), _const_spec(wrest.shape),
        _const_spec(qn_col.shape), _const_spec(kn_col.shape),
        _const_spec(poolw.shape), _const_spec(pscale.shape),
    ]
    return pl.pallas_call(
        functools.partial(_prompt_in_kernel, tm=tm, n_blocks=n_blocks),
        grid=grid, in_specs=in_specs, out_specs=out_specs, out_shape=out_shape,
        scratch_shapes=[pltpu.VMEM((HIST_ROWS + tm, POOL_WIDTH), F32),
                        pltpu.VMEM((N_HEADS, HEAD_DIM, LANES), F32)],
        compiler_params=pltpu.CompilerParams(
            dimension_semantics=("arbitrary", "arbitrary"), vmem_limit_bytes=VMEM_LIMIT),
        name="prompt_in",
    )(x, ln, wqkvT, wrest, qn_col, kn_col, poolw, pscale)


def _prompt_attn_kernel(qa_ref, ka_ref, vT_ref, o_ref, m_scr, l_scr, acc_scr, s_scr, *, heads):
    t = pl.program_id(2)
    blk = MOBA_BLOCK
    tri = (lax.broadcasted_iota(jnp.int32, (blk, 1), 0) <= lax.broadcasted_iota(jnp.int32, (1, blk), 1))
    m_scr[...] = jnp.full(m_scr.shape, NEG_INF, F32)
    l_scr[...] = jnp.zeros(l_scr.shape, F32)
    acc_scr[...] = jnp.zeros(acc_scr.shape, F32)

    def block(j, diag):
        r0 = pl.multiple_of(j * blk, blk)
        cmax = []
        for g in range(heads):
            s = jnp.dot(ka_ref[g, pl.ds(r0, blk), :], qa_ref[g], preferred_element_type=F32)
            if diag:
                s = jnp.where(tri, s, NEG_INF)
            s_scr[g] = s
            cmax.append(jnp.max(s, axis=0, keepdims=True))
        for g in range(heads):
            s = s_scr[g]
            m = m_scr[g]
            m_new = jnp.maximum(m, cmax[g])
            alpha = jnp.exp2(m - m_new)
            p = jnp.exp2(s - m_new)
            m_scr[g] = m_new
            l_scr[g] = alpha * l_scr[g] + jnp.sum(p, axis=0, keepdims=True)
            pv = jnp.dot(vT_ref[g, :, pl.ds(r0, blk)], p.astype(BF16), preferred_element_type=F32)
            acc_scr[g] = alpha * acc_scr[g] + pv

    def body(j, c):
        block(j, False)
        return c

    lax.fori_loop(0, t, body, 0)
    block(t, True)
    for g in range(heads):
        o_ref[g] = (acc_scr[g] / l_scr[g]).astype(o_ref.dtype)


def _prompt_attn(qa, ka, vT, heads=N_HEADS):
    bsz, _, aug, seq = qa.shape
    blk = MOBA_BLOCK
    grid = (bsz, N_HEADS // heads, seq // blk)
    resident = lambda r, c: pl.BlockSpec((None, heads, r, c), lambda b, h, t: (b, h, 0, 0),
                                         pipeline_mode=pl.Buffered(1))
    return pl.pallas_call(
        functools.partial(_prompt_attn_kernel, heads=heads),
        grid=grid,
        in_specs=[pl.BlockSpec((None, heads, aug, blk), lambda b, h, t: (b, h, 0, t)),
                  resident(seq, aug), resident(HEAD_DIM, seq)],
        out_specs=pl.BlockSpec((None, heads, HEAD_DIM, blk), lambda b, h, t: (b, h, 0, t)),
        out_shape=jax.ShapeDtypeStruct((bsz, N_HEADS, HEAD_DIM, seq), BF16),
        scratch_shapes=[pltpu.VMEM((heads, 1, blk), F32), pltpu.VMEM((heads, 1, blk), F32),
                        pltpu.VMEM((heads, HEAD_DIM, blk), F32), pltpu.VMEM((heads, blk, blk), F32)],
        compiler_params=pltpu.CompilerParams(
            dimension_semantics=("arbitrary", "arbitrary", "arbitrary"), vmem_limit_bytes=VMEM_LIMIT),
        name="prompt_attn",
    )(qa, ka, vT)


def _tail_kernel(x_ref, aT_ref, b_ref, sga_ref, sgb_ref, p_ref, wao_ref, wpo_ref, wo_ref, lnm_ref,
                 wup_ref, wdn_ref, lnp_ref, wpg_ref, wpp_ref, y_ref, *, ff_chunk):
    a_proj = lax.dot_general(aT_ref[...], wao_ref[...], _TN, preferred_element_type=F32)
    b_proj = jnp.dot(b_ref[...], wpo_ref[...], preferred_element_type=F32)
    m = sga_ref[...].astype(F32) * a_proj + sgb_ref[...].astype(F32) * b_proj
    x1 = x_ref[...] + jnp.dot(m.astype(BF16), wo_ref[...], preferred_element_type=F32)
    xn = _rms_rows(x1, lnm_ref[...]).astype(BF16)
    acc = x1
    for c in range(D_FF // ff_chunk):
        hid = jnp.maximum(jnp.dot(xn, wup_ref[:, c * ff_chunk:(c + 1) * ff_chunk],
                                  preferred_element_type=F32), 0.0)
        acc = acc + jnp.dot((hid * hid).astype(BF16), wdn_ref[c * ff_chunk:(c + 1) * ff_chunk, :],
                            preferred_element_type=F32)
    x2 = acc
    xg = _rms_rows(x2, lnp_ref[...]).astype(BF16)
    gate = jax.nn.sigmoid(jnp.dot(xg, wpg_ref[...], preferred_element_type=F32))
    emb = jnp.dot(p_ref[...].astype(BF16), wpp_ref[...], preferred_element_type=F32)
    y_ref[...] = x2 + gate * emb


def _tail(x, aT, b, sga, sgb, p, wao, wpo, wo, lnm, wup, wdn, lnp, wpg, wpp, tm, ff_chunk=1024):
    bsz, seq, _ = x.shape
    grid = (bsz, seq // tm)
    row_spec = lambda w: pl.BlockSpec((None, tm, w), lambda b_, i: (b_, i, 0))
    in_specs = [
        row_spec(D_MODEL),
        pl.BlockSpec((None, ATTN_WIDTH, tm), lambda b_, i: (b_, 0, i)),
        row_spec(POOL_WIDTH), row_spec(D_MODEL), row_spec(D_MODEL), row_spec(PLE_DIM),
    ] + [_const_spec(w.shape) for w in (wao, wpo, wo, lnm, wup, wdn, lnp, wpg, wpp)]
    return pl.pallas_call(
        functools.partial(_tail_kernel, ff_chunk=ff_chunk),
        grid=grid, in_specs=in_specs, out_specs=row_spec(D_MODEL),
        out_shape=jax.ShapeDtypeStruct((bsz, seq, D_MODEL), F32),
        compiler_params=pltpu.CompilerParams(
            dimension_semantics=("arbitrary", "arbitrary"), vmem_limit_bytes=VMEM_LIMIT),
        name="tail",
    )(x, aT, b, sga, sgb, p, wao, wpo, wo, lnm, wup, wdn, lnp, wpg, wpp)


def _sample_in_kernel(x_ref, ln_ref, win_ref, qn_ref, kn_ref, hist_ref, poolw_ref, pscale_ref,
                      q_ref, k_ref, v_ref, b_ref, sga_ref, sgb_ref, hist_out_ref, *, past_len):
    xn = _rms_rows(x_ref[...], ln_ref[...]).astype(BF16)
    z = jnp.dot(xn, win_ref[...], preferred_element_type=F32)
    for h in range(N_HEADS):
        c0 = h * HEAD_DIM
        q_ref[:, c0:c0 + HEAD_DIM] = _rms_rows(z[:, c0:c0 + HEAD_DIM], qn_ref[...])
        k_ref[:, c0:c0 + HEAD_DIM] = _rms_rows(
            z[:, ATTN_WIDTH + c0:ATTN_WIDTH + c0 + HEAD_DIM], kn_ref[...])
    v_ref[...] = z[:, 2 * ATTN_WIDTH:3 * ATTN_WIDTH]
    u0 = 3 * ATTN_WIDTH
    u = z[:, u0:u0 + POOL_WIDTH]
    for g_i, w in enumerate(POOL_WINDOWS):
        c0 = g_i * POOL_GROUP
        cur = u[:, c0:c0 + POOL_GROUP]
        acc = cur
        for k in range(1, w):
            acc = acc + hist_ref[POOL_HIST - k, :, c0:c0 + POOL_GROUP]
        cnt = float(min(w, past_len + 1))
        d = acc / cnt - cur
        y = jnp.dot(d.astype(BF16), poolw_ref[g_i], preferred_element_type=F32)
        b_ref[:, c0:c0 + POOL_GROUP] = (y * pscale_ref[:, c0:c0 + POOL_GROUP]).astype(BF16)
    for t in range(POOL_HIST - 1):
        hist_out_ref[t] = hist_ref[t + 1]
    hist_out_ref[POOL_HIST - 1] = u
    g0 = u0 + POOL_WIDTH
    sga_ref[...] = jax.nn.sigmoid(z[:, g0:g0 + D_MODEL]).astype(BF16)
    sgb_ref[...] = jax.nn.sigmoid(z[:, g0 + D_MODEL:g0 + 2 * D_MODEL]).astype(BF16)


def _sample_in(x, ln, win, qn_row, kn_row, hist_t, poolw, pscale, past_len):
    n = x.shape[0]
    sds = jax.ShapeDtypeStruct
    out_shape = (sds((n, ATTN_WIDTH), F32), sds((n, ATTN_WIDTH), F32), sds((n, ATTN_WIDTH), F32),
                 sds((n, POOL_WIDTH), BF16), sds((n, D_MODEL), BF16), sds((n, D_MODEL), BF16),
                 sds(hist_t.shape, F32))
    return pl.pallas_call(
        functools.partial(_sample_in_kernel, past_len=past_len),
        out_shape=out_shape,
        compiler_params=pltpu.CompilerParams(vmem_limit_bytes=VMEM_LIMIT),
        name="sample_in",
    )(x, ln, win, qn_row, kn_row, hist_t, poolw, pscale)


_GATE_BUFS = 16


def _sample_gate_kernel(pt_ref, q_ref, kc_ref, sel_ref, kbuf, sem, s_scr, *, n_pages):
    b = pl.program_id(0)
    n_b = pl.num_programs(0)
    n_blocks = n_pages // PAGES_PER_BLOCK

    def page_copy(seq_i, blk_i, slot, pg):
        page = pt_ref[seq_i * n_pages + blk_i * PAGES_PER_BLOCK + pg]
        return pltpu.make_async_copy(kc_ref.at[page], kbuf.at[slot, pg], sem.at[slot, pg])

    def start(seq_i, blk_i, slot):
        for pg in range(PAGES_PER_BLOCK):
            page_copy(seq_i, blk_i, slot, pg).start()

    @pl.when(b == 0)
    def _():
        for j in range(_GATE_BUFS):
            start(0, j, j)

    qb = jnp.broadcast_to(q_ref[...], (N_HEADS, HEAD_DIM, LANES))
    lane = lax.broadcasted_iota(jnp.int32, (1, LANES), 1)

    def body(j, c):
        slot = j % _GATE_BUFS
        for pg in range(PAGES_PER_BLOCK):
            page_copy(b, j, slot, pg).wait()
        s = kbuf[slot, 0] * qb
        for pg in range(1, PAGES_PER_BLOCK):
            s = s + kbuf[slot, pg] * qb
        s_scr[j] = jnp.sum(s, axis=1)
        nxt = j + _GATE_BUFS

        @pl.when(nxt < n_blocks)
        def _():
            start(b, nxt, slot)

        @pl.when((nxt >= n_blocks) & (b + 1 < n_b))
        def _():
            start(b + 1, nxt - n_blocks, slot)

        return c

    lax.fori_loop(0, n_blocks, body, 0)
    gate = jnp.zeros((N_HEADS, LANES), F32)
    for j in range(n_blocks):
        ssum = jnp.sum(s_scr[j], axis=1, keepdims=True) * (1.0 / MOBA_BLOCK)
        gate = jnp.where(lane == j, ssum, gate)
    g = jnp.where(lane < n_blocks, gate, LOWEST)
    sel = jnp.zeros((N_HEADS, LANES), jnp.int32)
    for r in range(MOBA_TOP_K):
        m = jnp.max(g, axis=1, keepdims=True)
        idx = jnp.min(jnp.where(g == m, lane, LANES), axis=1, keepdims=True)
        sel = jnp.where(lane == r, idx, sel)
        g = jnp.where(lane == idx, LOWEST, g)
    sel_ref[...] = sel


def _sample_gate(page_table_flat, q_col, cache_kT, n_seq, n_pages):
    grid_spec = pltpu.PrefetchScalarGridSpec(
        num_scalar_prefetch=1,
        grid=(n_seq,),
        in_specs=[pl.BlockSpec((None, N_HEADS, HEAD_DIM, 1), lambda b, pt: (b, 0, 0, 0)),
                  pl.BlockSpec(memory_space=pl.ANY)],
        out_specs=pl.BlockSpec((None, N_HEADS, LANES), lambda b, pt: (b, 0, 0)),
        scratch_shapes=[pltpu.VMEM((_GATE_BUFS, PAGES_PER_BLOCK, N_HEADS, HEAD_DIM, PAGE_SIZE), F32),
                        pltpu.SemaphoreType.DMA((_GATE_BUFS, PAGES_PER_BLOCK)),
                        pltpu.VMEM((n_pages // PAGES_PER_BLOCK, N_HEADS, LANES), F32)],
    )
    return pl.pallas_call(
        functools.partial(_sample_gate_kernel, n_pages=n_pages),
        grid_spec=grid_spec,
        out_shape=jax.ShapeDtypeStruct((n_seq, N_HEADS, LANES), jnp.int32),
        compiler_params=pltpu.CompilerParams(
            dimension_semantics=("arbitrary",), vmem_limit_bytes=VMEM_LIMIT),
        name="sample_gate",
    )(page_table_flat, q_col, cache_kT)


def _sample_attn_kernel(pt_ref, sel_ref, q_ref, kn_ref, vn_ref, kc_ref, vc_ref, o_ref,
                        kbuf, vbuf, sem, *, n_pages):
    b = pl.program_id(0)
    n_b = pl.num_programs(0)
    n_slots = MOBA_TOP_K * PAGES_PER_BLOCK

    def copies(seq_i, buf_i):
        out = []
        for h in range(N_HEADS):
            for r in range(MOBA_TOP_K):
                blk = sel_ref[(seq_i * N_HEADS + h) * MOBA_TOP_K + r]
                for pg in range(PAGES_PER_BLOCK):
                    page = pt_ref[seq_i * n_pages + blk * PAGES_PER_BLOCK + pg]
                    s_i = r * PAGES_PER_BLOCK + pg
                    out.append(pltpu.make_async_copy(kc_ref.at[page, h], kbuf.at[buf_i, h, s_i], sem.at[buf_i, 0]))
                    out.append(pltpu.make_async_copy(vc_ref.at[page, h], vbuf.at[buf_i, h, s_i], sem.at[buf_i, 1]))
        return out

    @pl.when(b == 0)
    def _():
        for c in copies(0, 0):
            c.start()

    cur = b % 2

    @pl.when(b + 1 < n_b)
    def _():
        for c in copies(b + 1, 1 - cur):
            c.start()

    for c in copies(b, cur):
        c.wait()

    for h in range(N_HEADS):
        qh = q_ref[h]
        kb = kbuf[cur, h]
        vb = vbuf[cur, h]
        s = jnp.sum(kb * qh[None], axis=1) * ATTN_SCALE
        s_new = jnp.sum(qh * kn_ref[h], axis=0, keepdims=True) * ATTN_SCALE
        m = jnp.maximum(jnp.max(jnp.max(s, axis=1, keepdims=True), axis=0, keepdims=True), s_new)
        p = jnp.exp(s - m)
        p_new = jnp.exp(s_new - m)
        l = jnp.sum(jnp.sum(p, axis=1, keepdims=True), axis=0, keepdims=True) + p_new
        pv = vb[0] * p[0:1]
        for s_i in range(1, n_slots):
            pv = pv + vb[s_i] * p[s_i:s_i + 1]
        o = jnp.sum(pv, axis=1, keepdims=True) + p_new * vn_ref[h]
        o_ref[h] = o / l


def _sample_attn(page_table_flat, sel_flat, q_col, kn_col, vn_col, cache_kT, cache_vT, n_seq, n_pages):
    col = pl.BlockSpec((None, N_HEADS, HEAD_DIM, 1), lambda b, pt, sel: (b, 0, 0, 0))
    n_slots = MOBA_TOP_K * PAGES_PER_BLOCK
    buf = pltpu.VMEM((2, N_HEADS, n_slots, HEAD_DIM, PAGE_SIZE), F32)
    grid_spec = pltpu.PrefetchScalarGridSpec(
        num_scalar_prefetch=2,
        grid=(n_seq,),
        in_specs=[col, col, col, pl.BlockSpec(memory_space=pl.ANY), pl.BlockSpec(memory_space=pl.ANY)],
        out_specs=col,
        scratch_shapes=[buf, buf, pltpu.SemaphoreType.DMA((2, 2))],
    )
    return pl.pallas_call(
        functools.partial(_sample_attn_kernel, n_pages=n_pages),
        grid_spec=grid_spec,
        out_shape=jax.ShapeDtypeStruct((n_seq, N_HEADS, HEAD_DIM, 1), F32),
        compiler_params=pltpu.CompilerParams(
            dimension_semantics=("arbitrary",), vmem_limit_bytes=VMEM_LIMIT),
        name="sample_attn",
    )(page_table_flat, sel_flat, q_col, kn_col, vn_col, cache_kT, cache_vT)


def kernel(x_prompt, x_sample, cache_k, cache_v, state_pool, page_table, p_prompt, p_sample, ln_mix, w_in,
           q_norm, k_norm, pool_w, pool_scale, w_attn_out, w_pool_out, w_out, ln_mlp, w_up, w_down, ln_ple,
           w_ple_gate, w_ple_proj):
    depth = w_in.shape[0]
    assert depth == 1
    n_seq, n_pages = page_table.shape
    past_len = n_pages * PAGE_SIZE
    assert x_sample.shape[1] == 1 and past_len % MOBA_BLOCK == 0 and past_len // MOBA_BLOCK >= MOBA_TOP_K
    assert (past_len // MOBA_BLOCK) % _GATE_BUFS == 0 and (x_prompt.shape[1] // MOBA_BLOCK) % 16 == 0
    bsz, seq, _ = x_prompt.shape
    l = 0

    w_in_b = w_in[l].astype(BF16)
    wqkvT = w_in_b[:, :3 * ATTN_WIDTH].T
    wrest = w_in_b[:, 3 * ATTN_WIDTH:]
    poolw = pool_w[l].astype(BF16)
    tail_w = (w_attn_out[l].astype(BF16), w_pool_out[l].astype(BF16), w_out[l].astype(BF16), ln_mlp,
              w_up[l].astype(BF16), w_down[l].astype(BF16), ln_ple, w_ple_gate[l].astype(BF16),
              w_ple_proj[l].astype(BF16))

    (kT, vT, qa, ka, vTb, b_p, sga, sgb, hist_p) = _prompt_in(
        x_prompt, ln_mix, wqkvT, wrest, q_norm.reshape(HEAD_DIM, 1), k_norm.reshape(HEAD_DIM, 1),
        poolw, pool_scale)
    aT = _prompt_attn(qa, ka, vTb).reshape(bsz, ATTN_WIDTH, seq)
    y_prompt = _tail(x_prompt, aT, b_p, sga, sgb, p_prompt[l], *tail_w, tm=512)
    k_prompt = kT.transpose(0, 3, 1, 2)[None]
    v_prompt = vT.transpose(0, 3, 1, 2)[None]
    pool_prompt = hist_p[:, HIST_ROWS - POOL_HIST:][None]

    xs = x_sample.reshape(n_seq, D_MODEL)
    hist_t = state_pool[l].transpose(1, 0, 2)
    q_s, k_s, v_s, b_s, sga_s, sgb_s, hist_s = _sample_in(
        xs, ln_mix, w_in_b, q_norm, k_norm, hist_t, poolw, pool_scale, past_len)
    cache_kT = cache_k[l].transpose(0, 2, 3, 1)
    cache_vT = cache_v[l].transpose(0, 2, 3, 1)
    pt_flat = page_table.reshape(-1)
    q_col = q_s.reshape(n_seq, N_HEADS, HEAD_DIM, 1)
    sel = _sample_gate(pt_flat, q_col, cache_kT, n_seq, n_pages)
    sel_flat = sel[:, :, :MOBA_TOP_K].reshape(-1)
    a_s = _sample_attn(pt_flat, sel_flat, q_col, k_s.reshape(n_seq, N_HEADS, HEAD_DIM, 1),
                       v_s.reshape(n_seq, N_HEADS, HEAD_DIM, 1), cache_kT, cache_vT, n_seq, n_pages)
    aT_s = a_s.reshape(n_seq, ATTN_WIDTH).T.astype(BF16)[None]
    y_sample = _tail(xs[None], aT_s, b_s[None], sga_s[None], sgb_s[None], p_sample[l].reshape(1, n_seq, PLE_DIM),
                     *tail_w, tm=n_seq)
    y_sample = y_sample.reshape(n_seq, 1, D_MODEL)
    k_sample = k_s.reshape(1, n_seq, 1, N_HEADS, HEAD_DIM)
    v_sample = v_s.reshape(1, n_seq, 1, N_HEADS, HEAD_DIM)
    pool_sample = hist_s.transpose(1, 0, 2)[None]
    return (y_prompt, y_sample, k_prompt, v_prompt, pool_prompt, k_sample, v_sample, pool_sample)
```

```python
import functools

import jax
import jax.numpy as jnp
from jax import lax
from jax.experimental import pallas as pl
from jax.experimental.pallas import tpu as pltpu

D_MODEL = 1024
N_HEADS = 8
HEAD_DIM = 64
ATTN_WIDTH = N_HEADS * HEAD_DIM
MOBA_BLOCK = 256
MOBA_TOP_K = 3
PAGE_SIZE = 128
PAGES_PER_BLOCK = MOBA_BLOCK // PAGE_SIZE
ATTN_SCALE = HEAD_DIM ** -0.5
POOL_WINDOWS = (2, 4, 8, 16)
POOL_GROUP = 128
POOL_WIDTH = 512
POOL_HIST = 15
HIST_ROWS = 16
V_AUG = HEAD_DIM + 16
D_FF = 4 * D_MODEL
PLE_DIM = 256
EPS = 1e-6
NEG_INF = -1e30
LOWEST = -3e38
LOG2E = 1.4426950408889634

LANES = 128
VMEM_LIMIT = 56 * 1024 * 1024

F32 = jnp.float32
BF16 = jnp.bfloat16

_NT = (((1,), (1,)), ((), ()))
_TN = (((0,), (0,)), ((), ()))


def _rms_rows(x, g):
    ms = jnp.mean(x * x, axis=-1, keepdims=True)
    return x * lax.rsqrt(ms + EPS) * g


def _split_bf16(a):
    hi = a.astype(BF16)
    lo = (a - hi.astype(F32)).astype(BF16)
    return hi, lo


def _const_spec(shape):
    zeros = (0,) * len(shape)
    return pl.BlockSpec(shape, lambda *_: zeros, pipeline_mode=pl.Buffered(1))


def _pool_means(e_ref, tm, pos0):
    pos = pos0 + lax.broadcasted_iota(jnp.int32, (tm, 1), 0)
    out = []
    for g, w in enumerate(POOL_WINDOWS):
        c0 = g * POOL_GROUP
        cur = e_ref[HIST_ROWS:HIST_ROWS + tm, c0:c0 + POOL_GROUP]
        acc = cur
        for k in range(1, w):
            acc = acc + e_ref[HIST_ROWS - k:HIST_ROWS - k + tm, c0:c0 + POOL_GROUP]
        cnt = jnp.minimum(w, pos + 1).astype(F32)
        out.append(acc / cnt - cur)
    return out


def _prompt_in_kernel(x_ref, ln_ref, wqkvT_ref, wrest_ref, qn_ref, kn_ref, poolw_ref, pscale_ref,
                      kT_ref, vT_ref, qa_ref, ka_ref, vTb_ref, b_ref, sga_ref, sgb_ref,
                      hist_ref, e_scr, km_scr, *, tm, n_blocks):
    i = pl.program_id(1)
    n_i = pl.num_programs(1)
    blocks_per_tile = tm // MOBA_BLOCK

    @pl.when(i == 0)
    def _():
        e_scr[0:HIST_ROWS, :] = jnp.zeros((HIST_ROWS, POOL_WIDTH), F32)
        km_scr[...] = jnp.zeros(km_scr.shape, F32)

    xn = _rms_rows(x_ref[...], ln_ref[...]).astype(BF16)

    def proj_t(r0):
        t = lax.dot_general(wqkvT_ref[r0:r0 + ATTN_WIDTH, :], xn, _NT, preferred_element_type=F32)
        return t.reshape(N_HEADS, HEAD_DIM, tm)

    def head_norm(t, g_col):
        ms = jnp.mean(t * t, axis=1, keepdims=True)
        return t * lax.rsqrt(ms + EPS) * g_col[None]

    q = head_norm(proj_t(0), qn_ref[...])
    k = head_norm(proj_t(ATTN_WIDTH), kn_ref[...])
    v = proj_t(2 * ATTN_WIDTH)
    kT_ref[...] = k
    vT_ref[...] = v
    vTb_ref[:, 0:HEAD_DIM, :] = v.astype(BF16)
    ones_row = lax.broadcasted_iota(jnp.int32, (N_HEADS, V_AUG - HEAD_DIM, tm), 1) == 0
    vTb_ref[:, HEAD_DIM:V_AUG, :] = jnp.where(ones_row, 1.0, 0.0).astype(BF16)
    qa_ref[:, 0:HEAD_DIM, :] = (q * (ATTN_SCALE * LOG2E)).astype(BF16)
    key_blk = (i * tm + lax.broadcasted_iota(jnp.int32, (tm, 1), 0)) // MOBA_BLOCK
    onehot = jnp.where(key_blk == lax.broadcasted_iota(jnp.int32, (1, n_blocks), 1), 1.0, 0.0).astype(BF16)
    for h in range(N_HEADS):
        ka_ref[h, :, 0:HEAD_DIM] = k[h].T.astype(BF16)
        ka_ref[h, :, HEAD_DIM:HEAD_DIM + n_blocks] = onehot

    lane = lax.broadcasted_iota(jnp.int32, (1, 1, LANES), 2)
    km = km_scr[...]
    for h in range(blocks_per_tile):
        ks = jnp.sum(k[:, :, h * MOBA_BLOCK:(h + 1) * MOBA_BLOCK], axis=2, keepdims=True) * (1.0 / MOBA_BLOCK)
        km = jnp.where(lane == i * blocks_per_tile + h, ks, km)
    km_scr[...] = km

    pos = i * tm + lax.broadcasted_iota(jnp.int32, (1, tm), 1)
    own = pos // MOBA_BLOCK
    blk = lax.broadcasted_iota(jnp.int32, (n_blocks, 1), 0)
    for h in range(N_HEADS):
        km_hi, km_lo = _split_bf16(km[h][:, 0:n_blocks])
        q_hi, q_lo = _split_bf16(q[h])
        gate = (lax.dot_general(km_hi, q_hi, _TN, preferred_element_type=F32)
                + lax.dot_general(km_hi, q_lo, _TN, preferred_element_type=F32)
                + lax.dot_general(km_lo, q_hi, _TN, preferred_element_type=F32))
        g = jnp.where(blk < own, gate, LOWEST)
        sel = blk == own
        for _ in range(MOBA_TOP_K):
            m = jnp.max(g, axis=0, keepdims=True)
            idx = jnp.min(jnp.where(g == m, blk, n_blocks), axis=0, keepdims=True)
            pick = (blk == idx) & (m > LOWEST)
            sel = sel | pick
            g = jnp.where(pick, LOWEST, g)
        qa_ref[h, HEAD_DIM:HEAD_DIM + n_blocks, :] = jnp.where(sel, 0.0, NEG_INF).astype(BF16)

    u = jnp.dot(xn, wrest_ref[:, 0:POOL_WIDTH], preferred_element_type=F32)
    e_scr[HIST_ROWS:HIST_ROWS + tm, :] = u
    d = _pool_means(e_scr, tm, i * tm)
    for g_i in range(len(POOL_WINDOWS)):
        c0 = g_i * POOL_GROUP
        y = jnp.dot(d[g_i].astype(BF16), poolw_ref[g_i], preferred_element_type=F32)
        b_ref[:, c0:c0 + POOL_GROUP] = (y * pscale_ref[:, c0:c0 + POOL_GROUP]).astype(BF16)
    tail = e_scr[tm:tm + HIST_ROWS, :]
    e_scr[0:HIST_ROWS, :] = tail

    @pl.when(i == n_i - 1)
    def _():
        hist_ref[...] = tail

    ga = jnp.dot(xn, wrest_ref[:, POOL_WIDTH:POOL_WIDTH + D_MODEL], preferred_element_type=F32)
    sga_ref[...] = jax.nn.sigmoid(ga).astype(BF16)
    gb = jnp.dot(xn, wrest_ref[:, POOL_WIDTH + D_MODEL:POOL_WIDTH + 2 * D_MODEL], preferred_element_type=F32)
    sgb_ref[...] = jax.nn.sigmoid(gb).astype(BF16)


def _prompt_in(x, ln, wqkvT, wrest, qn_col, kn_col, poolw, pscale, tm=512):
    bsz, seq, _ = x.shape
    n_blocks = seq // MOBA_BLOCK
    grid = (bsz, seq // tm)
    hT = lambda dt: jax.ShapeDtypeStruct((bsz, N_HEADS, HEAD_DIM, seq), dt)
    hT_spec = pl.BlockSpec((None, N_HEADS, HEAD_DIM, tm), lambda b, i: (b, 0, 0, i))
    row_spec = lambda w: pl.BlockSpec((None, tm, w), lambda b, i: (b, i, 0))
    aug = HEAD_DIM + n_blocks
    out_shape = (
        hT(F32), hT(F32),
        jax.ShapeDtypeStruct((bsz, N_HEADS, aug, seq), BF16),
        jax.ShapeDtypeStruct((bsz, N_HEADS, seq, aug), BF16),
        jax.ShapeDtypeStruct((bsz, N_HEADS, V_AUG, seq), BF16),
        jax.ShapeDtypeStruct((bsz, seq, POOL_WIDTH), BF16),
        jax.ShapeDtypeStruct((bsz, seq, D_MODEL), BF16),
        jax.ShapeDtypeStruct((bsz, seq, D_MODEL), BF16),
        jax.ShapeDtypeStruct((bsz, HIST_ROWS, POOL_WIDTH), F32),
    )
    out_specs = (
        hT_spec, hT_spec,
        pl.BlockSpec((None, N_HEADS, aug, tm), lambda b, i: (b, 0, 0, i)),
        pl.BlockSpec((None, N_HEADS, tm, aug), lambda b, i: (b, 0, i, 0)),
        pl.BlockSpec((None, N_HEADS, V_AUG, tm), lambda b, i: (b, 0, 0, i)),
        row_spec(POOL_WIDTH), row_spec(D_MODEL), row_spec(D_MODEL),
        pl.BlockSpec((None, HIST_ROWS, POOL_WIDTH), lambda b, i: (b, 0, 0)),
    )
    in_specs = [
        row_spec(D_MODEL),
        _const_spec(ln.shape), _const_spec(wqkvT.shape), _const_spec(wrest.shape),
        _const_spec(qn_col.shape), _const_spec(kn_col.shape),
        _const_spec(poolw.shape), _const_spec(pscale.shape),
    ]
    return pl.pallas_call(
        functools.partial(_prompt_in_kernel, tm=tm, n_blocks=n_blocks),
        grid=grid, in_specs=in_specs, out_specs=out_specs, out_shape=out_shape,
        scratch_shapes=[pltpu.VMEM((HIST_ROWS + tm, POOL_WIDTH), F32),
                        pltpu.VMEM((N_HEADS, HEAD_DIM, LANES), F32)],
        compiler_params=pltpu.CompilerParams(
            dimension_semantics=("arbitrary", "arbitrary"), vmem_limit_bytes=VMEM_LIMIT),
        name="prompt_in",
    )(x, ln, wqkvT, wrest, qn_col, kn_col, poolw, pscale)


def _prompt_attn_kernel(qa_ref, ka_ref, vT_ref, o_ref, m_scr, acc_scr, s_scr, cm_scr, *, heads):
    t = pl.program_id(2)
    blk = MOBA_BLOCK
    tri = (lax.broadcasted_iota(jnp.int32, (blk, 1), 0) <= lax.broadcasted_iota(jnp.int32, (1, blk), 1))
    m_scr[...] = jnp.full(m_scr.shape, NEG_INF, F32)
    acc_scr[...] = jnp.zeros(acc_scr.shape, F32)

    def scores(j, slot, mask, gs=tuple(range(heads))):
        r0 = j * blk if isinstance(j, int) else pl.multiple_of(j * blk, blk)
        for g in gs:
            s = jnp.dot(ka_ref[g, pl.ds(r0, blk), :], qa_ref[g], preferred_element_type=F32)
            if mask is not None:
                s = jnp.where(mask, s, NEG_INF)
            s_scr[slot, g] = s
            cm_scr[slot, g] = jnp.max(s, axis=0, keepdims=True)

    def softmax_pv(j, slot, gs=tuple(range(heads))):
        r0 = j * blk if isinstance(j, int) else pl.multiple_of(j * blk, blk)
        for g in gs:
            m = m_scr[g]
            m_new = jnp.maximum(m, cm_scr[slot, g])
            alpha = jnp.exp2(m - m_new)
            p = jnp.exp2((s_scr[slot, g] - m_new).astype(BF16))
            m_scr[g] = m_new
            pv = jnp.dot(vT_ref[g, :, pl.ds(r0, blk)], p, preferred_element_type=F32)
            acc_scr[g] = alpha * acc_scr[g] + pv

    def step(j, cur, diag_next):
        for g in range(heads):
            scores(j + 1, 1 - cur, tri if diag_next else None, (g,))
            softmax_pv(j, cur, (g,))

    odd_lead = (t + 1) % 2
    scores(0, odd_lead, tri | (t > 0))

    @pl.when((t >= 1) & (odd_lead == 1))
    def _():
        step(0, 1, False)

    def pair(i, c):
        j = odd_lead + 2 * i
        step(j, 0, False)
        step(j + 1, 1, False)
        return c

    lax.fori_loop(0, (t - 1) // 2, pair, 0)

    @pl.when(t >= 1)
    def _():
        step(t - 1, 0, True)

    softmax_pv(t, 1)
    for g in range(heads):
        o_ref[g] = (acc_scr[g, 0:HEAD_DIM] / acc_scr[g, HEAD_DIM:HEAD_DIM + 1]).astype(o_ref.dtype)


def _prompt_attn(qa, ka, vT, heads=N_HEADS):
    bsz, _, aug, seq = qa.shape
    blk = MOBA_BLOCK
    grid = (bsz, N_HEADS // heads, seq // blk)
    resident = lambda r, c: pl.BlockSpec((None, heads, r, c), lambda b, h, t: (b, h, 0, 0),
                                         pipeline_mode=pl.Buffered(1))
    return pl.pallas_call(
        functools.partial(_prompt_attn_kernel, heads=heads),
        grid=grid,
        in_specs=[pl.BlockSpec((None, heads, aug, blk), lambda b, h, t: (b, h, 0, t)),
                  resident(seq, aug), resident(V_AUG, seq)],
        out_specs=pl.BlockSpec((None, heads, HEAD_DIM, blk), lambda b, h, t: (b, h, 0, t)),
        out_shape=jax.ShapeDtypeStruct((bsz, N_HEADS, HEAD_DIM, seq), BF16),
        scratch_shapes=[pltpu.VMEM((heads, 1, blk), F32),
                        pltpu.VMEM((heads, V_AUG, blk), F32), pltpu.VMEM((2, heads, blk, blk), F32),
                        pltpu.VMEM((2, heads, 1, blk), F32)],
        compiler_params=pltpu.CompilerParams(
            dimension_semantics=("arbitrary", "arbitrary", "arbitrary"), vmem_limit_bytes=VMEM_LIMIT),
        name="prompt_attn",
    )(qa, ka, vT)


def _tail_kernel(x_ref, aT_ref, b_ref, sga_ref, sgb_ref, p_ref, wao_ref, wpo_ref, wo_ref, lnm_ref,
                 wup_ref, wdn_ref, lnp_ref, wpg_ref, wpp_ref, y_ref, *, ff_chunk):
    a_proj = lax.dot_general(aT_ref[...], wao_ref[...], _TN, preferred_element_type=F32)
    b_proj = jnp.dot(b_ref[...], wpo_ref[...], preferred_element_type=F32)
    m = sga_ref[...].astype(F32) * a_proj + sgb_ref[...].astype(F32) * b_proj
    x1 = x_ref[...] + jnp.dot(m.astype(BF16), wo_ref[...], preferred_element_type=F32)
    xn = _rms_rows(x1, lnm_ref[...]).astype(BF16)
    acc = x1
    for c in range(D_FF // ff_chunk):
        hid = jnp.maximum(jnp.dot(xn, wup_ref[:, c * ff_chunk:(c + 1) * ff_chunk],
                                  preferred_element_type=F32), 0.0)
        acc = acc + jnp.dot((hid * hid).astype(BF16), wdn_ref[c * ff_chunk:(c + 1) * ff_chunk, :],
                            preferred_element_type=F32)
    x2 = acc
    xg = _rms_rows(x2, lnp_ref[...]).astype(BF16)
    gate = jax.nn.sigmoid(jnp.dot(xg, wpg_ref[...], preferred_element_type=F32))
    emb = jnp.dot(p_ref[...].astype(BF16), wpp_ref[...], preferred_element_type=F32)
    y_ref[...] = x2 + gate * emb


def _tail(x, aT, b, sga, sgb, p, wao, wpo, wo, lnm, wup, wdn, lnp, wpg, wpp, tm, ff_chunk=1024):
    bsz, seq, _ = x.shape
    grid = (bsz, seq // tm)
    row_spec = lambda w: pl.BlockSpec((None, tm, w), lambda b_, i: (b_, i, 0))
    in_specs = [
        row_spec(D_MODEL),
        pl.BlockSpec((None, ATTN_WIDTH, tm), lambda b_, i: (b_, 0, i)),
        row_spec(POOL_WIDTH), row_spec(D_MODEL), row_spec(D_MODEL), row_spec(PLE_DIM),
    ] + [_const_spec(w.shape) for w in (wao, wpo, wo, lnm, wup, wdn, lnp, wpg, wpp)]
    return pl.pallas_call(
        functools.partial(_tail_kernel, ff_chunk=ff_chunk),
        grid=grid, in_specs=in_specs, out_specs=row_spec(D_MODEL),
        out_shape=jax.ShapeDtypeStruct((bsz, seq, D_MODEL), F32),
        compiler_params=pltpu.CompilerParams(
            dimension_semantics=("arbitrary", "arbitrary"), vmem_limit_bytes=VMEM_LIMIT),
        name="tail",
    )(x, aT, b, sga, sgb, p, wao, wpo, wo, lnm, wup, wdn, lnp, wpg, wpp)


def _sample_in_kernel(x_ref, ln_ref, win_ref, qn_ref, kn_ref, hist_ref, poolw_ref, pscale_ref,
                      q_ref, k_ref, v_ref, b_ref, sga_ref, sgb_ref, hist_out_ref, *, past_len):
    xn = _rms_rows(x_ref[...], ln_ref[...]).astype(BF16)
    z = jnp.dot(xn, win_ref[...], preferred_element_type=F32)
    for h in range(N_HEADS):
        c0 = h * HEAD_DIM
        q_ref[:, c0:c0 + HEAD_DIM] = _rms_rows(z[:, c0:c0 + HEAD_DIM], qn_ref[...])
        k_ref[:, c0:c0 + HEAD_DIM] = _rms_rows(
            z[:, ATTN_WIDTH + c0:ATTN_WIDTH + c0 + HEAD_DIM], kn_ref[...])
    v_ref[...] = z[:, 2 * ATTN_WIDTH:3 * ATTN_WIDTH]
    u0 = 3 * ATTN_WIDTH
    u = z[:, u0:u0 + POOL_WIDTH]
    for g_i, w in enumerate(POOL_WINDOWS):
        c0 = g_i * POOL_GROUP
        cur = u[:, c0:c0 + POOL_GROUP]
        acc = cur
        for k in range(1, w):
            acc = acc + hist_ref[POOL_HIST - k, :, c0:c0 + POOL_GROUP]
        cnt = float(min(w, past_len + 1))
        d = acc / cnt - cur
        y = jnp.dot(d.astype(BF16), poolw_ref[g_i], preferred_element_type=F32)
        b_ref[:, c0:c0 + POOL_GROUP] = (y * pscale_ref[:, c0:c0 + POOL_GROUP]).astype(BF16)
    for t in range(POOL_HIST - 1):
        hist_out_ref[t] = hist_ref[t + 1]
    hist_out_ref[POOL_HIST - 1] = u
    g0 = u0 + POOL_WIDTH
    sga_ref[...] = jax.nn.sigmoid(z[:, g0:g0 + D_MODEL]).astype(BF16)
    sgb_ref[...] = jax.nn.sigmoid(z[:, g0 + D_MODEL:g0 + 2 * D_MODEL]).astype(BF16)


def _sample_in(x, ln, win, qn_row, kn_row, hist_t, poolw, pscale, past_len):
    n = x.shape[0]
    sds = jax.ShapeDtypeStruct
    out_shape = (sds((n, ATTN_WIDTH), F32), sds((n, ATTN_WIDTH), F32), sds((n, ATTN_WIDTH), F32),
                 sds((n, POOL_WIDTH), BF16), sds((n, D_MODEL), BF16), sds((n, D_MODEL), BF16),
                 sds(hist_t.shape, F32))
    return pl.pallas_call(
        functools.partial(_sample_in_kernel, past_len=past_len),
        out_shape=out_shape,
        compiler_params=pltpu.CompilerParams(vmem_limit_bytes=VMEM_LIMIT),
        name="sample_in",
    )(x, ln, win, qn_row, kn_row, hist_t, poolw, pscale)


_GATE_BUFS = 16


def _sample_gate_kernel(pt_ref, q_ref, kc_ref, sel_ref, kbuf, sem, s_scr, *, n_pages):
    b = pl.program_id(0)
    n_b = pl.num_programs(0)
    n_blocks = n_pages // PAGES_PER_BLOCK

    def page_copy(seq_i, blk_i, slot, pg):
        page = pt_ref[seq_i * n_pages + blk_i * PAGES_PER_BLOCK + pg]
        return pltpu.make_async_copy(kc_ref.at[page], kbuf.at[slot, pg], sem.at[slot, pg])

    def start(seq_i, blk_i, slot):
        for pg in range(PAGES_PER_BLOCK):
            page_copy(seq_i, blk_i, slot, pg).start()

    @pl.when(b == 0)
    def _():
        for j in range(_GATE_BUFS):
            start(0, j, j)

    qb = jnp.broadcast_to(q_ref[...], (N_HEADS, HEAD_DIM, LANES))
    lane = lax.broadcasted_iota(jnp.int32, (1, LANES), 1)

    def body(j, c):
        slot = j % _GATE_BUFS
        for pg in range(PAGES_PER_BLOCK):
            page_copy(b, j, slot, pg).wait()
        s = kbuf[slot, 0] * qb
        for pg in range(1, PAGES_PER_BLOCK):
            s = s + kbuf[slot, pg] * qb
        s_scr[j] = jnp.sum(s, axis=1)
        nxt = j + _GATE_BUFS

        @pl.when(nxt < n_blocks)
        def _():
            start(b, nxt, slot)

        @pl.when((nxt >= n_blocks) & (b + 1 < n_b))
        def _():
            start(b + 1, nxt - n_blocks, slot)

        return c

    lax.fori_loop(0, n_blocks, body, 0)
    gate = jnp.zeros((N_HEADS, LANES), F32)
    for j in range(n_blocks):
        ssum = jnp.sum(s_scr[j], axis=1, keepdims=True) * (1.0 / MOBA_BLOCK)
        gate = jnp.where(lane == j, ssum, gate)
    g = jnp.where(lane < n_blocks, gate, LOWEST)
    sel = jnp.zeros((N_HEADS, LANES), jnp.int32)
    for r in range(MOBA_TOP_K):
        m = jnp.max(g, axis=1, keepdims=True)
        idx = jnp.min(jnp.where(g == m, lane, LANES), axis=1, keepdims=True)
        sel = jnp.where(lane == r, idx, sel)
        g = jnp.where(lane == idx, LOWEST, g)
    sel_ref[...] = sel


def _sample_gate(page_table_flat, q_col, cache_kT, n_seq, n_pages):
    grid_spec = pltpu.PrefetchScalarGridSpec(
        num_scalar_prefetch=1,
        grid=(n_seq,),
        in_specs=[pl.BlockSpec((None, N_HEADS, HEAD_DIM, 1), lambda b, pt: (b, 0, 0, 0)),
                  pl.BlockSpec(memory_space=pl.ANY)],
        out_specs=pl.BlockSpec((None, N_HEADS, LANES), lambda b, pt: (b, 0, 0)),
        scratch_shapes=[pltpu.VMEM((_GATE_BUFS, PAGES_PER_BLOCK, N_HEADS, HEAD_DIM, PAGE_SIZE), F32),
                        pltpu.SemaphoreType.DMA((_GATE_BUFS, PAGES_PER_BLOCK)),
                        pltpu.VMEM((n_pages // PAGES_PER_BLOCK, N_HEADS, LANES), F32)],
    )
    return pl.pallas_call(
        functools.partial(_sample_gate_kernel, n_pages=n_pages),
        grid_spec=grid_spec,
        out_shape=jax.ShapeDtypeStruct((n_seq, N_HEADS, LANES), jnp.int32),
        compiler_params=pltpu.CompilerParams(
            dimension_semantics=("arbitrary",), vmem_limit_bytes=VMEM_LIMIT),
        name="sample_gate",
    )(page_table_flat, q_col, cache_kT)


def _sample_attn_kernel(pt_ref, sel_ref, q_ref, kn_ref, vn_ref, kc_ref, vc_ref, o_ref,
                        kbuf, vbuf, sem, *, n_pages):
    b = pl.program_id(0)
    n_b = pl.num_programs(0)
    n_slots = MOBA_TOP_K * PAGES_PER_BLOCK

    def copies(seq_i, buf_i):
        out = []
        for h in range(N_HEADS):
            for r in range(MOBA_TOP_K):
                blk = sel_ref[(seq_i * N_HEADS + h) * MOBA_TOP_K + r]
                for pg in range(PAGES_PER_BLOCK):
                    page = pt_ref[seq_i * n_pages + blk * PAGES_PER_BLOCK + pg]
                    s_i = r * PAGES_PER_BLOCK + pg
                    out.append(pltpu.make_async_copy(kc_ref.at[page, h], kbuf.at[buf_i, h, s_i], sem.at[buf_i, 0]))
                    out.append(pltpu.make_async_copy(vc_ref.at[page, h], vbuf.at[buf_i, h, s_i], sem.at[buf_i, 1]))
        return out

    @pl.when(b == 0)
    def _():
        for c in copies(0, 0):
            c.start()

    cur = b % 2

    @pl.when(b + 1 < n_b)
    def _():
        for c in copies(b + 1, 1 - cur):
            c.start()

    for c in copies(b, cur):
        c.wait()

    for h in range(N_HEADS):
        qh = q_ref[h]
        kb = kbuf[cur, h]
        vb = vbuf[cur, h]
        s = jnp.sum(kb * qh[None], axis=1) * ATTN_SCALE
        s_new = jnp.sum(qh * kn_ref[h], axis=0, keepdims=True) * ATTN_SCALE
        m = jnp.maximum(jnp.max(jnp.max(s, axis=1, keepdims=True), axis=0, keepdims=True), s_new)
        p = jnp.exp(s - m)
        p_new = jnp.exp(s_new - m)
        l = jnp.sum(jnp.sum(p, axis=1, keepdims=True), axis=0, keepdims=True) + p_new
        pv = vb[0] * p[0:1]
        for s_i in range(1, n_slots):
            pv = pv + vb[s_i] * p[s_i:s_i + 1]
        o = jnp.sum(pv, axis=1, keepdims=True) + p_new * vn_ref[h]
        o_ref[h] = o / l


def _sample_attn(page_table_flat, sel_flat, q_col, kn_col, vn_col, cache_kT, cache_vT, n_seq, n_pages):
    col = pl.BlockSpec((None, N_HEADS, HEAD_DIM, 1), lambda b, pt, sel: (b, 0, 0, 0))
    n_slots = MOBA_TOP_K * PAGES_PER_BLOCK
    buf = pltpu.VMEM((2, N_HEADS, n_slots, HEAD_DIM, PAGE_SIZE), F32)
    grid_spec = pltpu.PrefetchScalarGridSpec(
        num_scalar_prefetch=2,
        grid=(n_seq,),
        in_specs=[col, col, col, pl.BlockSpec(memory_space=pl.ANY), pl.BlockSpec(memory_space=pl.ANY)],
        out_specs=col,
        scratch_shapes=[buf, buf, pltpu.SemaphoreType.DMA((2, 2))],
    )
    return pl.pallas_call(
        functools.partial(_sample_attn_kernel, n_pages=n_pages),
        grid_spec=grid_spec,
        out_shape=jax.ShapeDtypeStruct((n_seq, N_HEADS, HEAD_DIM, 1), F32),
        compiler_params=pltpu.CompilerParams(
            dimension_semantics=("arbitrary",), vmem_limit_bytes=VMEM_LIMIT),
        name="sample_attn",
    )(page_table_flat, sel_flat, q_col, kn_col, vn_col, cache_kT, cache_vT)


def kernel(x_prompt, x_sample, cache_k, cache_v, state_pool, page_table, p_prompt, p_sample, ln_mix, w_in,
           q_norm, k_norm, pool_w, pool_scale, w_attn_out, w_pool_out, w_out, ln_mlp, w_up, w_down, ln_ple,
           w_ple_gate, w_ple_proj):
    depth = w_in.shape[0]
    assert depth == 1
    n_seq, n_pages = page_table.shape
    past_len = n_pages * PAGE_SIZE
    assert x_sample.shape[1] == 1 and past_len % MOBA_BLOCK == 0 and past_len // MOBA_BLOCK >= MOBA_TOP_K
    assert (past_len // MOBA_BLOCK) % _GATE_BUFS == 0 and (x_prompt.shape[1] // MOBA_BLOCK) % 16 == 0
    bsz, seq, _ = x_prompt.shape
    l = 0

    w_in_b = w_in[l].astype(BF16)
    wqkvT = w_in_b[:, :3 * ATTN_WIDTH].T
    wrest = w_in_b[:, 3 * ATTN_WIDTH:]
    poolw = pool_w[l].astype(BF16)
    tail_w = (w_attn_out[l].astype(BF16), w_pool_out[l].astype(BF16), w_out[l].astype(BF16), ln_mlp,
              w_up[l].astype(BF16), w_down[l].astype(BF16), ln_ple, w_ple_gate[l].astype(BF16),
              w_ple_proj[l].astype(BF16))

    (kT, vT, qa, ka, vTb, b_p, sga, sgb, hist_p) = _prompt_in(
        x_prompt, ln_mix, wqkvT, wrest, q_norm.reshape(HEAD_DIM, 1), k_norm.reshape(HEAD_DIM, 1),
        poolw, pool_scale)
    aT = _prompt_attn(qa, ka, vTb).reshape(bsz, ATTN_WIDTH, seq)
    y_prompt = _tail(x_prompt, aT, b_p, sga, sgb, p_prompt[l], *tail_w, tm=512)
    k_prompt = kT.transpose(0, 3, 1, 2)[None]
    v_prompt = vT.transpose(0, 3, 1, 2)[None]
    pool_prompt = hist_p[:, HIST_ROWS - POOL_HIST:][None]

    xs = x_sample.reshape(n_seq, D_MODEL)
    hist_t = state_pool[l].transpose(1, 0, 2)
    q_s, k_s, v_s, b_s, sga_s, sgb_s, hist_s = _sample_in(
        xs, ln_mix, w_in_b, q_norm, k_norm, hist_t, poolw, pool_scale, past_len)
    cache_kT = cache_k[l].transpose(0, 2, 3, 1)
    cache_vT = cache_v[l].transpose(0, 2, 3, 1)
    pt_flat = page_table.reshape(-1)
    q_col = q_s.reshape(n_seq, N_HEADS, HEAD_DIM, 1)
    sel = _sample_gate(pt_flat, q_col, cache_kT, n_seq, n_pages)
    sel_flat = sel[:, :, :MOBA_TOP_K].reshape(-1)
    a_s = _sample_attn(pt_flat, sel_flat, q_col, k_s.reshape(n_seq, N_HEADS, HEAD_DIM, 1),
                       v_s.reshape(n_seq, N_HEADS, HEAD_DIM, 1), cache_kT, cache_vT, n_seq, n_pages)
    aT_s = a_s.reshape(n_seq, ATTN_WIDTH).T.astype(BF16)[None]
    y_sample = _tail(xs[None], aT_s, b_s[None], sga_s[None], sgb_s[None], p_sample[l].reshape(1, n_seq, PLE_DIM),
                     *tail_w, tm=n_seq)
    y_sample = y_sample.reshape(n_seq, 1, D_MODEL)
    k_sample = k_s.reshape(1, n_seq, 1, N_HEADS, HEAD_DIM)
    v_sample = v_s.reshape(1, n_seq, 1, N_HEADS, HEAD_DIM)
    pool_sample = hist_s.transpose(1, 0, 2)[None]
    return (y_prompt, y_sample, k_prompt, v_prompt, pool_prompt, k_sample, v_sample, pool_sample)
```

```python
import functools

import jax
import jax.numpy as jnp
from jax import lax
from jax.experimental import pallas as pl
from jax.experimental.pallas import tpu as pltpu

D_MODEL = 1024
N_HEADS = 8
HEAD_DIM = 64
ATTN_WIDTH = N_HEADS * HEAD_DIM
MOBA_BLOCK = 256
MOBA_TOP_K = 3
PAGE_SIZE = 128
PAGES_PER_BLOCK = MOBA_BLOCK // PAGE_SIZE
ATTN_SCALE = HEAD_DIM ** -0.5
POOL_WINDOWS = (2, 4, 8, 16)
POOL_GROUP = 128
POOL_WIDTH = 512
POOL_HIST = 15
HIST_ROWS = 16
V_AUG = HEAD_DIM + 16
D_FF = 4 * D_MODEL
PLE_DIM = 256
EPS = 1e-6
NEG_INF = -1e30
LOWEST = -3e38
LOG2E = 1.4426950408889634

LANES = 128
VMEM_LIMIT = 60 * 1024 * 1024

F32 = jnp.float32
BF16 = jnp.bfloat16

_NT = (((1,), (1,)), ((), ()))
_TN = (((0,), (0,)), ((), ()))


def _rms_rows(x, g):
    ms = jnp.mean(x * x, axis=-1, keepdims=True)
    return x * lax.rsqrt(ms + EPS) * g


def _split_bf16(a):
    hi = a.astype(BF16)
    lo = (a - hi.astype(F32)).astype(BF16)
    return hi, lo


def _const_spec(shape):
    zeros = (0,) * len(shape)
    return pl.BlockSpec(shape, lambda *_: zeros, pipeline_mode=pl.Buffered(1))


def _pool_means(e_ref, tm, pos0):
    pos = pos0 + lax.broadcasted_iota(jnp.int32, (tm, 1), 0)
    out = []
    for g, w in enumerate(POOL_WINDOWS):
        c0 = g * POOL_GROUP
        cur = e_ref[HIST_ROWS:HIST_ROWS + tm, c0:c0 + POOL_GROUP]
        acc = cur
        for k in range(1, w):
            acc = acc + e_ref[HIST_ROWS - k:HIST_ROWS - k + tm, c0:c0 + POOL_GROUP]
        cnt = jnp.minimum(w, pos + 1).astype(F32)
        out.append(acc / cnt - cur)
    return out


def _prompt_in_kernel(x_ref, ln_ref, wqkvT_ref, wrest_ref, qn_ref, kn_ref, poolw_ref, pscale_ref,
                      kT_ref, vT_ref, qa_ref, ka_ref, vTb_ref, b_ref, sga_ref, sgb_ref,
                      hist_ref, e_scr, km_scr, *, tm, n_blocks):
    i = pl.program_id(1)
    n_i = pl.num_programs(1)
    blocks_per_tile = tm // MOBA_BLOCK

    @pl.when(i == 0)
    def _():
        e_scr[0:HIST_ROWS, :] = jnp.zeros((HIST_ROWS, POOL_WIDTH), F32)
        km_scr[...] = jnp.zeros(km_scr.shape, F32)

    xn = _rms_rows(x_ref[...], ln_ref[...]).astype(BF16)

    def proj_t(r0):
        t = lax.dot_general(wqkvT_ref[r0:r0 + ATTN_WIDTH, :], xn, _NT, preferred_element_type=F32)
        return t.reshape(N_HEADS, HEAD_DIM, tm)

    def head_norm(t, g_col):
        ms = jnp.mean(t * t, axis=1, keepdims=True)
        return t * lax.rsqrt(ms + EPS) * g_col[None]

    q = head_norm(proj_t(0), qn_ref[...])
    k = head_norm(proj_t(ATTN_WIDTH), kn_ref[...])
    v = proj_t(2 * ATTN_WIDTH)
    kT_ref[...] = k
    vT_ref[...] = v
    vTb_ref[:, 0:HEAD_DIM, :] = v.astype(BF16)
    ones_row = lax.broadcasted_iota(jnp.int32, (N_HEADS, V_AUG - HEAD_DIM, tm), 1) == 0
    vTb_ref[:, HEAD_DIM:V_AUG, :] = jnp.where(ones_row, 1.0, 0.0).astype(BF16)
    qa_ref[:, 0:HEAD_DIM, :] = (q * (ATTN_SCALE * LOG2E)).astype(BF16)
    key_blk = (i * tm + lax.broadcasted_iota(jnp.int32, (tm, 1), 0)) // MOBA_BLOCK
    onehot = jnp.where(key_blk == lax.broadcasted_iota(jnp.int32, (1, n_blocks), 1), 1.0, 0.0).astype(BF16)
    for h in range(N_HEADS):
        ka_ref[h, :, 0:HEAD_DIM] = k[h].T.astype(BF16)
        ka_ref[h, :, HEAD_DIM:HEAD_DIM + n_blocks] = onehot

    lane = lax.broadcasted_iota(jnp.int32, (1, 1, LANES), 2)
    km = km_scr[...]
    for h in range(blocks_per_tile):
        ks = jnp.sum(k[:, :, h * MOBA_BLOCK:(h + 1) * MOBA_BLOCK], axis=2, keepdims=True) * (1.0 / MOBA_BLOCK)
        km = jnp.where(lane == i * blocks_per_tile + h, ks, km)
    km_scr[...] = km

    pos = i * tm + lax.broadcasted_iota(jnp.int32, (1, tm), 1)
    own = pos // MOBA_BLOCK
    blk = lax.broadcasted_iota(jnp.int32, (n_blocks, 1), 0)
    for h in range(N_HEADS):
        km_hi, km_lo = _split_bf16(km[h][:, 0:n_blocks])
        q_hi, q_lo = _split_bf16(q[h])
        gate = (lax.dot_general(km_hi, q_hi, _TN, preferred_element_type=F32)
                + lax.dot_general(km_hi, q_lo, _TN, preferred_element_type=F32)
                + lax.dot_general(km_lo, q_hi, _TN, preferred_element_type=F32))
        g = jnp.where(blk < own, gate, LOWEST)
        sel = blk == own
        for _ in range(MOBA_TOP_K):
            m = jnp.max(g, axis=0, keepdims=True)
            idx = jnp.min(jnp.where(g == m, blk, n_blocks), axis=0, keepdims=True)
            pick = (blk == idx) & (m > LOWEST)
            sel = sel | pick
            g = jnp.where(pick, LOWEST, g)
        qa_ref[h, HEAD_DIM:HEAD_DIM + n_blocks, :] = jnp.where(sel, 0.0, NEG_INF).astype(BF16)

    u = jnp.dot(xn, wrest_ref[:, 0:POOL_WIDTH], preferred_element_type=F32)
    e_scr[HIST_ROWS:HIST_ROWS + tm, :] = u
    d = _pool_means(e_scr, tm, i * tm)
    for g_i in range(len(POOL_WINDOWS)):
        c0 = g_i * POOL_GROUP
        y = jnp.dot(d[g_i].astype(BF16), poolw_ref[g_i], preferred_element_type=F32)
        b_ref[:, c0:c0 + POOL_GROUP] = (y * pscale_ref[:, c0:c0 + POOL_GROUP]).astype(BF16)
    tail = e_scr[tm:tm + HIST_ROWS, :]
    e_scr[0:HIST_ROWS, :] = tail

    @pl.when(i == n_i - 1)
    def _():
        hist_ref[...] = tail

    ga = jnp.dot(xn, wrest_ref[:, POOL_WIDTH:POOL_WIDTH + D_MODEL], preferred_element_type=F32)
    sga_ref[...] = jax.nn.sigmoid(ga).astype(BF16)
    gb = jnp.dot(xn, wrest_ref[:, POOL_WIDTH + D_MODEL:POOL_WIDTH + 2 * D_MODEL], preferred_element_type=F32)
    sgb_ref[...] = jax.nn.sigmoid(gb).astype(BF16)


def _prompt_in(x, ln, wqkvT, wrest, qn_col, kn_col, poolw, pscale, tm=512):
    bsz, seq, _ = x.shape
    n_blocks = seq // MOBA_BLOCK
    grid = (bsz, seq // tm)
    hT = lambda dt: jax.ShapeDtypeStruct((bsz, N_HEADS, HEAD_DIM, seq), dt)
    hT_spec = pl.BlockSpec((None, N_HEADS, HEAD_DIM, tm), lambda b, i: (b, 0, 0, i))
    row_spec = lambda w: pl.BlockSpec((None, tm, w), lambda b, i: (b, i, 0))
    aug = HEAD_DIM + n_blocks
    out_shape = (
        hT(F32), hT(F32),
        jax.ShapeDtypeStruct((bsz, N_HEADS, aug, seq), BF16),
        jax.ShapeDtypeStruct((bsz, N_HEADS, seq, aug), BF16),
        jax.ShapeDtypeStruct((bsz, N_HEADS, V_AUG, seq), BF16),
        jax.ShapeDtypeStruct((bsz, seq, POOL_WIDTH), BF16),
        jax.ShapeDtypeStruct((bsz, seq, D_MODEL), BF16),
        jax.ShapeDtypeStruct((bsz, seq, D_MODEL), BF16),
        jax.ShapeDtypeStruct((bsz, HIST_ROWS, POOL_WIDTH), F32),
    )
    out_specs = (
        hT_spec, hT_spec,
        pl.BlockSpec((None, N_HEADS, aug, tm), lambda b, i: (b, 0, 0, i)),
        pl.BlockSpec((None, N_HEADS, tm, aug), lambda b, i: (b, 0, i, 0)),
        pl.BlockSpec((None, N_HEADS, V_AUG, tm), lambda b, i: (b, 0, 0, i)),
        row_spec(POOL_WIDTH), row_spec(D_MODEL), row_spec(D_MODEL),
        pl.BlockSpec((None, HIST_ROWS, POOL_WIDTH), lambda b, i: (b, 0, 0)),
    )
    in_specs = [
        row_spec(D_MODEL),
        _const_spec(ln.shape), _const_spec(wqkvT.shape), _const_spec(wrest.shape),
        _const_spec(qn_col.shape), _const_spec(kn_col.shape),
        _const_spec(poolw.shape), _const_spec(pscale.shape),
    ]
    return pl.pallas_call(
        functools.partial(_prompt_in_kernel, tm=tm, n_blocks=n_blocks),
        grid=grid, in_specs=in_specs, out_specs=out_specs, out_shape=out_shape,
        scratch_shapes=[pltpu.VMEM((HIST_ROWS + tm, POOL_WIDTH), F32),
                        pltpu.VMEM((N_HEADS, HEAD_DIM, LANES), F32)],
        compiler_params=pltpu.CompilerParams(
            dimension_semantics=("arbitrary", "arbitrary"), vmem_limit_bytes=VMEM_LIMIT),
        name="prompt_in",
    )(x, ln, wqkvT, wrest, qn_col, kn_col, poolw, pscale)


def _prompt_attn_kernel(qa_ref, ka_ref, vT_ref, o_ref, m_scr, acc_scr, s_scr, cm_scr, *, heads):
    t = pl.program_id(2)
    blk = MOBA_BLOCK
    tri = (lax.broadcasted_iota(jnp.int32, (blk, 1), 0) <= lax.broadcasted_iota(jnp.int32, (1, blk), 1))
    m_scr[...] = jnp.full(m_scr.shape, NEG_INF, F32)
    acc_scr[...] = jnp.zeros(acc_scr.shape, F32)

    def scores(j, slot, mask, gs=tuple(range(heads))):
        r0 = j * blk if isinstance(j, int) else pl.multiple_of(j * blk, blk)
        for g in gs:
            s = jnp.dot(ka_ref[g, pl.ds(r0, blk), :], qa_ref[g], preferred_element_type=F32)
            if mask is not None:
                s = jnp.where(mask, s, NEG_INF)
            s_scr[slot, g] = s
            cm_scr[slot, g] = jnp.max(s, axis=0, keepdims=True)

    def softmax_pv(j, slot, gs=tuple(range(heads))):
        r0 = j * blk if isinstance(j, int) else pl.multiple_of(j * blk, blk)
        for g in gs:
            m = m_scr[g]
            m_new = jnp.maximum(m, cm_scr[slot, g])
            alpha = jnp.exp2(m - m_new)
            p = jnp.exp2((s_scr[slot, g] - m_new).astype(BF16))
            m_scr[g] = m_new
            pv = jnp.dot(vT_ref[g, :, pl.ds(r0, blk)], p, preferred_element_type=F32)
            acc_scr[g] = alpha * acc_scr[g] + pv

    def step(j, cur, diag_next):
        for g in range(heads):
            scores(j + 1, 1 - cur, tri if diag_next else None, (g,))
            softmax_pv(j, cur, (g,))

    odd_lead = (t + 1) % 2
    scores(0, odd_lead, tri | (t > 0))

    @pl.when((t >= 1) & (odd_lead == 1))
    def _():
        step(0, 1, False)

    def pair(i, c):
        j = odd_lead + 2 * i
        step(j, 0, False)
        step(j + 1, 1, False)
        return c

    lax.fori_loop(0, (t - 1) // 2, pair, 0)

    @pl.when(t >= 1)
    def _():
        step(t - 1, 0, True)

    softmax_pv(t, 1)
    for g in range(heads):
        o_ref[g] = (acc_scr[g, 0:HEAD_DIM] / acc_scr[g, HEAD_DIM:HEAD_DIM + 1]).astype(o_ref.dtype)


def _prompt_attn(qa, ka, vT, heads=N_HEADS):
    bsz, _, aug, seq = qa.shape
    blk = MOBA_BLOCK
    grid = (bsz, N_HEADS // heads, seq // blk)
    resident = lambda r, c: pl.BlockSpec((None, heads, r, c), lambda b, h, t: (b, h, 0, 0),
                                         pipeline_mode=pl.Buffered(1))
    return pl.pallas_call(
        functools.partial(_prompt_attn_kernel, heads=heads),
        grid=grid,
        in_specs=[pl.BlockSpec((None, heads, aug, blk), lambda b, h, t: (b, h, 0, t)),
                  resident(seq, aug), resident(V_AUG, seq)],
        out_specs=pl.BlockSpec((None, heads, HEAD_DIM, blk), lambda b, h, t: (b, h, 0, t)),
        out_shape=jax.ShapeDtypeStruct((bsz, N_HEADS, HEAD_DIM, seq), BF16),
        scratch_shapes=[pltpu.VMEM((heads, 1, blk), F32),
                        pltpu.VMEM((heads, V_AUG, blk), F32), pltpu.VMEM((2, heads, blk, blk), F32),
                        pltpu.VMEM((2, heads, 1, blk), F32)],
        compiler_params=pltpu.CompilerParams(
            dimension_semantics=("arbitrary", "arbitrary", "arbitrary"), vmem_limit_bytes=VMEM_LIMIT),
        name="prompt_attn",
    )(qa, ka, vT)


_GATE_BUFS = 16
_GATE_GROUP = 8


def _top_k_blocks(s_scr, sel_ref, n_blocks):
    lane = lax.broadcasted_iota(jnp.int32, (1, LANES), 1)
    gate = jnp.zeros((N_HEADS, LANES), F32)
    for j in range(n_blocks):
        ssum = jnp.sum(s_scr[j], axis=1, keepdims=True) * (1.0 / MOBA_BLOCK)
        gate = jnp.where(lane == j, ssum, gate)
    g = jnp.where(lane < n_blocks, gate, LOWEST)
    sel = jnp.zeros((N_HEADS, LANES), jnp.int32)
    for r in range(MOBA_TOP_K):
        m = jnp.max(g, axis=1, keepdims=True)
        idx = jnp.min(jnp.where(g == m, lane, LANES), axis=1, keepdims=True)
        sel = jnp.where(lane == r, idx, sel)
        g = jnp.where(lane == idx, LOWEST, g)
    sel_ref[...] = sel


def _tail_kernel(*refs, ff_chunk, n_pages):
    if n_pages is None:
        (x_ref, aT_ref, b_ref, sga_ref, sgb_ref, p_ref, wao_ref, wpo_ref, wo_ref, lnm_ref,
         wup_ref, wdn_ref, lnp_ref, wpg_ref, wpp_ref, y_ref) = refs
    else:
        (pt_ref, x_ref, aT_ref, b_ref, sga_ref, sgb_ref, p_ref, wao_ref, wpo_ref, wo_ref, lnm_ref,
         wup_ref, wdn_ref, lnp_ref, wpg_ref, wpp_ref, q_ref, kc_ref, y_ref, sel_ref,
         kbuf, sem, s_scr, qb_scr) = refs
        seq_i = pl.program_id(0) * pl.num_programs(1) + pl.program_id(1)
        n_seq = pl.num_programs(0) * pl.num_programs(1)
        n_blocks = n_pages // PAGES_PER_BLOCK

        def page_copy(s_i, blk_i, slot, pg):
            page = pt_ref[s_i * n_pages + blk_i * PAGES_PER_BLOCK + pg]
            return pltpu.make_async_copy(kc_ref.at[page], kbuf.at[slot, pg], sem.at[slot, pg])

        def start(s_i, blk_i, slot):
            for pg in range(PAGES_PER_BLOCK):
                page_copy(s_i, blk_i, slot, pg).start()

        def wait_group(k):
            for blk_i in range(k * _GATE_GROUP, (k + 1) * _GATE_GROUP):
                for pg in range(PAGES_PER_BLOCK):
                    page_copy(seq_i, blk_i, blk_i % _GATE_BUFS, pg).wait()

        def score_group(k):
            for h in range(N_HEADS):
                qh = qb_scr[h]
                for blk_i in range(k * _GATE_GROUP, (k + 1) * _GATE_GROUP):
                    slot = blk_i % _GATE_BUFS
                    ksum = kbuf[slot, 0, h]
                    for pg in range(1, PAGES_PER_BLOCK):
                        ksum = ksum + kbuf[slot, pg, h]
                    s_scr[blk_i, h:h + 1, :] = jnp.sum(ksum * qh, axis=0, keepdims=True)

        def refill_group(k):
            blks = range(k * _GATE_GROUP, (k + 1) * _GATE_GROUP)
            for blk_i in blks:
                if blk_i + _GATE_BUFS < n_blocks:
                    start(seq_i, blk_i + _GATE_BUFS, blk_i % _GATE_BUFS)
            wrap = [blk_i for blk_i in blks if blk_i + _GATE_BUFS >= n_blocks]
            if wrap:
                @pl.when(seq_i + 1 < n_seq)
                def _():
                    for blk_i in wrap:
                        start(seq_i + 1, blk_i + _GATE_BUFS - n_blocks, blk_i % _GATE_BUFS)

        @pl.when(seq_i == 0)
        def _():
            for j in range(_GATE_BUFS):
                start(0, j, j)

        qb_scr[...] = jnp.broadcast_to(q_ref[...], qb_scr.shape)

    a_proj = lax.dot_general(aT_ref[...], wao_ref[...], _TN, preferred_element_type=F32)
    b_proj = jnp.dot(b_ref[...], wpo_ref[...], preferred_element_type=F32)
    m = sga_ref[...].astype(F32) * a_proj + sgb_ref[...].astype(F32) * b_proj
    x1 = x_ref[...] + jnp.dot(m.astype(BF16), wo_ref[...], preferred_element_type=F32)
    xn = _rms_rows(x1, lnm_ref[...]).astype(BF16)
    acc = x1
    for c in range(D_FF // ff_chunk):
        if n_pages is not None:
            wait_group(c)
        hid = jnp.maximum(jnp.dot(xn, wup_ref[:, c * ff_chunk:(c + 1) * ff_chunk],
                                  preferred_element_type=F32), 0.0)
        acc = acc + jnp.dot((hid * hid).astype(BF16), wdn_ref[c * ff_chunk:(c + 1) * ff_chunk, :],
                            preferred_element_type=F32)
        if n_pages is not None:
            score_group(c)
            refill_group(c)
    x2 = acc
    if n_pages is not None:
        _top_k_blocks(s_scr, sel_ref, n_blocks)
    xg = _rms_rows(x2, lnp_ref[...]).astype(BF16)
    gate = jax.nn.sigmoid(jnp.dot(xg, wpg_ref[...], preferred_element_type=F32))
    emb = jnp.dot(p_ref[...].astype(BF16), wpp_ref[...], preferred_element_type=F32)
    y_ref[...] = x2 + gate * emb


def _tail(x, aT, b, sga, sgb, p, wao, wpo, wo, lnm, wup, wdn, lnp, wpg, wpp, tm, gate_inputs=None):
    bsz, seq, _ = x.shape
    grid = (bsz, seq // tm)
    n_i = seq // tm
    row_spec = lambda w: pl.BlockSpec((None, tm, w), lambda b_, i, *_: (b_, i, 0))
    in_specs = [
        row_spec(D_MODEL),
        pl.BlockSpec((None, ATTN_WIDTH, tm), lambda b_, i, *_: (b_, 0, i)),
        row_spec(POOL_WIDTH), row_spec(D_MODEL), row_spec(D_MODEL), row_spec(PLE_DIM),
    ] + [_const_spec(w.shape) for w in (wao, wpo, wo, lnm, wup, wdn, lnp, wpg, wpp)]
    y_shape = jax.ShapeDtypeStruct((bsz, seq, D_MODEL), F32)
    params = pltpu.CompilerParams(dimension_semantics=("arbitrary", "arbitrary"), vmem_limit_bytes=VMEM_LIMIT)
    args = (x, aT, b, sga, sgb, p, wao, wpo, wo, lnm, wup, wdn, lnp, wpg, wpp)
    if gate_inputs is None:
        return pl.pallas_call(
            functools.partial(_tail_kernel, ff_chunk=D_MODEL, n_pages=None),
            grid=grid, in_specs=in_specs, out_specs=row_spec(D_MODEL), out_shape=y_shape,
            compiler_params=params, name="tail",
        )(*args)
    pt_flat, q_col, cache_kT, n_pages = gate_inputs
    n_seq = q_col.shape[0]
    n_blocks = n_pages // PAGES_PER_BLOCK
    n_groups = n_blocks // _GATE_GROUP
    assert bsz * n_i == n_seq and n_blocks % _GATE_BUFS == 0 and _GATE_BUFS % _GATE_GROUP == 0
    assert D_FF % n_groups == 0
    seq_spec = lambda shape: pl.BlockSpec((None,) + shape, lambda b_, i, *_: (b_ * n_i + i,) + (0,) * len(shape))
    grid_spec = pltpu.PrefetchScalarGridSpec(
        num_scalar_prefetch=1, grid=grid,
        in_specs=in_specs + [seq_spec((N_HEADS, HEAD_DIM, 1)), pl.BlockSpec(memory_space=pl.ANY)],
        out_specs=(row_spec(D_MODEL), seq_spec((N_HEADS, LANES))),
        scratch_shapes=[pltpu.VMEM((_GATE_BUFS, PAGES_PER_BLOCK, N_HEADS, HEAD_DIM, PAGE_SIZE), F32),
                        pltpu.SemaphoreType.DMA((_GATE_BUFS, PAGES_PER_BLOCK)),
                        pltpu.VMEM((n_blocks, N_HEADS, LANES), F32),
                        pltpu.VMEM((N_HEADS, HEAD_DIM, LANES), F32)],
    )
    return pl.pallas_call(
        functools.partial(_tail_kernel, ff_chunk=D_FF // n_groups, n_pages=n_pages),
        grid_spec=grid_spec,
        out_shape=(y_shape, jax.ShapeDtypeStruct((n_seq, N_HEADS, LANES), jnp.int32)),
        compiler_params=params, name="tail_gate",
    )(pt_flat, *args, q_col, cache_kT)


def _sample_in_kernel(x_ref, ln_ref, win_ref, qn_ref, kn_ref, hist_ref, poolw_ref, pscale_ref,
                      q_ref, k_ref, v_ref, b_ref, sga_ref, sgb_ref, hist_out_ref, *, past_len):
    xn = _rms_rows(x_ref[...], ln_ref[...]).astype(BF16)
    z = jnp.dot(xn, win_ref[...], preferred_element_type=F32)
    for h in range(N_HEADS):
        c0 = h * HEAD_DIM
        q_ref[:, c0:c0 + HEAD_DIM] = _rms_rows(z[:, c0:c0 + HEAD_DIM], qn_ref[...])
        k_ref[:, c0:c0 + HEAD_DIM] = _rms_rows(
            z[:, ATTN_WIDTH + c0:ATTN_WIDTH + c0 + HEAD_DIM], kn_ref[...])
    v_ref[...] = z[:, 2 * ATTN_WIDTH:3 * ATTN_WIDTH]
    u0 = 3 * ATTN_WIDTH
    u = z[:, u0:u0 + POOL_WIDTH]
    for g_i, w in enumerate(POOL_WINDOWS):
        c0 = g_i * POOL_GROUP
        cur = u[:, c0:c0 + POOL_GROUP]
        acc = cur
        for k in range(1, w):
            acc = acc + hist_ref[POOL_HIST - k, :, c0:c0 + POOL_GROUP]
        cnt = float(min(w, past_len + 1))
        d = acc / cnt - cur
        y = jnp.dot(d.astype(BF16), poolw_ref[g_i], preferred_element_type=F32)
        b_ref[:, c0:c0 + POOL_GROUP] = (y * pscale_ref[:, c0:c0 + POOL_GROUP]).astype(BF16)
    for t in range(POOL_HIST - 1):
        hist_out_ref[t] = hist_ref[t + 1]
    hist_out_ref[POOL_HIST - 1] = u
    g0 = u0 + POOL_WIDTH
    sga_ref[...] = jax.nn.sigmoid(z[:, g0:g0 + D_MODEL]).astype(BF16)
    sgb_ref[...] = jax.nn.sigmoid(z[:, g0 + D_MODEL:g0 + 2 * D_MODEL]).astype(BF16)


def _sample_in(x, ln, win, qn_row, kn_row, hist_t, poolw, pscale, past_len):
    n = x.shape[0]
    sds = jax.ShapeDtypeStruct
    out_shape = (sds((n, ATTN_WIDTH), F32), sds((n, ATTN_WIDTH), F32), sds((n, ATTN_WIDTH), F32),
                 sds((n, POOL_WIDTH), BF16), sds((n, D_MODEL), BF16), sds((n, D_MODEL), BF16),
                 sds(hist_t.shape, F32))
    return pl.pallas_call(
        functools.partial(_sample_in_kernel, past_len=past_len),
        out_shape=out_shape,
        compiler_params=pltpu.CompilerParams(vmem_limit_bytes=VMEM_LIMIT),
        name="sample_in",
    )(x, ln, win, qn_row, kn_row, hist_t, poolw, pscale)


def _sample_attn_kernel(pt_ref, sel_ref, q_ref, kn_ref, vn_ref, kc_ref, vc_ref, o_ref,
                        kbuf, vbuf, sem, *, n_pages):
    b = pl.program_id(0)
    n_b = pl.num_programs(0)
    n_slots = MOBA_TOP_K * PAGES_PER_BLOCK

    def copies(seq_i, buf_i):
        out = []
        for h in range(N_HEADS):
            for r in range(MOBA_TOP_K):
                blk = sel_ref[(seq_i * N_HEADS + h) * MOBA_TOP_K + r]
                for pg in range(PAGES_PER_BLOCK):
                    page = pt_ref[seq_i * n_pages + blk * PAGES_PER_BLOCK + pg]
                    s_i = r * PAGES_PER_BLOCK + pg
                    out.append(pltpu.make_async_copy(kc_ref.at[page, h], kbuf.at[buf_i, h, s_i], sem.at[buf_i, 0]))
                    out.append(pltpu.make_async_copy(vc_ref.at[page, h], vbuf.at[buf_i, h, s_i], sem.at[buf_i, 1]))
        return out

    @pl.when(b == 0)
    def _():
        for c in copies(0, 0):
            c.start()

    cur = b % 2

    @pl.when(b + 1 < n_b)
    def _():
        for c in copies(b + 1, 1 - cur):
            c.start()

    for c in copies(b, cur):
        c.wait()

    for h in range(N_HEADS):
        qh = q_ref[h]
        kb = kbuf[cur, h]
        vb = vbuf[cur, h]
        s = jnp.sum(kb * qh[None], axis=1) * ATTN_SCALE
        s_new = jnp.sum(qh * kn_ref[h], axis=0, keepdims=True) * ATTN_SCALE
        m = jnp.maximum(jnp.max(jnp.max(s, axis=1, keepdims=True), axis=0, keepdims=True), s_new)
        p = jnp.exp(s - m)
        p_new = jnp.exp(s_new - m)
        l = jnp.sum(jnp.sum(p, axis=1, keepdims=True), axis=0, keepdims=True) + p_new
        pv = vb[0] * p[0:1]
        for s_i in range(1, n_slots):
            pv = pv + vb[s_i] * p[s_i:s_i + 1]
        o = jnp.sum(pv, axis=1, keepdims=True) + p_new * vn_ref[h]
        o_ref[h] = o / l


def _sample_attn(page_table_flat, sel_flat, q_col, kn_col, vn_col, cache_kT, cache_vT, n_seq, n_pages):
    col = pl.BlockSpec((None, N_HEADS, HEAD_DIM, 1), lambda b, pt, sel: (b, 0, 0, 0))
    n_slots = MOBA_TOP_K * PAGES_PER_BLOCK
    buf = pltpu.VMEM((2, N_HEADS, n_slots, HEAD_DIM, PAGE_SIZE), F32)
    grid_spec = pltpu.PrefetchScalarGridSpec(
        num_scalar_prefetch=2,
        grid=(n_seq,),
        in_specs=[col, col, col, pl.BlockSpec(memory_space=pl.ANY), pl.BlockSpec(memory_space=pl.ANY)],
        out_specs=col,
        scratch_shapes=[buf, buf, pltpu.SemaphoreType.DMA((2, 2))],
    )
    return pl.pallas_call(
        functools.partial(_sample_attn_kernel, n_pages=n_pages),
        grid_spec=grid_spec,
        out_shape=jax.ShapeDtypeStruct((n_seq, N_HEADS, HEAD_DIM, 1), F32),
        compiler_params=pltpu.CompilerParams(
            dimension_semantics=("arbitrary",), vmem_limit_bytes=VMEM_LIMIT),
        name="sample_attn",
    )(page_table_flat, sel_flat, q_col, kn_col, vn_col, cache_kT, cache_vT)


def kernel(x_prompt, x_sample, cache_k, cache_v, state_pool, page_table, p_prompt, p_sample, ln_mix, w_in,
           q_norm, k_norm, pool_w, pool_scale, w_attn_out, w_pool_out, w_out, ln_mlp, w_up, w_down, ln_ple,
           w_ple_gate, w_ple_proj):
    depth = w_in.shape[0]
    assert depth == 1
    n_seq, n_pages = page_table.shape
    past_len = n_pages * PAGE_SIZE
    assert x_sample.shape[1] == 1 and past_len % MOBA_BLOCK == 0 and past_len // MOBA_BLOCK >= MOBA_TOP_K
    assert (past_len // MOBA_BLOCK) % _GATE_BUFS == 0 and (x_prompt.shape[1] // MOBA_BLOCK) % 16 == 0
    bsz, seq, _ = x_prompt.shape
    l = 0

    w_in_b = w_in[l].astype(BF16)
    wqkvT = w_in_b[:, :3 * ATTN_WIDTH].T
    wrest = w_in_b[:, 3 * ATTN_WIDTH:]
    poolw = pool_w[l].astype(BF16)
    tail_w = (w_attn_out[l].astype(BF16), w_pool_out[l].astype(BF16), w_out[l].astype(BF16), ln_mlp,
              w_up[l].astype(BF16), w_down[l].astype(BF16), ln_ple, w_ple_gate[l].astype(BF16),
              w_ple_proj[l].astype(BF16))

    xs = x_sample.reshape(n_seq, D_MODEL)
    hist_t = state_pool[l].transpose(1, 0, 2)
    q_s, k_s, v_s, b_s, sga_s, sgb_s, hist_s = _sample_in(
        xs, ln_mix, w_in_b, q_norm, k_norm, hist_t, poolw, pool_scale, past_len)
    cache_kT = cache_k[l].transpose(0, 2, 3, 1)
    cache_vT = cache_v[l].transpose(0, 2, 3, 1)
    pt_flat = page_table.reshape(-1)
    q_col = q_s.reshape(n_seq, N_HEADS, HEAD_DIM, 1)

    (kT, vT, qa, ka, vTb, b_p, sga, sgb, hist_p) = _prompt_in(
        x_prompt, ln_mix, wqkvT, wrest, q_norm.reshape(HEAD_DIM, 1), k_norm.reshape(HEAD_DIM, 1),
        poolw, pool_scale)
    aT = _prompt_attn(qa, ka, vTb).reshape(bsz, ATTN_WIDTH, seq)
    y_prompt, sel = _tail(x_prompt, aT, b_p, sga, sgb, p_prompt[l], *tail_w, tm=512,
                          gate_inputs=(pt_flat, q_col, cache_kT, n_pages))
    k_prompt = kT.transpose(0, 3, 1, 2)[None]
    v_prompt = vT.transpose(0, 3, 1, 2)[None]
    pool_prompt = hist_p[:, HIST_ROWS - POOL_HIST:][None]

    sel_flat = sel[:, :, :MOBA_TOP_K].reshape(-1)
    a_s = _sample_attn(pt_flat, sel_flat, q_col, k_s.reshape(n_seq, N_HEADS, HEAD_DIM, 1),
                       v_s.reshape(n_seq, N_HEADS, HEAD_DIM, 1), cache_kT, cache_vT, n_seq, n_pages)
    aT_s = a_s.reshape(n_seq, ATTN_WIDTH).T.astype(BF16)[None]
    y_sample = _tail(xs[None], aT_s, b_s[None], sga_s[None], sgb_s[None], p_sample[l].reshape(1, n_seq, PLE_DIM),
                     *tail_w, tm=n_seq)
    y_sample = y_sample.reshape(n_seq, 1, D_MODEL)
    k_sample = k_s.reshape(1, n_seq, 1, N_HEADS, HEAD_DIM)
    v_sample = v_s.reshape(1, n_seq, 1, N_HEADS, HEAD_DIM)
    pool_sample = hist_s.transpose(1, 0, 2)[None]
    return (y_prompt, y_sample, k_prompt, v_prompt, pool_prompt, k_sample, v_sample, pool_sample)
```

```python
import functools

import jax
import jax.numpy as jnp
from jax import lax
from jax.experimental import pallas as pl
from jax.experimental.pallas import tpu as pltpu

D_MODEL = 1024
N_HEADS = 8
HEAD_DIM = 64
ATTN_WIDTH = N_HEADS * HEAD_DIM
MOBA_BLOCK = 256
MOBA_TOP_K = 3
PAGE_SIZE = 128
PAGES_PER_BLOCK = MOBA_BLOCK // PAGE_SIZE
ATTN_SCALE = HEAD_DIM ** -0.5
POOL_WINDOWS = (2, 4, 8, 16)
POOL_GROUP = 128
POOL_WIDTH = 512
POOL_HIST = 15
HIST_ROWS = 16
V_AUG = HEAD_DIM + 16
D_FF = 4 * D_MODEL
PLE_DIM = 256
EPS = 1e-6
NEG_INF = -1e30
LOWEST = -3e38
LOG2E = 1.4426950408889634

LANES = 128
VMEM_LIMIT = 60 * 1024 * 1024

F32 = jnp.float32
BF16 = jnp.bfloat16

_NT = (((1,), (1,)), ((), ()))
_TN = (((0,), (0,)), ((), ()))


def _rms_rows(x, g):
    ms = jnp.mean(x * x, axis=-1, keepdims=True)
    return x * lax.rsqrt(ms + EPS) * g


def _split_bf16(a):
    hi = a.astype(BF16)
    lo = (a - hi.astype(F32)).astype(BF16)
    return hi, lo


def _const_spec(shape):
    zeros = (0,) * len(shape)
    return pl.BlockSpec(shape, lambda *_: zeros, pipeline_mode=pl.Buffered(1))


def _pool_means(e_ref, tm, pos0):
    pos = pos0 + lax.broadcasted_iota(jnp.int32, (tm, 1), 0)
    out = []
    for g, w in enumerate(POOL_WINDOWS):
        c0 = g * POOL_GROUP
        cur = e_ref[HIST_ROWS:HIST_ROWS + tm, c0:c0 + POOL_GROUP]
        acc = cur
        for k in range(1, w):
            acc = acc + e_ref[HIST_ROWS - k:HIST_ROWS - k + tm, c0:c0 + POOL_GROUP]
        cnt = jnp.minimum(w, pos + 1).astype(F32)
        out.append(acc / cnt - cur)
    return out


def _prompt_in_kernel(x_ref, ln_ref, wqkvT_ref, wrest_ref, qn_ref, kn_ref, poolw_ref, pscale_ref,
                      kT_ref, vT_ref, qa_ref, ka_ref, vTb_ref, b_ref, sga_ref, sgb_ref,
                      hist_ref, e_scr, km_scr, *, tm, n_blocks):
    i = pl.program_id(1)
    n_i = pl.num_programs(1)
    blocks_per_tile = tm // MOBA_BLOCK

    @pl.when(i == 0)
    def _():
        e_scr[0:HIST_ROWS, :] = jnp.zeros((HIST_ROWS, POOL_WIDTH), F32)
        km_scr[...] = jnp.zeros(km_scr.shape, F32)

    xn = _rms_rows(x_ref[...], ln_ref[...]).astype(BF16)

    def proj_t(r0):
        t = lax.dot_general(wqkvT_ref[r0:r0 + ATTN_WIDTH, :], xn, _NT, preferred_element_type=F32)
        return t.reshape(N_HEADS, HEAD_DIM, tm)

    def head_norm(t, g_col):
        ms = jnp.mean(t * t, axis=1, keepdims=True)
        return t * lax.rsqrt(ms + EPS) * g_col[None]

    q = head_norm(proj_t(0), qn_ref[...])
    k = head_norm(proj_t(ATTN_WIDTH), kn_ref[...])
    v = proj_t(2 * ATTN_WIDTH)
    kT_ref[...] = k
    vT_ref[...] = v
    vTb_ref[:, 0:HEAD_DIM, :] = v.astype(BF16)
    ones_row = lax.broadcasted_iota(jnp.int32, (N_HEADS, V_AUG - HEAD_DIM, tm), 1) == 0
    vTb_ref[:, HEAD_DIM:V_AUG, :] = jnp.where(ones_row, 1.0, 0.0).astype(BF16)
    qa_ref[:, 0:HEAD_DIM, :] = (q * (ATTN_SCALE * LOG2E)).astype(BF16)
    key_blk = (i * tm + lax.broadcasted_iota(jnp.int32, (tm, 1), 0)) // MOBA_BLOCK
    onehot = jnp.where(key_blk == lax.broadcasted_iota(jnp.int32, (1, n_blocks), 1), 1.0, 0.0).astype(BF16)
    for h in range(N_HEADS):
        ka_ref[h, :, 0:HEAD_DIM] = k[h].T.astype(BF16)
        ka_ref[h, :, HEAD_DIM:HEAD_DIM + n_blocks] = onehot

    lane = lax.broadcasted_iota(jnp.int32, (1, 1, LANES), 2)
    km = km_scr[...]
    for h in range(blocks_per_tile):
        ks = jnp.sum(k[:, :, h * MOBA_BLOCK:(h + 1) * MOBA_BLOCK], axis=2, keepdims=True) * (1.0 / MOBA_BLOCK)
        km = jnp.where(lane == i * blocks_per_tile + h, ks, km)
    km_scr[...] = km

    pos = i * tm + lax.broadcasted_iota(jnp.int32, (1, tm), 1)
    own = pos // MOBA_BLOCK
    blk = lax.broadcasted_iota(jnp.int32, (n_blocks, 1), 0)
    for h in range(N_HEADS):
        km_hi, km_lo = _split_bf16(km[h][:, 0:n_blocks])
        q_hi, q_lo = _split_bf16(q[h])
        gate = (lax.dot_general(km_hi, q_hi, _TN, preferred_element_type=F32)
                + lax.dot_general(km_hi, q_lo, _TN, preferred_element_type=F32)
                + lax.dot_general(km_lo, q_hi, _TN, preferred_element_type=F32))
        g = jnp.where(blk < own, gate, LOWEST)
        sel = blk == own
        for _ in range(MOBA_TOP_K):
            m = jnp.max(g, axis=0, keepdims=True)
            idx = jnp.min(jnp.where(g == m, blk, n_blocks), axis=0, keepdims=True)
            pick = (blk == idx) & (m > LOWEST)
            sel = sel | pick
            g = jnp.where(pick, LOWEST, g)
        qa_ref[h, HEAD_DIM:HEAD_DIM + n_blocks, :] = jnp.where(sel, 0.0, NEG_INF).astype(BF16)

    u = jnp.dot(xn, wrest_ref[:, 0:POOL_WIDTH], preferred_element_type=F32)
    e_scr[HIST_ROWS:HIST_ROWS + tm, :] = u
    d = _pool_means(e_scr, tm, i * tm)
    for g_i in range(len(POOL_WINDOWS)):
        c0 = g_i * POOL_GROUP
        y = jnp.dot(d[g_i].astype(BF16), poolw_ref[g_i], preferred_element_type=F32)
        b_ref[:, c0:c0 + POOL_GROUP] = (y * pscale_ref[:, c0:c0 + POOL_GROUP]).astype(BF16)
    tail = e_scr[tm:tm + HIST_ROWS, :]
    e_scr[0:HIST_ROWS, :] = tail

    @pl.when(i == n_i - 1)
    def _():
        hist_ref[...] = tail

    ga = jnp.dot(xn, wrest_ref[:, POOL_WIDTH:POOL_WIDTH + D_MODEL], preferred_element_type=F32)
    sga_ref[...] = jax.nn.sigmoid(ga).astype(BF16)
    gb = jnp.dot(xn, wrest_ref[:, POOL_WIDTH + D_MODEL:POOL_WIDTH + 2 * D_MODEL], preferred_element_type=F32)
    sgb_ref[...] = jax.nn.sigmoid(gb).astype(BF16)


def _prompt_in(x, ln, wqkvT, wrest, qn_col, kn_col, poolw, pscale, tm=512):
    bsz, seq, _ = x.shape
    n_blocks = seq // MOBA_BLOCK
    grid = (bsz, seq // tm)
    hT = lambda dt: jax.ShapeDtypeStruct((bsz, N_HEADS, HEAD_DIM, seq), dt)
    hT_spec = pl.BlockSpec((None, N_HEADS, HEAD_DIM, tm), lambda b, i: (b, 0, 0, i))
    row_spec = lambda w: pl.BlockSpec((None, tm, w), lambda b, i: (b, i, 0))
    aug = HEAD_DIM + n_blocks
    out_shape = (
        hT(F32), hT(F32),
        jax.ShapeDtypeStruct((bsz, N_HEADS, aug, seq), BF16),
        jax.ShapeDtypeStruct((bsz, N_HEADS, seq, aug), BF16),
        jax.ShapeDtypeStruct((bsz, N_HEADS, V_AUG, seq), BF16),
        jax.ShapeDtypeStruct((bsz, seq, POOL_WIDTH), BF16),
        jax.ShapeDtypeStruct((bsz, seq, D_MODEL), BF16),
        jax.ShapeDtypeStruct((bsz, seq, D_MODEL), BF16),
        jax.ShapeDtypeStruct((bsz, HIST_ROWS, POOL_WIDTH), F32),
    )
    out_specs = (
        hT_spec, hT_spec,
        pl.BlockSpec((None, N_HEADS, aug, tm), lambda b, i: (b, 0, 0, i)),
        pl.BlockSpec((None, N_HEADS, tm, aug), lambda b, i: (b, 0, i, 0)),
        pl.BlockSpec((None, N_HEADS, V_AUG, tm), lambda b, i: (b, 0, 0, i)),
        row_spec(POOL_WIDTH), row_spec(D_MODEL), row_spec(D_MODEL),
        pl.BlockSpec((None, HIST_ROWS, POOL_WIDTH), lambda b, i: (b, 0, 0)),
    )
    in_specs = [
        row_spec(D_MODEL),
        _const_spec(ln.shape), _const_spec(wqkvT.shape), _const_spec(wrest.shape),
        _const_spec(qn_col.shape), _const_spec(kn_col.shape),
        _const_spec(poolw.shape), _const_spec(pscale.shape),
    ]
    return pl.pallas_call(
        functools.partial(_prompt_in_kernel, tm=tm, n_blocks=n_blocks),
        grid=grid, in_specs=in_specs, out_specs=out_specs, out_shape=out_shape,
        scratch_shapes=[pltpu.VMEM((HIST_ROWS + tm, POOL_WIDTH), F32),
                        pltpu.VMEM((N_HEADS, HEAD_DIM, LANES), F32)],
        compiler_params=pltpu.CompilerParams(
            dimension_semantics=("arbitrary", "arbitrary"), vmem_limit_bytes=VMEM_LIMIT),
        name="prompt_in",
    )(x, ln, wqkvT, wrest, qn_col, kn_col, poolw, pscale)


def _prompt_attn_kernel(qa_ref, ka_ref, vT_ref, o_ref, m_scr, acc_scr, s_scr, cm_scr, *, heads):
    t = pl.program_id(2)
    blk = MOBA_BLOCK
    tri = (lax.broadcasted_iota(jnp.int32, (blk, 1), 0) <= lax.broadcasted_iota(jnp.int32, (1, blk), 1))
    m_scr[...] = jnp.full(m_scr.shape, NEG_INF, F32)
    acc_scr[...] = jnp.zeros(acc_scr.shape, F32)

    def scores(j, slot, mask, gs=tuple(range(heads))):
        r0 = j * blk if isinstance(j, int) else pl.multiple_of(j * blk, blk)
        for g in gs:
            s = jnp.dot(ka_ref[g, pl.ds(r0, blk), :], qa_ref[g], preferred_element_type=F32)
            if mask is not None:
                s = jnp.where(mask, s, NEG_INF)
            s_scr[slot, g] = s
            cm_scr[slot, g] = jnp.max(s, axis=0, keepdims=True)

    def softmax_pv(j, slot, gs=tuple(range(heads))):
        r0 = j * blk if isinstance(j, int) else pl.multiple_of(j * blk, blk)
        for g in gs:
            m = m_scr[g]
            m_new = jnp.maximum(m, cm_scr[slot, g])
            alpha = jnp.exp2(m - m_new)
            p = jnp.exp2((s_scr[slot, g] - m_new).astype(BF16))
            m_scr[g] = m_new
            pv = jnp.dot(vT_ref[g, :, pl.ds(r0, blk)], p, preferred_element_type=F32)
            acc_scr[g] = alpha * acc_scr[g] + pv

    def step(j, cur, diag_next):
        for g in range(heads):
            scores(j + 1, 1 - cur, tri if diag_next else None, (g,))
            softmax_pv(j, cur, (g,))

    odd_lead = (t + 1) % 2
    scores(0, odd_lead, tri | (t > 0))

    @pl.when((t >= 1) & (odd_lead == 1))
    def _():
        step(0, 1, False)

    def pair(i, c):
        j = odd_lead + 2 * i
        step(j, 0, False)
        step(j + 1, 1, False)
        return c

    lax.fori_loop(0, (t - 1) // 2, pair, 0)

    @pl.when(t >= 1)
    def _():
        step(t - 1, 0, True)

    softmax_pv(t, 1)
    for g in range(heads):
        o_ref[g] = (acc_scr[g, 0:HEAD_DIM] / acc_scr[g, HEAD_DIM:HEAD_DIM + 1]).astype(o_ref.dtype)


def _prompt_attn(qa, ka, vT, heads=N_HEADS):
    bsz, _, aug, seq = qa.shape
    blk = MOBA_BLOCK
    grid = (bsz, N_HEADS // heads, seq // blk)
    resident = lambda r, c: pl.BlockSpec((None, heads, r, c), lambda b, h, t: (b, h, 0, 0),
                                         pipeline_mode=pl.Buffered(1))
    return pl.pallas_call(
        functools.partial(_prompt_attn_kernel, heads=heads),
        grid=grid,
        in_specs=[pl.BlockSpec((None, heads, aug, blk), lambda b, h, t: (b, h, 0, t)),
                  resident(seq, aug), resident(V_AUG, seq)],
        out_specs=pl.BlockSpec((None, heads, HEAD_DIM, blk), lambda b, h, t: (b, h, 0, t)),
        out_shape=jax.ShapeDtypeStruct((bsz, N_HEADS, HEAD_DIM, seq), BF16),
        scratch_shapes=[pltpu.VMEM((heads, 1, blk), F32),
                        pltpu.VMEM((heads, V_AUG, blk), F32), pltpu.VMEM((2, heads, blk, blk), F32),
                        pltpu.VMEM((2, heads, 1, blk), F32)],
        compiler_params=pltpu.CompilerParams(
            dimension_semantics=("arbitrary", "arbitrary", "arbitrary"), vmem_limit_bytes=VMEM_LIMIT),
        name="prompt_attn",
    )(qa, ka, vT)


_GATE_BUFS = 16
_GATE_GROUP = 8


def _seq_column(t, seq_i):
    lane = lax.broadcasted_iota(jnp.int32, (1, 1, t.shape[2]), 2)
    return jnp.sum(jnp.where(lane == seq_i, t, 0.0), axis=2, keepdims=True)


def _top_k_blocks(s_scr, sel_ref, n_blocks):
    lane = lax.broadcasted_iota(jnp.int32, (1, LANES), 1)
    gate = jnp.zeros((N_HEADS, LANES), F32)
    for j in range(n_blocks):
        ssum = jnp.sum(s_scr[j], axis=1, keepdims=True) * (1.0 / MOBA_BLOCK)
        gate = jnp.where(lane == j, ssum, gate)
    g = jnp.where(lane < n_blocks, gate, LOWEST)
    sel = jnp.zeros((N_HEADS, LANES), jnp.int32)
    for r in range(MOBA_TOP_K):
        m = jnp.max(g, axis=1, keepdims=True)
        idx = jnp.min(jnp.where(g == m, lane, LANES), axis=1, keepdims=True)
        sel = jnp.where(lane == r, idx, sel)
        g = jnp.where(lane == idx, LOWEST, g)
    sel_ref[...] = sel


def _tail_kernel(*refs, ff_chunk, n_pages):
    if n_pages is None:
        (x_ref, aT_ref, b_ref, sga_ref, sgb_ref, p_ref, wao_ref, wpo_ref, wo_ref, lnm_ref,
         wup_ref, wdn_ref, lnp_ref, wpg_ref, wpp_ref, y_ref) = refs
    else:
        (pt_ref, x_ref, aT_ref, b_ref, sga_ref, sgb_ref, p_ref, wao_ref, wpo_ref, wo_ref, lnm_ref,
         wup_ref, wdn_ref, lnp_ref, wpg_ref, wpp_ref, q_ref, kc_ref, y_ref, sel_ref,
         kbuf, sem, s_scr, qb_scr) = refs
        seq_i = pl.program_id(0) * pl.num_programs(1) + pl.program_id(1)
        n_seq = pl.num_programs(0) * pl.num_programs(1)
        n_blocks = n_pages // PAGES_PER_BLOCK

        def page_copy(s_i, blk_i, slot, pg):
            page = pt_ref[s_i * n_pages + blk_i * PAGES_PER_BLOCK + pg]
            return pltpu.make_async_copy(kc_ref.at[page], kbuf.at[slot, pg], sem.at[slot, pg])

        def start(s_i, blk_i, slot):
            for pg in range(PAGES_PER_BLOCK):
                page_copy(s_i, blk_i, slot, pg).start()

        def wait_group(k):
            for blk_i in range(k * _GATE_GROUP, (k + 1) * _GATE_GROUP):
                for pg in range(PAGES_PER_BLOCK):
                    page_copy(seq_i, blk_i, blk_i % _GATE_BUFS, pg).wait()

        def score_group(k):
            for h in range(N_HEADS):
                qh = qb_scr[h]
                for blk_i in range(k * _GATE_GROUP, (k + 1) * _GATE_GROUP):
                    slot = blk_i % _GATE_BUFS
                    ksum = kbuf[slot, 0, h]
                    for pg in range(1, PAGES_PER_BLOCK):
                        ksum = ksum + kbuf[slot, pg, h]
                    s_scr[blk_i, h:h + 1, :] = jnp.sum(ksum * qh, axis=0, keepdims=True)

        def refill_group(k):
            blks = range(k * _GATE_GROUP, (k + 1) * _GATE_GROUP)
            for blk_i in blks:
                if blk_i + _GATE_BUFS < n_blocks:
                    start(seq_i, blk_i + _GATE_BUFS, blk_i % _GATE_BUFS)
            wrap = [blk_i for blk_i in blks if blk_i + _GATE_BUFS >= n_blocks]
            if wrap:
                @pl.when(seq_i + 1 < n_seq)
                def _():
                    for blk_i in wrap:
                        start(seq_i + 1, blk_i + _GATE_BUFS - n_blocks, blk_i % _GATE_BUFS)

        @pl.when(seq_i == 0)
        def _():
            for j in range(_GATE_BUFS):
                start(0, j, j)

        qb_scr[...] = jnp.broadcast_to(_seq_column(q_ref[0], seq_i), qb_scr.shape)

    a_proj = lax.dot_general(aT_ref[...], wao_ref[...], _TN, preferred_element_type=F32)
    b_proj = jnp.dot(b_ref[...], wpo_ref[...], preferred_element_type=F32)
    m = sga_ref[...].astype(F32) * a_proj + sgb_ref[...].astype(F32) * b_proj
    x1 = x_ref[...] + jnp.dot(m.astype(BF16), wo_ref[...], preferred_element_type=F32)
    xn = _rms_rows(x1, lnm_ref[...]).astype(BF16)
    acc = x1
    for c in range(D_FF // ff_chunk):
        if n_pages is not None:
            wait_group(c)
            score_group(c)
            refill_group(c)
        hid = jnp.maximum(jnp.dot(xn, wup_ref[:, c * ff_chunk:(c + 1) * ff_chunk],
                                  preferred_element_type=F32), 0.0)
        acc = acc + jnp.dot((hid * hid).astype(BF16), wdn_ref[c * ff_chunk:(c + 1) * ff_chunk, :],
                            preferred_element_type=F32)
    x2 = acc
    if n_pages is not None:
        _top_k_blocks(s_scr, sel_ref, n_blocks)
    xg = _rms_rows(x2, lnp_ref[...]).astype(BF16)
    gate = jax.nn.sigmoid(jnp.dot(xg, wpg_ref[...], preferred_element_type=F32))
    emb = jnp.dot(p_ref[...].astype(BF16), wpp_ref[...], preferred_element_type=F32)
    y_ref[...] = x2 + gate * emb


def _tail(x, aT, b, sga, sgb, p, wao, wpo, wo, lnm, wup, wdn, lnp, wpg, wpp, tm, gate_inputs=None):
    bsz, seq, _ = x.shape
    grid = (bsz, seq // tm)
    n_i = seq // tm
    row_spec = lambda w: pl.BlockSpec((None, tm, w), lambda b_, i, *_: (b_, i, 0))
    in_specs = [
        row_spec(D_MODEL),
        pl.BlockSpec((None, ATTN_WIDTH, tm), lambda b_, i, *_: (b_, 0, i)),
        row_spec(POOL_WIDTH), row_spec(D_MODEL), row_spec(D_MODEL), row_spec(PLE_DIM),
    ] + [_const_spec(w.shape) for w in (wao, wpo, wo, lnm, wup, wdn, lnp, wpg, wpp)]
    y_shape = jax.ShapeDtypeStruct((bsz, seq, D_MODEL), F32)
    params = pltpu.CompilerParams(dimension_semantics=("arbitrary", "arbitrary"), vmem_limit_bytes=VMEM_LIMIT)
    args = (x, aT, b, sga, sgb, p, wao, wpo, wo, lnm, wup, wdn, lnp, wpg, wpp)
    if gate_inputs is None:
        return pl.pallas_call(
            functools.partial(_tail_kernel, ff_chunk=D_MODEL, n_pages=None),
            grid=grid, in_specs=in_specs, out_specs=row_spec(D_MODEL), out_shape=y_shape,
            compiler_params=params, name="tail",
        )(*args)
    pt_flat, qkvT, cache_kT, n_pages = gate_inputs
    n_seq = qkvT.shape[3]
    n_blocks = n_pages // PAGES_PER_BLOCK
    n_groups = n_blocks // _GATE_GROUP
    assert bsz * n_i == n_seq and n_blocks % _GATE_BUFS == 0 and _GATE_BUFS % _GATE_GROUP == 0
    assert D_FF % n_groups == 0
    seq_spec = lambda shape: pl.BlockSpec((None,) + shape, lambda b_, i, *_: (b_ * n_i + i,) + (0,) * len(shape))
    grid_spec = pltpu.PrefetchScalarGridSpec(
        num_scalar_prefetch=1, grid=grid,
        in_specs=in_specs + [_const_spec(qkvT.shape), pl.BlockSpec(memory_space=pl.ANY)],
        out_specs=(row_spec(D_MODEL), seq_spec((N_HEADS, LANES))),
        scratch_shapes=[pltpu.VMEM((_GATE_BUFS, PAGES_PER_BLOCK, N_HEADS, HEAD_DIM, PAGE_SIZE), F32),
                        pltpu.SemaphoreType.DMA((_GATE_BUFS, PAGES_PER_BLOCK)),
                        pltpu.VMEM((n_blocks, N_HEADS, LANES), F32),
                        pltpu.VMEM((N_HEADS, HEAD_DIM, LANES), F32)],
    )
    return pl.pallas_call(
        functools.partial(_tail_kernel, ff_chunk=D_FF // n_groups, n_pages=n_pages),
        grid_spec=grid_spec,
        out_shape=(y_shape, jax.ShapeDtypeStruct((n_seq, N_HEADS, LANES), jnp.int32)),
        compiler_params=params, name="tail_gate",
    )(pt_flat, *args, qkvT, cache_kT)


def _sample_in_kernel(x_ref, ln_ref, win_ref, wqkvT_ref, qn_ref, kn_ref, qn_col_ref, kn_col_ref, hist_ref,
                      poolw_ref, pscale_ref,
                      k_ref, v_ref, qkvT_ref, b_ref, sga_ref, sgb_ref, hist_out_ref, *, past_len):
    n = x_ref.shape[0]
    xn = _rms_rows(x_ref[...], ln_ref[...]).astype(BF16)
    z = jnp.dot(xn, win_ref[...], preferred_element_type=F32)
    for h in range(N_HEADS):
        c0 = h * HEAD_DIM
        k_ref[:, c0:c0 + HEAD_DIM] = _rms_rows(
            z[:, ATTN_WIDTH + c0:ATTN_WIDTH + c0 + HEAD_DIM], kn_ref[...])
    v_ref[...] = z[:, 2 * ATTN_WIDTH:3 * ATTN_WIDTH]
    zT = lax.dot_general(wqkvT_ref[...], xn, _NT, preferred_element_type=F32).reshape(3, N_HEADS, HEAD_DIM, n)

    def head_norm(t, g_col):
        ms = jnp.mean(t * t, axis=1, keepdims=True)
        return t * lax.rsqrt(ms + EPS) * g_col[None]

    qkvT_ref[0] = head_norm(zT[0], qn_col_ref[...])
    qkvT_ref[1] = head_norm(zT[1], kn_col_ref[...])
    qkvT_ref[2] = zT[2]
    u0 = 3 * ATTN_WIDTH
    u = z[:, u0:u0 + POOL_WIDTH]
    for g_i, w in enumerate(POOL_WINDOWS):
        c0 = g_i * POOL_GROUP
        cur = u[:, c0:c0 + POOL_GROUP]
        acc = cur
        for k in range(1, w):
            acc = acc + hist_ref[POOL_HIST - k, :, c0:c0 + POOL_GROUP]
        cnt = float(min(w, past_len + 1))
        d = acc / cnt - cur
        y = jnp.dot(d.astype(BF16), poolw_ref[g_i], preferred_element_type=F32)
        b_ref[:, c0:c0 + POOL_GROUP] = (y * pscale_ref[:, c0:c0 + POOL_GROUP]).astype(BF16)
    for t in range(POOL_HIST - 1):
        hist_out_ref[t] = hist_ref[t + 1]
    hist_out_ref[POOL_HIST - 1] = u
    g0 = u0 + POOL_WIDTH
    sga_ref[...] = jax.nn.sigmoid(z[:, g0:g0 + D_MODEL]).astype(BF16)
    sgb_ref[...] = jax.nn.sigmoid(z[:, g0 + D_MODEL:g0 + 2 * D_MODEL]).astype(BF16)


def _sample_in(x, ln, win, wqkvT, qn_row, kn_row, qn_col, kn_col, hist_t, poolw, pscale, past_len):
    n = x.shape[0]
    sds = jax.ShapeDtypeStruct
    out_shape = (sds((n, ATTN_WIDTH), F32), sds((n, ATTN_WIDTH), F32), sds((3, N_HEADS, HEAD_DIM, n), F32),
                 sds((n, POOL_WIDTH), BF16), sds((n, D_MODEL), BF16), sds((n, D_MODEL), BF16),
                 sds(hist_t.shape, F32))
    return pl.pallas_call(
        functools.partial(_sample_in_kernel, past_len=past_len),
        out_shape=out_shape,
        compiler_params=pltpu.CompilerParams(vmem_limit_bytes=VMEM_LIMIT),
        name="sample_in",
    )(x, ln, win, wqkvT, qn_row, kn_row, qn_col, kn_col, hist_t, poolw, pscale)


def _sample_attn_kernel(pt_ref, sel_ref, qkvT_ref, kc_ref, vc_ref, o_ref,
                        kbuf, vbuf, sem, *, n_pages):
    b = pl.program_id(0)
    n_b = pl.num_programs(0)
    n_slots = MOBA_TOP_K * PAGES_PER_BLOCK

    def copies(seq_i, buf_i):
        out = []
        for h in range(N_HEADS):
            for r in range(MOBA_TOP_K):
                blk = sel_ref[(seq_i * N_HEADS + h) * MOBA_TOP_K + r]
                for pg in range(PAGES_PER_BLOCK):
                    page = pt_ref[seq_i * n_pages + blk * PAGES_PER_BLOCK + pg]
                    s_i = r * PAGES_PER_BLOCK + pg
                    out.append(pltpu.make_async_copy(kc_ref.at[page, h], kbuf.at[buf_i, h, s_i], sem.at[buf_i, 0]))
                    out.append(pltpu.make_async_copy(vc_ref.at[page, h], vbuf.at[buf_i, h, s_i], sem.at[buf_i, 1]))
        return out

    @pl.when(b == 0)
    def _():
        for c in copies(0, 0):
            c.start()

    cur = b % 2

    @pl.when(b + 1 < n_b)
    def _():
        for c in copies(b + 1, 1 - cur):
            c.start()

    for c in copies(b, cur):
        c.wait()

    q_col, kn_col, vn_col = (_seq_column(qkvT_ref[i], b) for i in range(3))
    seq_lane = lax.broadcasted_iota(jnp.int32, (1, o_ref.shape[2]), 1)

    @pl.when(b == 0)
    def _():
        o_ref[...] = jnp.zeros(o_ref.shape, F32)

    for h in range(N_HEADS):
        qh = q_col[h]
        kb = kbuf[cur, h]
        vb = vbuf[cur, h]
        s = jnp.sum(kb * qh[None], axis=1) * ATTN_SCALE
        s_new = jnp.sum(qh * kn_col[h], axis=0, keepdims=True) * ATTN_SCALE
        m = jnp.maximum(jnp.max(jnp.max(s, axis=1, keepdims=True), axis=0, keepdims=True), s_new)
        p = jnp.exp(s - m)
        p_new = jnp.exp(s_new - m)
        l = jnp.sum(jnp.sum(p, axis=1, keepdims=True), axis=0, keepdims=True) + p_new
        pv = vb[0] * p[0:1]
        for s_i in range(1, n_slots):
            pv = pv + vb[s_i] * p[s_i:s_i + 1]
        o = jnp.sum(pv, axis=1, keepdims=True) + p_new * vn_col[h]
        o_ref[h] = jnp.where(seq_lane == b, o / l, o_ref[h])


def _sample_attn(page_table_flat, sel_flat, qkvT, cache_kT, cache_vT, n_seq, n_pages):
    n_slots = MOBA_TOP_K * PAGES_PER_BLOCK
    buf = pltpu.VMEM((2, N_HEADS, n_slots, HEAD_DIM, PAGE_SIZE), F32)
    grid_spec = pltpu.PrefetchScalarGridSpec(
        num_scalar_prefetch=2,
        grid=(n_seq,),
        in_specs=[_const_spec(qkvT.shape), pl.BlockSpec(memory_space=pl.ANY), pl.BlockSpec(memory_space=pl.ANY)],
        out_specs=pl.BlockSpec((N_HEADS, HEAD_DIM, n_seq), lambda b, pt, sel: (0, 0, 0)),
        scratch_shapes=[buf, buf, pltpu.SemaphoreType.DMA((2, 2))],
    )
    return pl.pallas_call(
        functools.partial(_sample_attn_kernel, n_pages=n_pages),
        grid_spec=grid_spec,
        out_shape=jax.ShapeDtypeStruct((N_HEADS, HEAD_DIM, n_seq), F32),
        compiler_params=pltpu.CompilerParams(
            dimension_semantics=("arbitrary",), vmem_limit_bytes=VMEM_LIMIT),
        name="sample_attn",
    )(page_table_flat, sel_flat, qkvT, cache_kT, cache_vT)


def kernel(x_prompt, x_sample, cache_k, cache_v, state_pool, page_table, p_prompt, p_sample, ln_mix, w_in,
           q_norm, k_norm, pool_w, pool_scale, w_attn_out, w_pool_out, w_out, ln_mlp, w_up, w_down, ln_ple,
           w_ple_gate, w_ple_proj):
    depth = w_in.shape[0]
    assert depth == 1
    n_seq, n_pages = page_table.shape
    past_len = n_pages * PAGE_SIZE
    assert x_sample.shape[1] == 1 and past_len % MOBA_BLOCK == 0 and past_len // MOBA_BLOCK >= MOBA_TOP_K
    assert (past_len // MOBA_BLOCK) % _GATE_BUFS == 0 and (x_prompt.shape[1] // MOBA_BLOCK) % 16 == 0
    bsz, seq, _ = x_prompt.shape
    l = 0

    w_in_b = w_in[l].astype(BF16)
    wqkvT = w_in_b[:, :3 * ATTN_WIDTH].T
    wrest = w_in_b[:, 3 * ATTN_WIDTH:]
    poolw = pool_w[l].astype(BF16)
    tail_w = (w_attn_out[l].astype(BF16), w_pool_out[l].astype(BF16), w_out[l].astype(BF16), ln_mlp,
              w_up[l].astype(BF16), w_down[l].astype(BF16), ln_ple, w_ple_gate[l].astype(BF16),
              w_ple_proj[l].astype(BF16))

    xs = x_sample.reshape(n_seq, D_MODEL)
    hist_t = state_pool[l].transpose(1, 0, 2)
    qn_col, kn_col = q_norm.reshape(HEAD_DIM, 1), k_norm.reshape(HEAD_DIM, 1)
    k_s, v_s, qkvT_s, b_s, sga_s, sgb_s, hist_s = _sample_in(
        xs, ln_mix, w_in_b, wqkvT, q_norm, k_norm, qn_col, kn_col, hist_t, poolw, pool_scale, past_len)
    cache_kT = cache_k[l].transpose(0, 2, 3, 1)
    cache_vT = cache_v[l].transpose(0, 2, 3, 1)
    pt_flat = page_table.reshape(-1)

    (kT, vT, qa, ka, vTb, b_p, sga, sgb, hist_p) = _prompt_in(
        x_prompt, ln_mix, wqkvT, wrest, qn_col, kn_col, poolw, pool_scale)
    aT = _prompt_attn(qa, ka, vTb).reshape(bsz, ATTN_WIDTH, seq)
    y_prompt, sel = _tail(x_prompt, aT, b_p, sga, sgb, p_prompt[l], *tail_w, tm=512,
                          gate_inputs=(pt_flat, qkvT_s, cache_kT, n_pages))
    k_prompt = kT.transpose(0, 3, 1, 2)[None]
    v_prompt = vT.transpose(0, 3, 1, 2)[None]
    pool_prompt = hist_p[:, HIST_ROWS - POOL_HIST:][None]

    sel_flat = sel[:, :, :MOBA_TOP_K].reshape(-1)
    a_s = _sample_attn(pt_flat, sel_flat, qkvT_s, cache_kT, cache_vT, n_seq, n_pages)
    aT_s = a_s.reshape(1, ATTN_WIDTH, n_seq).astype(BF16)
    y_sample = _tail(xs[None], aT_s, b_s[None], sga_s[None], sgb_s[None], p_sample[l].reshape(1, n_seq, PLE_DIM),
                     *tail_w, tm=n_seq)
    y_sample = y_sample.reshape(n_seq, 1, D_MODEL)
    k_sample = k_s.reshape(1, n_seq, 1, N_HEADS, HEAD_DIM)
    v_sample = v_s.reshape(1, n_seq, 1, N_HEADS, HEAD_DIM)
    pool_sample = hist_s.transpose(1, 0, 2)[None]
    return (y_prompt, y_sample, k_prompt, v_prompt, pool_prompt, k_sample, v_sample, pool_sample)
```

```python
import functools

import jax
import jax.numpy as jnp
from jax import lax
from jax.experimental import pallas as pl
from jax.experimental.pallas import tpu as pltpu

D_MODEL = 1024
N_HEADS = 8
HEAD_DIM = 64
ATTN_WIDTH = N_HEADS * HEAD_DIM
MOBA_BLOCK = 256
MOBA_TOP_K = 3
PAGE_SIZE = 128
PAGES_PER_BLOCK = MOBA_BLOCK // PAGE_SIZE
ATTN_SCALE = HEAD_DIM ** -0.5
POOL_WINDOWS = (2, 4, 8, 16)
POOL_GROUP = 128
POOL_WIDTH = 512
POOL_HIST = 15
HIST_ROWS = 16
V_AUG = HEAD_DIM + 16
D_FF = 4 * D_MODEL
PLE_DIM = 256
EPS = 1e-6
NEG_INF = -1e30
LOWEST = -3e38
LOG2E = 1.4426950408889634

LANES = 128
VMEM_LIMIT = 60 * 1024 * 1024

F32 = jnp.float32
BF16 = jnp.bfloat16

_NT = (((1,), (1,)), ((), ()))
_TN = (((0,), (0,)), ((), ()))


def _rms_rows(x, g):
    ms = jnp.mean(x * x, axis=-1, keepdims=True)
    return x * lax.rsqrt(ms + EPS) * g


def _sigmoid(x):
    return 0.5 * jnp.tanh(0.5 * x) + 0.5


def _split_bf16(a):
    hi = a.astype(BF16)
    lo = (a - hi.astype(F32)).astype(BF16)
    return hi, lo


def _const_spec(shape):
    zeros = (0,) * len(shape)
    return pl.BlockSpec(shape, lambda *_: zeros, pipeline_mode=pl.Buffered(1))


def _pool_means(e_ref, tm, pos0):
    pos = pos0 + lax.broadcasted_iota(jnp.int32, (tm, 1), 0)
    out = []
    for g, w in enumerate(POOL_WINDOWS):
        c0 = g * POOL_GROUP
        cur = e_ref[HIST_ROWS:HIST_ROWS + tm, c0:c0 + POOL_GROUP]
        acc = cur
        for k in range(1, w):
            acc = acc + e_ref[HIST_ROWS - k:HIST_ROWS - k + tm, c0:c0 + POOL_GROUP]
        inv_cnt = 1.0 / jnp.minimum(w, pos + 1).astype(F32)
        out.append(acc * inv_cnt - cur)
    return out


def _prompt_in_kernel(x_ref, ln_ref, wqkvT_ref, wrest_ref, qn_ref, kn_ref, poolw_ref, pscale_ref,
                      kT_ref, vT_ref, qa_ref, ka_ref, vTb_ref, b_ref, sga_ref, sgb_ref,
                      hist_ref, e_scr, km_scr, *, tm, n_blocks):
    i = pl.program_id(1)
    n_i = pl.num_programs(1)
    blocks_per_tile = tm // MOBA_BLOCK

    @pl.when(i == 0)
    def _():
        e_scr[0:HIST_ROWS, :] = jnp.zeros((HIST_ROWS, POOL_WIDTH), F32)
        km_scr[...] = jnp.zeros(km_scr.shape, F32)

    xn = _rms_rows(x_ref[...], ln_ref[...]).astype(BF16)

    def proj_t(r0):
        t = lax.dot_general(wqkvT_ref[r0:r0 + ATTN_WIDTH, :], xn, _NT, preferred_element_type=F32)
        return t.reshape(N_HEADS, HEAD_DIM, tm)

    def head_norm(t, g_col):
        ms = jnp.mean(t * t, axis=1, keepdims=True)
        return t * lax.rsqrt(ms + EPS) * g_col[None]

    q = head_norm(proj_t(0), qn_ref[...])
    k = head_norm(proj_t(ATTN_WIDTH), kn_ref[...])
    v = proj_t(2 * ATTN_WIDTH)
    kT_ref[...] = k
    vT_ref[...] = v
    vTb_ref[:, 0:HEAD_DIM, :] = v.astype(BF16)
    ones_row = lax.broadcasted_iota(jnp.int32, (N_HEADS, V_AUG - HEAD_DIM, tm), 1) == 0
    vTb_ref[:, HEAD_DIM:V_AUG, :] = jnp.where(ones_row, 1.0, 0.0).astype(BF16)
    qa_ref[:, 0:HEAD_DIM, :] = (q * (ATTN_SCALE * LOG2E)).astype(BF16)
    key_blk = (i * tm + lax.broadcasted_iota(jnp.int32, (tm, 1), 0)) // MOBA_BLOCK
    onehot = jnp.where(key_blk == lax.broadcasted_iota(jnp.int32, (1, n_blocks), 1), 1.0, 0.0).astype(BF16)
    for h in range(N_HEADS):
        ka_ref[h, :, 0:HEAD_DIM] = k[h].T.astype(BF16)
        ka_ref[h, :, HEAD_DIM:HEAD_DIM + n_blocks] = onehot

    lane = lax.broadcasted_iota(jnp.int32, (1, 1, LANES), 2)
    km = km_scr[...]
    for h in range(blocks_per_tile):
        ks = jnp.sum(k[:, :, h * MOBA_BLOCK:(h + 1) * MOBA_BLOCK], axis=2, keepdims=True) * (1.0 / MOBA_BLOCK)
        km = jnp.where(lane == i * blocks_per_tile + h, ks, km)
    km_scr[...] = km

    pos = i * tm + lax.broadcasted_iota(jnp.int32, (1, tm), 1)
    own = pos // MOBA_BLOCK
    blk = lax.broadcasted_iota(jnp.int32, (n_blocks, 1), 0)
    for h in range(N_HEADS):
        km_hi, km_lo = _split_bf16(km[h][:, 0:n_blocks])
        q_hi, q_lo = _split_bf16(q[h])
        gate = (lax.dot_general(km_hi, q_hi, _TN, preferred_element_type=F32)
                + lax.dot_general(km_hi, q_lo, _TN, preferred_element_type=F32)
                + lax.dot_general(km_lo, q_hi, _TN, preferred_element_type=F32))
        g = jnp.where(blk < own, gate, LOWEST)
        bias = jnp.where(blk == own, 0.0, NEG_INF)
        for _ in range(MOBA_TOP_K):
            m = jnp.max(g, axis=0, keepdims=True)
            idx = jnp.min(jnp.where(g == m, blk, n_blocks), axis=0, keepdims=True)
            idx = jnp.where(m > LOWEST, idx, n_blocks)
            pick = blk == idx
            bias = jnp.where(pick, 0.0, bias)
            g = jnp.where(pick, LOWEST, g)
        qa_ref[h, HEAD_DIM:HEAD_DIM + n_blocks, :] = bias.astype(BF16)

    u = jnp.dot(xn, wrest_ref[:, 0:POOL_WIDTH], preferred_element_type=F32)
    e_scr[HIST_ROWS:HIST_ROWS + tm, :] = u
    d = _pool_means(e_scr, tm, i * tm)
    for g_i in range(len(POOL_WINDOWS)):
        c0 = g_i * POOL_GROUP
        y = jnp.dot(d[g_i].astype(BF16), poolw_ref[g_i], preferred_element_type=F32)
        b_ref[:, c0:c0 + POOL_GROUP] = (y * pscale_ref[:, c0:c0 + POOL_GROUP]).astype(BF16)
    tail = e_scr[tm:tm + HIST_ROWS, :]
    e_scr[0:HIST_ROWS, :] = tail

    @pl.when(i == n_i - 1)
    def _():
        hist_ref[...] = tail

    ga = jnp.dot(xn, wrest_ref[:, POOL_WIDTH:POOL_WIDTH + D_MODEL], preferred_element_type=F32)
    sga_ref[...] = _sigmoid(ga).astype(BF16)
    gb = jnp.dot(xn, wrest_ref[:, POOL_WIDTH + D_MODEL:POOL_WIDTH + 2 * D_MODEL], preferred_element_type=F32)
    sgb_ref[...] = _sigmoid(gb).astype(BF16)


def _prompt_in(x, ln, wqkvT, wrest, qn_col, kn_col, poolw, pscale, tm=512):
    bsz, seq, _ = x.shape
    n_blocks = seq // MOBA_BLOCK
    grid = (bsz, seq // tm)
    hT = lambda dt: jax.ShapeDtypeStruct((bsz, N_HEADS, HEAD_DIM, seq), dt)
    hT_spec = pl.BlockSpec((None, N_HEADS, HEAD_DIM, tm), lambda b, i: (b, 0, 0, i))
    row_spec = lambda w: pl.BlockSpec((None, tm, w), lambda b, i: (b, i, 0))
    aug = HEAD_DIM + n_blocks
    out_shape = (
        hT(F32), hT(F32),
        jax.ShapeDtypeStruct((bsz, N_HEADS, aug, seq), BF16),
        jax.ShapeDtypeStruct((bsz, N_HEADS, seq, aug), BF16),
        jax.ShapeDtypeStruct((bsz, N_HEADS, V_AUG, seq), BF16),
        jax.ShapeDtypeStruct((bsz, seq, POOL_WIDTH), BF16),
        jax.ShapeDtypeStruct((bsz, seq, D_MODEL), BF16),
        jax.ShapeDtypeStruct((bsz, seq, D_MODEL), BF16),
        jax.ShapeDtypeStruct((bsz, HIST_ROWS, POOL_WIDTH), F32),
    )
    out_specs = (
        hT_spec, hT_spec,
        pl.BlockSpec((None, N_HEADS, aug, tm), lambda b, i: (b, 0, 0, i)),
        pl.BlockSpec((None, N_HEADS, tm, aug), lambda b, i: (b, 0, i, 0)),
        pl.BlockSpec((None, N_HEADS, V_AUG, tm), lambda b, i: (b, 0, 0, i)),
        row_spec(POOL_WIDTH), row_spec(D_MODEL), row_spec(D_MODEL),
        pl.BlockSpec((None, HIST_ROWS, POOL_WIDTH), lambda b, i: (b, 0, 0)),
    )
    in_specs = [
        row_spec(D_MODEL),
        _const_spec(ln.shape), _const_spec(wqkvT.shape), _const_spec(wrest.shape),
        _const_spec(qn_col.shape), _const_spec(kn_col.shape),
        _const_spec(poolw.shape), _const_spec(pscale.shape),
    ]
    return pl.pallas_call(
        functools.partial(_prompt_in_kernel, tm=tm, n_blocks=n_blocks),
        grid=grid, in_specs=in_specs, out_specs=out_specs, out_shape=out_shape,
        scratch_shapes=[pltpu.VMEM((HIST_ROWS + tm, POOL_WIDTH), F32),
                        pltpu.VMEM((N_HEADS, HEAD_DIM, LANES), F32)],
        compiler_params=pltpu.CompilerParams(
            dimension_semantics=("arbitrary", "arbitrary"), vmem_limit_bytes=VMEM_LIMIT),
        name="prompt_in",
    )(x, ln, wqkvT, wrest, qn_col, kn_col, poolw, pscale)


_ATTN_UNROLL_PAIRS = 4


def _prompt_attn_kernel(qa_ref, ka_ref, vT_ref, o_ref, m_scr, acc_scr, s_scr, cm_scr, *, heads):
    t = pl.program_id(2)
    blk = MOBA_BLOCK
    tri = (lax.broadcasted_iota(jnp.int32, (blk, 1), 0) <= lax.broadcasted_iota(jnp.int32, (1, blk), 1))
    m_scr[...] = jnp.full(m_scr.shape, NEG_INF, F32)
    acc_scr[...] = jnp.zeros(acc_scr.shape, F32)

    def scores(j, slot, mask, gs=tuple(range(heads))):
        r0 = j * blk if isinstance(j, int) else pl.multiple_of(j * blk, blk)
        for g in gs:
            s = jnp.dot(ka_ref[g, pl.ds(r0, blk), :], qa_ref[g], preferred_element_type=F32)
            if mask is not None:
                s = jnp.where(mask, s, NEG_INF)
            s_scr[slot, g] = s
            cm_scr[slot, g] = jnp.max(s, axis=0, keepdims=True)

    def softmax_pv(j, slot, gs=tuple(range(heads))):
        r0 = j * blk if isinstance(j, int) else pl.multiple_of(j * blk, blk)
        for g in gs:
            m = m_scr[g]
            m_new = jnp.maximum(m, cm_scr[slot, g])
            alpha = jnp.exp2(m - m_new)
            p = jnp.exp2((s_scr[slot, g] - m_new).astype(BF16))
            m_scr[g] = m_new
            pv = jnp.dot(vT_ref[g, :, pl.ds(r0, blk)], p, preferred_element_type=F32)
            acc_scr[g] = alpha * acc_scr[g] + pv

    def step(j, cur, diag_next):
        for g in range(heads):
            scores(j + 1, 1 - cur, tri if diag_next else None, (g,))
            softmax_pv(j, cur, (g,))

    odd_lead = (t + 1) % 2
    scores(0, odd_lead, tri | (t > 0))

    @pl.when((t >= 1) & (odd_lead == 1))
    def _():
        step(0, 1, False)

    def pair(j):
        step(j, 0, False)
        step(j + 1, 1, False)

    n_pairs = jnp.maximum(t - 1, 0) // 2
    lead_pairs = n_pairs % _ATTN_UNROLL_PAIRS

    def one_pair(i, c):
        pair(odd_lead + 2 * i)
        return c

    lax.fori_loop(0, lead_pairs, one_pair, 0)

    def many_pairs(i, c):
        j = odd_lead + 2 * lead_pairs + 2 * _ATTN_UNROLL_PAIRS * i
        for u in range(_ATTN_UNROLL_PAIRS):
            pair(j + 2 * u)
        return c

    lax.fori_loop(0, n_pairs // _ATTN_UNROLL_PAIRS, many_pairs, 0)

    @pl.when(t >= 1)
    def _():
        step(t - 1, 0, True)

    softmax_pv(t, 1)
    for g in range(heads):
        o_ref[g] = (acc_scr[g, 0:HEAD_DIM] / acc_scr[g, HEAD_DIM:HEAD_DIM + 1]).astype(o_ref.dtype)


def _prompt_attn(qa, ka, vT, heads=N_HEADS):
    bsz, _, aug, seq = qa.shape
    blk = MOBA_BLOCK
    grid = (bsz, N_HEADS // heads, seq // blk)
    resident = lambda r, c: pl.BlockSpec((None, heads, r, c), lambda b, h, t: (b, h, 0, 0),
                                         pipeline_mode=pl.Buffered(1))
    return pl.pallas_call(
        functools.partial(_prompt_attn_kernel, heads=heads),
        grid=grid,
        in_specs=[pl.BlockSpec((None, heads, aug, blk), lambda b, h, t: (b, h, 0, t)),
                  resident(seq, aug), resident(V_AUG, seq)],
        out_specs=pl.BlockSpec((None, heads, HEAD_DIM, blk), lambda b, h, t: (b, h, 0, t)),
        out_shape=jax.ShapeDtypeStruct((bsz, N_HEADS, HEAD_DIM, seq), BF16),
        scratch_shapes=[pltpu.VMEM((heads, 1, blk), F32),
                        pltpu.VMEM((heads, V_AUG, blk), F32), pltpu.VMEM((2, heads, blk, blk), F32),
                        pltpu.VMEM((2, heads, 1, blk), F32)],
        compiler_params=pltpu.CompilerParams(
            dimension_semantics=("arbitrary", "arbitrary", "arbitrary"), vmem_limit_bytes=VMEM_LIMIT),
        name="prompt_attn",
    )(qa, ka, vT)


_GATE_BUFS = 16
_GATE_GROUP = 8


def _seq_column(t, seq_i):
    lane = lax.broadcasted_iota(jnp.int32, (1, 1, t.shape[2]), 2)
    return jnp.sum(jnp.where(lane == seq_i, t, 0.0), axis=2, keepdims=True)


def _top_k_blocks(s_scr, sel_ref, n_blocks):
    lane = lax.broadcasted_iota(jnp.int32, (1, LANES), 1)
    gate = jnp.zeros((N_HEADS, LANES), F32)
    for j in range(n_blocks):
        ssum = jnp.sum(s_scr[j], axis=1, keepdims=True) * (1.0 / MOBA_BLOCK)
        gate = jnp.where(lane == j, ssum, gate)
    g = jnp.where(lane < n_blocks, gate, LOWEST)
    sel = jnp.zeros((N_HEADS, LANES), jnp.int32)
    for r in range(MOBA_TOP_K):
        m = jnp.max(g, axis=1, keepdims=True)
        idx = jnp.min(jnp.where(g == m, lane, LANES), axis=1, keepdims=True)
        sel = jnp.where(lane == r, idx, sel)
        g = jnp.where(lane == idx, LOWEST, g)
    sel_ref[...] = sel


def _tail_kernel(*refs, ff_chunk, n_pages):
    if n_pages is None:
        (x_ref, aT_ref, b_ref, sga_ref, sgb_ref, p_ref, wao_ref, wpo_ref, wo_ref, lnm_ref,
         wup_ref, wdn_ref, lnp_ref, wpg_ref, wpp_ref, y_ref) = refs
    else:
        (pt_ref, x_ref, aT_ref, b_ref, sga_ref, sgb_ref, p_ref, wao_ref, wpo_ref, wo_ref, lnm_ref,
         wup_ref, wdn_ref, lnp_ref, wpg_ref, wpp_ref, q_ref, kc_ref, y_ref, sel_ref,
         kbuf, sem, s_scr, qb_scr) = refs
        seq_i = pl.program_id(0) * pl.num_programs(1) + pl.program_id(1)
        n_seq = pl.num_programs(0) * pl.num_programs(1)
        n_blocks = n_pages // PAGES_PER_BLOCK

        def page_copy(s_i, blk_i, slot, pg):
            page = pt_ref[s_i * n_pages + blk_i * PAGES_PER_BLOCK + pg]
            return pltpu.make_async_copy(kc_ref.at[page], kbuf.at[slot, pg], sem.at[slot, pg])

        def start(s_i, blk_i, slot):
            for pg in range(PAGES_PER_BLOCK):
                page_copy(s_i, blk_i, slot, pg).start()

        def wait_group(k):
            for blk_i in range(k * _GATE_GROUP, (k + 1) * _GATE_GROUP):
                for pg in range(PAGES_PER_BLOCK):
                    page_copy(seq_i, blk_i, blk_i % _GATE_BUFS, pg).wait()

        def score_group(k):
            for h in range(N_HEADS):
                qh = qb_scr[h]
                for blk_i in range(k * _GATE_GROUP, (k + 1) * _GATE_GROUP):
                    slot = blk_i % _GATE_BUFS
                    ksum = kbuf[slot, 0, h]
                    for pg in range(1, PAGES_PER_BLOCK):
                        ksum = ksum + kbuf[slot, pg, h]
                    s_scr[blk_i, h:h + 1, :] = jnp.sum(ksum * qh, axis=0, keepdims=True)

        def refill_group(k):
            blks = range(k * _GATE_GROUP, (k + 1) * _GATE_GROUP)
            for blk_i in blks:
                if blk_i + _GATE_BUFS < n_blocks:
                    start(seq_i, blk_i + _GATE_BUFS, blk_i % _GATE_BUFS)
            wrap = [blk_i for blk_i in blks if blk_i + _GATE_BUFS >= n_blocks]
            if wrap:
                @pl.when(seq_i + 1 < n_seq)
                def _():
                    for blk_i in wrap:
                        start(seq_i + 1, blk_i + _GATE_BUFS - n_blocks, blk_i % _GATE_BUFS)

        @pl.when(seq_i == 0)
        def _():
            for j in range(_GATE_BUFS):
                start(0, j, j)

        qb_scr[...] = jnp.broadcast_to(_seq_column(q_ref[0], seq_i), qb_scr.shape)

    a_proj = lax.dot_general(aT_ref[...], wao_ref[...], _TN, preferred_element_type=F32)
    b_proj = jnp.dot(b_ref[...], wpo_ref[...], preferred_element_type=F32)
    m = sga_ref[...].astype(F32) * a_proj + sgb_ref[...].astype(F32) * b_proj
    x1 = x_ref[...] + jnp.dot(m.astype(BF16), wo_ref[...], preferred_element_type=F32)
    xn = _rms_rows(x1, lnm_ref[...]).astype(BF16)
    acc = x1
    for c in range(D_FF // ff_chunk):
        if n_pages is not None:
            wait_group(c)
            score_group(c)
            refill_group(c)
        hid = jnp.maximum(jnp.dot(xn, wup_ref[:, c * ff_chunk:(c + 1) * ff_chunk],
                                  preferred_element_type=F32), 0.0)
        acc = acc + jnp.dot((hid * hid).astype(BF16), wdn_ref[c * ff_chunk:(c + 1) * ff_chunk, :],
                            preferred_element_type=F32)
    x2 = acc
    if n_pages is not None:
        _top_k_blocks(s_scr, sel_ref, n_blocks)
    xg = _rms_rows(x2, lnp_ref[...]).astype(BF16)
    gate = _sigmoid(jnp.dot(xg, wpg_ref[...], preferred_element_type=F32))
    emb = jnp.dot(p_ref[...].astype(BF16), wpp_ref[...], preferred_element_type=F32)
    y_ref[...] = x2 + gate * emb


def _tail(x, aT, b, sga, sgb, p, wao, wpo, wo, lnm, wup, wdn, lnp, wpg, wpp, tm, gate_inputs=None):
    bsz, seq, _ = x.shape
    grid = (bsz, seq // tm)
    n_i = seq // tm
    row_spec = lambda w: pl.BlockSpec((None, tm, w), lambda b_, i, *_: (b_, i, 0))
    in_specs = [
        row_spec(D_MODEL),
        pl.BlockSpec((None, ATTN_WIDTH, tm), lambda b_, i, *_: (b_, 0, i)),
        row_spec(POOL_WIDTH), row_spec(D_MODEL), row_spec(D_MODEL), row_spec(PLE_DIM),
    ] + [_const_spec(w.shape) for w in (wao, wpo, wo, lnm, wup, wdn, lnp, wpg, wpp)]
    y_shape = jax.ShapeDtypeStruct((bsz, seq, D_MODEL), F32)
    params = pltpu.CompilerParams(dimension_semantics=("arbitrary", "arbitrary"), vmem_limit_bytes=VMEM_LIMIT)
    args = (x, aT, b, sga, sgb, p, wao, wpo, wo, lnm, wup, wdn, lnp, wpg, wpp)
    if gate_inputs is None:
        return pl.pallas_call(
            functools.partial(_tail_kernel, ff_chunk=D_MODEL, n_pages=None),
            grid=grid, in_specs=in_specs, out_specs=row_spec(D_MODEL), out_shape=y_shape,
            compiler_params=params, name="tail",
        )(*args)
    pt_flat, qkvT, cache_kT, n_pages = gate_inputs
    n_seq = qkvT.shape[3]
    n_blocks = n_pages // PAGES_PER_BLOCK
    n_groups = n_blocks // _GATE_GROUP
    assert bsz * n_i == n_seq and n_blocks % _GATE_BUFS == 0 and _GATE_BUFS % _GATE_GROUP == 0
    assert D_FF % n_groups == 0
    seq_spec = lambda shape: pl.BlockSpec((None,) + shape, lambda b_, i, *_: (b_ * n_i + i,) + (0,) * len(shape))
    grid_spec = pltpu.PrefetchScalarGridSpec(
        num_scalar_prefetch=1, grid=grid,
        in_specs=in_specs + [_const_spec(qkvT.shape), pl.BlockSpec(memory_space=pl.ANY)],
        out_specs=(row_spec(D_MODEL), seq_spec((N_HEADS, LANES))),
        scratch_shapes=[pltpu.VMEM((_GATE_BUFS, PAGES_PER_BLOCK, N_HEADS, HEAD_DIM, PAGE_SIZE), F32),
                        pltpu.SemaphoreType.DMA((_GATE_BUFS, PAGES_PER_BLOCK)),
                        pltpu.VMEM((n_blocks, N_HEADS, LANES), F32),
                        pltpu.VMEM((N_HEADS, HEAD_DIM, LANES), F32)],
    )
    return pl.pallas_call(
        functools.partial(_tail_kernel, ff_chunk=D_FF // n_groups, n_pages=n_pages),
        grid_spec=grid_spec,
        out_shape=(y_shape, jax.ShapeDtypeStruct((n_seq, N_HEADS, LANES), jnp.int32)),
        compiler_params=params, name="tail_gate",
    )(pt_flat, *args, qkvT, cache_kT)


def _sample_in_kernel(x_ref, ln_ref, wqkvT_ref, wrest_ref, kn_ref, qn_col_ref, kn_col_ref, hist_ref,
                      poolw_ref, pscale_ref,
                      k_ref, v_ref, qkvT_ref, b_ref, sga_ref, sgb_ref, hist_out_ref, *, past_len):
    n = x_ref.shape[0]
    xn = _rms_rows(x_ref[...], ln_ref[...]).astype(BF16)
    k_row = lax.dot_general(xn, wqkvT_ref[ATTN_WIDTH:2 * ATTN_WIDTH, :], _NT, preferred_element_type=F32)
    for h in range(N_HEADS):
        c0 = h * HEAD_DIM
        k_ref[:, c0:c0 + HEAD_DIM] = _rms_rows(k_row[:, c0:c0 + HEAD_DIM], kn_ref[...])
    v_ref[...] = lax.dot_general(xn, wqkvT_ref[2 * ATTN_WIDTH:3 * ATTN_WIDTH, :], _NT, preferred_element_type=F32)
    z = jnp.dot(xn, wrest_ref[...], preferred_element_type=F32)
    zT = lax.dot_general(wqkvT_ref[...], xn, _NT, preferred_element_type=F32).reshape(3, N_HEADS, HEAD_DIM, n)

    def head_norm(t, g_col):
        ms = jnp.mean(t * t, axis=1, keepdims=True)
        return t * lax.rsqrt(ms + EPS) * g_col[None]

    qkvT_ref[0] = head_norm(zT[0], qn_col_ref[...])
    qkvT_ref[1] = head_norm(zT[1], kn_col_ref[...])
    qkvT_ref[2] = zT[2]
    u = z[:, 0:POOL_WIDTH]
    for g_i, w in enumerate(POOL_WINDOWS):
        c0 = g_i * POOL_GROUP
        cur = u[:, c0:c0 + POOL_GROUP]
        acc = cur
        for k in range(1, w):
            acc = acc + hist_ref[POOL_HIST - k, :, c0:c0 + POOL_GROUP]
        cnt = float(min(w, past_len + 1))
        d = acc / cnt - cur
        y = jnp.dot(d.astype(BF16), poolw_ref[g_i], preferred_element_type=F32)
        b_ref[:, c0:c0 + POOL_GROUP] = (y * pscale_ref[:, c0:c0 + POOL_GROUP]).astype(BF16)
    for t in range(POOL_HIST - 1):
        hist_out_ref[t] = hist_ref[t + 1]
    hist_out_ref[POOL_HIST - 1] = u
    g0 = POOL_WIDTH
    sga_ref[...] = _sigmoid(z[:, g0:g0 + D_MODEL]).astype(BF16)
    sgb_ref[...] = _sigmoid(z[:, g0 + D_MODEL:g0 + 2 * D_MODEL]).astype(BF16)


def _sample_in(x, ln, wqkvT, wrest, kn_row, qn_col, kn_col, hist_t, poolw, pscale, past_len):
    n = x.shape[0]
    sds = jax.ShapeDtypeStruct
    out_shape = (sds((n, ATTN_WIDTH), F32), sds((n, ATTN_WIDTH), F32), sds((3, N_HEADS, HEAD_DIM, n), F32),
                 sds((n, POOL_WIDTH), BF16), sds((n, D_MODEL), BF16), sds((n, D_MODEL), BF16),
                 sds(hist_t.shape, F32))
    return pl.pallas_call(
        functools.partial(_sample_in_kernel, past_len=past_len),
        out_shape=out_shape,
        compiler_params=pltpu.CompilerParams(vmem_limit_bytes=VMEM_LIMIT),
        name="sample_in",
    )(x, ln, wqkvT, wrest, kn_row, qn_col, kn_col, hist_t, poolw, pscale)


def _sample_attn_kernel(pt_ref, sel_ref, qkvT_ref, kc_ref, vc_ref, o_ref,
                        kbuf, vbuf, sem, *, n_pages):
    b = pl.program_id(0)
    n_b = pl.num_programs(0)
    n_slots = MOBA_TOP_K * PAGES_PER_BLOCK

    def copies(seq_i, buf_i):
        out = []
        for h in range(N_HEADS):
            for r in range(MOBA_TOP_K):
                blk = sel_ref[(seq_i * N_HEADS + h) * MOBA_TOP_K + r]
                for pg in range(PAGES_PER_BLOCK):
                    page = pt_ref[seq_i * n_pages + blk * PAGES_PER_BLOCK + pg]
                    s_i = r * PAGES_PER_BLOCK + pg
                    out.append(pltpu.make_async_copy(kc_ref.at[page, h], kbuf.at[buf_i, h, s_i], sem.at[buf_i, 0]))
                    out.append(pltpu.make_async_copy(vc_ref.at[page, h], vbuf.at[buf_i, h, s_i], sem.at[buf_i, 1]))
        return out

    @pl.when(b == 0)
    def _():
        for c in copies(0, 0):
            c.start()

    cur = b % 2

    @pl.when(b + 1 < n_b)
    def _():
        for c in copies(b + 1, 1 - cur):
            c.start()

    for c in copies(b, cur):
        c.wait()

    q_col, kn_col, vn_col = (_seq_column(qkvT_ref[i], b) for i in range(3))
    seq_lane = lax.broadcasted_iota(jnp.int32, (1, o_ref.shape[2]), 1)

    @pl.when(b == 0)
    def _():
        o_ref[...] = jnp.zeros(o_ref.shape, F32)

    for h in range(N_HEADS):
        qh = q_col[h]
        kb = kbuf[cur, h]
        vb = vbuf[cur, h]
        s = jnp.sum(kb * qh[None], axis=1) * ATTN_SCALE
        s_new = jnp.sum(qh * kn_col[h], axis=0, keepdims=True) * ATTN_SCALE
        m = jnp.maximum(jnp.max(jnp.max(s, axis=1, keepdims=True), axis=0, keepdims=True), s_new)
        p = jnp.exp(s - m)
        p_new = jnp.exp(s_new - m)
        l = jnp.sum(jnp.sum(p, axis=1, keepdims=True), axis=0, keepdims=True) + p_new
        pv = vb[0] * p[0:1]
        for s_i in range(1, n_slots):
            pv = pv + vb[s_i] * p[s_i:s_i + 1]
        o = jnp.sum(pv, axis=1, keepdims=True) + p_new * vn_col[h]
        o_ref[h] = jnp.where(seq_lane == b, o / l, o_ref[h])


def _sample_attn(page_table_flat, sel_flat, qkvT, cache_kT, cache_vT, n_seq, n_pages):
    n_slots = MOBA_TOP_K * PAGES_PER_BLOCK
    buf = pltpu.VMEM((2, N_HEADS, n_slots, HEAD_DIM, PAGE_SIZE), F32)
    grid_spec = pltpu.PrefetchScalarGridSpec(
        num_scalar_prefetch=2,
        grid=(n_seq,),
        in_specs=[_const_spec(qkvT.shape), pl.BlockSpec(memory_space=pl.ANY), pl.BlockSpec(memory_space=pl.ANY)],
        out_specs=pl.BlockSpec((N_HEADS, HEAD_DIM, n_seq), lambda b, pt, sel: (0, 0, 0)),
        scratch_shapes=[buf, buf, pltpu.SemaphoreType.DMA((2, 2))],
    )
    return pl.pallas_call(
        functools.partial(_sample_attn_kernel, n_pages=n_pages),
        grid_spec=grid_spec,
        out_shape=jax.ShapeDtypeStruct((N_HEADS, HEAD_DIM, n_seq), F32),
        compiler_params=pltpu.CompilerParams(
            dimension_semantics=("arbitrary",), vmem_limit_bytes=VMEM_LIMIT),
        name="sample_attn",
    )(page_table_flat, sel_flat, qkvT, cache_kT, cache_vT)


def kernel(x_prompt, x_sample, cache_k, cache_v, state_pool, page_table, p_prompt, p_sample, ln_mix, w_in,
           q_norm, k_norm, pool_w, pool_scale, w_attn_out, w_pool_out, w_out, ln_mlp, w_up, w_down, ln_ple,
           w_ple_gate, w_ple_proj):
    depth = w_in.shape[0]
    assert depth == 1
    n_seq, n_pages = page_table.shape
    past_len = n_pages * PAGE_SIZE
    assert x_sample.shape[1] == 1 and past_len % MOBA_BLOCK == 0 and past_len // MOBA_BLOCK >= MOBA_TOP_K
    assert (past_len // MOBA_BLOCK) % _GATE_BUFS == 0 and (x_prompt.shape[1] // MOBA_BLOCK) % 16 == 0
    bsz, seq, _ = x_prompt.shape
    l = 0

    wqkvT = w_in[l][:, :3 * ATTN_WIDTH].T.astype(BF16)
    wrest = w_in[l][:, 3 * ATTN_WIDTH:].astype(BF16)
    poolw = pool_w[l].astype(BF16)
    tail_w = (w_attn_out[l].astype(BF16), w_pool_out[l].astype(BF16), w_out[l].astype(BF16), ln_mlp,
              w_up[l].astype(BF16), w_down[l].astype(BF16), ln_ple, w_ple_gate[l].astype(BF16),
              w_ple_proj[l].astype(BF16))

    xs = x_sample.reshape(n_seq, D_MODEL)
    hist_t = state_pool[l].transpose(1, 0, 2)
    qn_col, kn_col = q_norm.reshape(HEAD_DIM, 1), k_norm.reshape(HEAD_DIM, 1)
    k_s, v_s, qkvT_s, b_s, sga_s, sgb_s, hist_s = _sample_in(
        xs, ln_mix, wqkvT, wrest, k_norm, qn_col, kn_col, hist_t, poolw, pool_scale, past_len)
    cache_kT = cache_k[l].transpose(0, 2, 3, 1)
    cache_vT = cache_v[l].transpose(0, 2, 3, 1)
    pt_flat = page_table.reshape(-1)

    (kT, vT, qa, ka, vTb, b_p, sga, sgb, hist_p) = _prompt_in(
        x_prompt, ln_mix, wqkvT, wrest, qn_col, kn_col, poolw, pool_scale)
    aT = _prompt_attn(qa, ka, vTb).reshape(bsz, ATTN_WIDTH, seq)
    y_prompt, sel = _tail(x_prompt, aT, b_p, sga, sgb, p_prompt[l], *tail_w, tm=512,
                          gate_inputs=(pt_flat, qkvT_s, cache_kT, n_pages))
    k_prompt = kT.transpose(0, 3, 1, 2)[None]
    v_prompt = vT.transpose(0, 3, 1, 2)[None]
    pool_prompt = hist_p[:, HIST_ROWS - POOL_HIST:][None]

    sel_flat = sel[:, :, :MOBA_TOP_K].reshape(-1)
    a_s = _sample_attn(pt_flat, sel_flat, qkvT_s, cache_kT, cache_vT, n_seq, n_pages)
    aT_s = a_s.reshape(1, ATTN_WIDTH, n_seq).astype(BF16)
    y_sample = _tail(xs[None], aT_s, b_s[None], sga_s[None], sgb_s[None], p_sample[l].reshape(1, n_seq, PLE_DIM),
                     *tail_w, tm=n_seq)
    y_sample = y_sample.reshape(n_seq, 1, D_MODEL)
    k_sample = k_s.reshape(1, n_seq, 1, N_HEADS, HEAD_DIM)
    v_sample = v_s.reshape(1, n_seq, 1, N_HEADS, HEAD_DIM)
    pool_sample = hist_s.transpose(1, 0, 2)[None]
    return (y_prompt, y_sample, k_prompt, v_prompt, pool_prompt, k_sample, v_sample, pool_sample)
```

```python
import functools

import jax
import jax.numpy as jnp
from jax import lax
from jax.experimental import pallas as pl
from jax.experimental.pallas import tpu as pltpu

D_MODEL = 1024
N_HEADS = 8
HEAD_DIM = 64
ATTN_WIDTH = N_HEADS * HEAD_DIM
MOBA_BLOCK = 256
MOBA_TOP_K = 3
PAGE_SIZE = 128
PAGES_PER_BLOCK = MOBA_BLOCK // PAGE_SIZE
ATTN_SCALE = HEAD_DIM ** -0.5
POOL_WINDOWS = (2, 4, 8, 16)
POOL_GROUP = 128
POOL_WIDTH = 512
POOL_HIST = 15
HIST_ROWS = 16
V_AUG = HEAD_DIM + 16
D_FF = 4 * D_MODEL
PLE_DIM = 256
EPS = 1e-6
NEG_INF = -1e30
LOWEST = -3e38
LOG2E = 1.4426950408889634

LANES = 128
VMEM_LIMIT = 60 * 1024 * 1024

F32 = jnp.float32
BF16 = jnp.bfloat16

_NT = (((1,), (1,)), ((), ()))
_TN = (((0,), (0,)), ((), ()))


def _rms_rows(x, g):
    ms = jnp.mean(x * x, axis=-1, keepdims=True)
    return x * lax.rsqrt(ms + EPS) * g


def _sigmoid(x):
    return 0.5 * jnp.tanh(0.5 * x) + 0.5


def _split_bf16(a):
    hi = a.astype(BF16)
    lo = (a - hi.astype(F32)).astype(BF16)
    return hi, lo


def _const_spec(shape):
    zeros = (0,) * len(shape)
    return pl.BlockSpec(shape, lambda *_: zeros, pipeline_mode=pl.Buffered(1))


def _pool_means(e_ref, tm, pos0):
    pos = pos0 + lax.broadcasted_iota(jnp.int32, (tm, 1), 0)
    out = []
    for g, w in enumerate(POOL_WINDOWS):
        c0 = g * POOL_GROUP
        cur = e_ref[HIST_ROWS:HIST_ROWS + tm, c0:c0 + POOL_GROUP]
        acc = cur
        for k in range(1, w):
            acc = acc + e_ref[HIST_ROWS - k:HIST_ROWS - k + tm, c0:c0 + POOL_GROUP]
        inv_cnt = 1.0 / jnp.minimum(w, pos + 1).astype(F32)
        out.append(acc * inv_cnt - cur)
    return out


def _prompt_in_kernel(x_ref, ln_ref, wqkvT_ref, wrest_ref, qn_ref, kn_ref, poolw_ref, pscale_ref,
                      kT_ref, vT_ref, qa_ref, ka_ref, vTb_ref, b_ref, sga_ref, sgb_ref,
                      hist_ref, e_scr, km_scr, *, tm, n_blocks):
    i = pl.program_id(1)
    n_i = pl.num_programs(1)
    blocks_per_tile = tm // MOBA_BLOCK

    @pl.when(i == 0)
    def _():
        e_scr[0:HIST_ROWS, :] = jnp.zeros((HIST_ROWS, POOL_WIDTH), F32)
        km_scr[...] = jnp.zeros(km_scr.shape, F32)

    xn = _rms_rows(x_ref[...], ln_ref[...]).astype(BF16)

    def proj_t(r0):
        t = lax.dot_general(wqkvT_ref[r0:r0 + ATTN_WIDTH, :], xn, _NT, preferred_element_type=F32)
        return t.reshape(N_HEADS, HEAD_DIM, tm)

    def head_norm(t, g_col):
        ms = jnp.mean(t * t, axis=1, keepdims=True)
        return t * lax.rsqrt(ms + EPS) * g_col[None]

    q = head_norm(proj_t(0), qn_ref[...])
    k = head_norm(proj_t(ATTN_WIDTH), kn_ref[...])
    v = proj_t(2 * ATTN_WIDTH)
    kT_ref[...] = k
    vT_ref[...] = v
    vTb_ref[:, 0:HEAD_DIM, :] = v.astype(BF16)
    ones_row = lax.broadcasted_iota(jnp.int32, (N_HEADS, V_AUG - HEAD_DIM, tm), 1) == 0
    vTb_ref[:, HEAD_DIM:V_AUG, :] = jnp.where(ones_row, 1.0, 0.0).astype(BF16)
    qa_ref[:, 0:HEAD_DIM, :] = (q * (ATTN_SCALE * LOG2E)).astype(BF16)
    key_blk = (i * tm + lax.broadcasted_iota(jnp.int32, (tm, 1), 0)) // MOBA_BLOCK
    onehot = jnp.where(key_blk == lax.broadcasted_iota(jnp.int32, (1, n_blocks), 1), 1.0, 0.0).astype(BF16)
    for h in range(N_HEADS):
        ka_ref[h, :, 0:HEAD_DIM] = k[h].T.astype(BF16)
        ka_ref[h, :, HEAD_DIM:HEAD_DIM + n_blocks] = onehot

    lane = lax.broadcasted_iota(jnp.int32, (1, 1, LANES), 2)
    km = km_scr[...]
    for h in range(blocks_per_tile):
        ks = jnp.sum(k[:, :, h * MOBA_BLOCK:(h + 1) * MOBA_BLOCK], axis=2, keepdims=True) * (1.0 / MOBA_BLOCK)
        km = jnp.where(lane == i * blocks_per_tile + h, ks, km)
    km_scr[...] = km

    pos = i * tm + lax.broadcasted_iota(jnp.int32, (1, tm), 1)
    own = pos // MOBA_BLOCK
    blk = lax.broadcasted_iota(jnp.int32, (n_blocks, 1), 0)
    for h in range(N_HEADS):
        km_hi, km_lo = _split_bf16(km[h][:, 0:n_blocks])
        q_hi, q_lo = _split_bf16(q[h])
        gate = (lax.dot_general(km_hi, q_hi, _TN, preferred_element_type=F32)
                + lax.dot_general(km_hi, q_lo, _TN, preferred_element_type=F32)
                + lax.dot_general(km_lo, q_hi, _TN, preferred_element_type=F32))
        g = jnp.where(blk < own, gate, LOWEST)
        bias = jnp.where(blk == own, 0.0, NEG_INF)
        for _ in range(MOBA_TOP_K):
            m = jnp.max(g, axis=0, keepdims=True)
            idx = jnp.min(jnp.where(g == m, blk, n_blocks), axis=0, keepdims=True)
            idx = jnp.where(m > LOWEST, idx, n_blocks)
            pick = blk == idx
            bias = jnp.where(pick, 0.0, bias)
            g = jnp.where(pick, LOWEST, g)
        qa_ref[h, HEAD_DIM:HEAD_DIM + n_blocks, :] = bias.astype(BF16)

    u = jnp.dot(xn, wrest_ref[:, 0:POOL_WIDTH], preferred_element_type=F32)
    e_scr[HIST_ROWS:HIST_ROWS + tm, :] = u
    d = _pool_means(e_scr, tm, i * tm)
    for g_i in range(len(POOL_WINDOWS)):
        c0 = g_i * POOL_GROUP
        y = jnp.dot(d[g_i].astype(BF16), poolw_ref[g_i], preferred_element_type=F32)
        b_ref[:, c0:c0 + POOL_GROUP] = (y * pscale_ref[:, c0:c0 + POOL_GROUP]).astype(BF16)
    tail = e_scr[tm:tm + HIST_ROWS, :]
    e_scr[0:HIST_ROWS, :] = tail

    @pl.when(i == n_i - 1)
    def _():
        hist_ref[...] = tail

    ga = jnp.dot(xn, wrest_ref[:, POOL_WIDTH:POOL_WIDTH + D_MODEL], preferred_element_type=F32)
    sga_ref[...] = _sigmoid(ga).astype(BF16)
    gb = jnp.dot(xn, wrest_ref[:, POOL_WIDTH + D_MODEL:POOL_WIDTH + 2 * D_MODEL], preferred_element_type=F32)
    sgb_ref[...] = _sigmoid(gb).astype(BF16)


def _prompt_in(x, ln, wqkvT, wrest, qn_col, kn_col, poolw, pscale, tm=512):
    bsz, seq, _ = x.shape
    n_blocks = seq // MOBA_BLOCK
    grid = (bsz, seq // tm)
    hT = lambda dt: jax.ShapeDtypeStruct((bsz, N_HEADS, HEAD_DIM, seq), dt)
    hT_spec = pl.BlockSpec((None, N_HEADS, HEAD_DIM, tm), lambda b, i: (b, 0, 0, i))
    row_spec = lambda w: pl.BlockSpec((None, tm, w), lambda b, i: (b, i, 0))
    aug = HEAD_DIM + n_blocks
    out_shape = (
        hT(F32), hT(F32),
        jax.ShapeDtypeStruct((bsz, N_HEADS, aug, seq), BF16),
        jax.ShapeDtypeStruct((bsz, N_HEADS, seq, aug), BF16),
        jax.ShapeDtypeStruct((bsz, N_HEADS, V_AUG, seq), BF16),
        jax.ShapeDtypeStruct((bsz, seq, POOL_WIDTH), BF16),
        jax.ShapeDtypeStruct((bsz, seq, D_MODEL), BF16),
        jax.ShapeDtypeStruct((bsz, seq, D_MODEL), BF16),
        jax.ShapeDtypeStruct((bsz, HIST_ROWS, POOL_WIDTH), F32),
    )
    out_specs = (
        hT_spec, hT_spec,
        pl.BlockSpec((None, N_HEADS, aug, tm), lambda b, i: (b, 0, 0, i)),
        pl.BlockSpec((None, N_HEADS, tm, aug), lambda b, i: (b, 0, i, 0)),
        pl.BlockSpec((None, N_HEADS, V_AUG, tm), lambda b, i: (b, 0, 0, i)),
        row_spec(POOL_WIDTH), row_spec(D_MODEL), row_spec(D_MODEL),
        pl.BlockSpec((None, HIST_ROWS, POOL_WIDTH), lambda b, i: (b, 0, 0)),
    )
    in_specs = [
        row_spec(D_MODEL),
        _const_spec(ln.shape), _const_spec(wqkvT.shape), _const_spec(wrest.shape),
        _const_spec(qn_col.shape), _const_spec(kn_col.shape),
        _const_spec(poolw.shape), _const_spec(pscale.shape),
    ]
    return pl.pallas_call(
        functools.partial(_prompt_in_kernel, tm=tm, n_blocks=n_blocks),
        grid=grid, in_specs=in_specs, out_specs=out_specs, out_shape=out_shape,
        scratch_shapes=[pltpu.VMEM((HIST_ROWS + tm, POOL_WIDTH), F32),
                        pltpu.VMEM((N_HEADS, HEAD_DIM, LANES), F32)],
        compiler_params=pltpu.CompilerParams(
            dimension_semantics=("arbitrary", "arbitrary"), vmem_limit_bytes=VMEM_LIMIT),
        name="prompt_in",
    )(x, ln, wqkvT, wrest, qn_col, kn_col, poolw, pscale)


_ATTN_UNROLL = 10


def _prompt_attn_kernel(qx_ref, qy_ref, ka_ref, vT_ref, ox_ref, oy_ref, q_scr, m_scr, acc_scr, s_scr, cm_scr,
                        *, heads, n_tiles):
    p = pl.program_id(1)
    own = (p, n_tiles - 1 - p)
    blk = MOBA_BLOCK
    tri = (lax.broadcasted_iota(jnp.int32, (blk, 1), 0) <= lax.broadcasted_iota(jnp.int32, (1, blk), 1))
    q_scr[0] = qx_ref[...]
    q_scr[1] = qy_ref[...]
    m_scr[...] = jnp.full(m_scr.shape, NEG_INF, F32)
    acc_scr[...] = jnp.zeros(acc_scr.shape, F32)

    def chain_block(n):
        idx = n - 2
        c = jnp.where(n == 1, 1, (idx >= own[0]).astype(jnp.int32))
        j = jnp.where(n == 1, own[1], idx - c * own[0])
        return c, j

    def scores(c, j, slot, diag, g):
        r0 = pl.multiple_of(j * blk, blk)
        s = jnp.dot(ka_ref[g, pl.ds(r0, blk), :], q_scr[c, g], preferred_element_type=F32)
        if diag:
            s = jnp.where(tri, s, NEG_INF)
        s_scr[slot, g] = s
        cm_scr[slot, g] = jnp.max(s, axis=0, keepdims=True)

    def softmax_pv(c, j, slot, g):
        r0 = pl.multiple_of(j * blk, blk)
        m = m_scr[c, g]
        m_new = jnp.maximum(m, cm_scr[slot, g])
        alpha = jnp.exp2(m - m_new)
        pr = jnp.exp2((s_scr[slot, g] - m_new).astype(BF16))
        m_scr[c, g] = m_new
        pv = jnp.dot(vT_ref[g, :, pl.ds(r0, blk)], pr, preferred_element_type=F32)
        acc_scr[c, g] = alpha * acc_scr[c, g] + pv

    def step(nxt, nxt_slot, nxt_diag, cur, cur_slot):
        for g in range(heads):
            if nxt is not None:
                scores(*nxt, nxt_slot, nxt_diag, g)
            if cur is not None:
                softmax_pv(*cur, cur_slot, g)

    step((0, own[0]), 0, True, None, None)
    step((1, own[1]), 1, True, (0, own[0]), 0)

    def trip(i, carry):
        for u in range(_ATTN_UNROLL):
            n = 1 + _ATTN_UNROLL * i + u
            step(chain_block(n + 1), u % 2, False, chain_block(n), (u + 1) % 2)
        return carry

    lax.fori_loop(0, (n_tiles - 2) // _ATTN_UNROLL, trip, 0)
    last = n_tiles - 1
    step(chain_block(last + 1), (last + 1) % 2, False, chain_block(last), last % 2)
    step(None, None, False, chain_block(last + 1), (last + 1) % 2)
    for o_ref, c in ((ox_ref, 0), (oy_ref, 1)):
        for g in range(heads):
            o_ref[g] = (acc_scr[c, g, 0:HEAD_DIM] / acc_scr[c, g, HEAD_DIM:HEAD_DIM + 1]).astype(o_ref.dtype)


def _prompt_attn(qa, ka, vT, heads=N_HEADS):
    bsz, _, aug, seq = qa.shape
    blk = MOBA_BLOCK
    n_tiles = seq // blk
    half = n_tiles // 2
    assert heads == N_HEADS and n_tiles % 2 == 0 and (n_tiles - 2) % _ATTN_UNROLL == 0 and _ATTN_UNROLL % 2 == 0
    grid = (bsz, half)
    resident = lambda r, c: pl.BlockSpec((None, heads, r, c), lambda b, p: (b, 0, 0, 0),
                                         pipeline_mode=pl.Buffered(1))
    q_spec = lambda f: pl.BlockSpec((None, heads, aug, blk), lambda b, p: (b, 0, 0, f(p)))
    o_spec = lambda f: pl.BlockSpec((None, heads, HEAD_DIM, blk), lambda b, p: (b, 0, 0, f(p)))
    o_shape = jax.ShapeDtypeStruct((bsz, N_HEADS, HEAD_DIM, seq // 2), BF16)
    return pl.pallas_call(
        functools.partial(_prompt_attn_kernel, heads=heads, n_tiles=n_tiles),
        grid=grid,
        in_specs=[q_spec(lambda p: p), q_spec(lambda p: n_tiles - 1 - p),
                  resident(seq, aug), resident(V_AUG, seq)],
        out_specs=(o_spec(lambda p: p), o_spec(lambda p: half - 1 - p)),
        out_shape=(o_shape, o_shape),
        scratch_shapes=[pltpu.VMEM((2, heads, aug, blk), BF16),
                        pltpu.VMEM((2, heads, 1, blk), F32),
                        pltpu.VMEM((2, heads, V_AUG, blk), F32), pltpu.VMEM((2, heads, blk, blk), F32),
                        pltpu.VMEM((2, heads, 1, blk), F32)],
        compiler_params=pltpu.CompilerParams(
            dimension_semantics=("arbitrary", "arbitrary"), vmem_limit_bytes=VMEM_LIMIT),
        name="prompt_attn",
    )(qa, qa, ka, vT)


_GATE_BUFS = 16
_GATE_GROUP = 8


def _seq_column(t, seq_i):
    lane = lax.broadcasted_iota(jnp.int32, (1, 1, t.shape[2]), 2)
    return jnp.sum(jnp.where(lane == seq_i, t, 0.0), axis=2, keepdims=True)


def _top_k_blocks(s_scr, sel_ref, n_blocks):
    lane = lax.broadcasted_iota(jnp.int32, (1, LANES), 1)
    gate = jnp.zeros((N_HEADS, LANES), F32)
    for j in range(n_blocks):
        ssum = jnp.sum(s_scr[j], axis=1, keepdims=True) * (1.0 / MOBA_BLOCK)
        gate = jnp.where(lane == j, ssum, gate)
    g = jnp.where(lane < n_blocks, gate, LOWEST)
    sel = jnp.zeros((N_HEADS, LANES), jnp.int32)
    for r in range(MOBA_TOP_K):
        m = jnp.max(g, axis=1, keepdims=True)
        idx = jnp.min(jnp.where(g == m, lane, LANES), axis=1, keepdims=True)
        sel = jnp.where(lane == r, idx, sel)
        g = jnp.where(lane == idx, LOWEST, g)
    sel_ref[...] = sel


def _tail_kernel(*refs, ff_chunk, n_pages, half_tiles):
    if n_pages is None:
        (x_ref, alo_ref, ahi_ref, b_ref, sga_ref, sgb_ref, p_ref, wao_ref, wpo_ref, wo_ref, lnm_ref,
         wup_ref, wdn_ref, lnp_ref, wpg_ref, wpp_ref, y_ref) = refs
    else:
        (pt_ref, x_ref, alo_ref, ahi_ref, b_ref, sga_ref, sgb_ref, p_ref, wao_ref, wpo_ref, wo_ref, lnm_ref,
         wup_ref, wdn_ref, lnp_ref, wpg_ref, wpp_ref, q_ref, kc_ref, y_ref, sel_ref,
         kbuf, sem, s_scr, qb_scr) = refs
        seq_i = pl.program_id(0) * pl.num_programs(1) + pl.program_id(1)
        n_seq = pl.num_programs(0) * pl.num_programs(1)
        n_blocks = n_pages // PAGES_PER_BLOCK

        def page_copy(s_i, blk_i, slot, pg):
            page = pt_ref[s_i * n_pages + blk_i * PAGES_PER_BLOCK + pg]
            return pltpu.make_async_copy(kc_ref.at[page], kbuf.at[slot, pg], sem.at[slot, pg])

        def start(s_i, blk_i, slot):
            for pg in range(PAGES_PER_BLOCK):
                page_copy(s_i, blk_i, slot, pg).start()

        def wait_group(k):
            for blk_i in range(k * _GATE_GROUP, (k + 1) * _GATE_GROUP):
                for pg in range(PAGES_PER_BLOCK):
                    page_copy(seq_i, blk_i, blk_i % _GATE_BUFS, pg).wait()

        def score_group(k):
            for h in range(N_HEADS):
                qh = qb_scr[h]
                for blk_i in range(k * _GATE_GROUP, (k + 1) * _GATE_GROUP):
                    slot = blk_i % _GATE_BUFS
                    ksum = kbuf[slot, 0, h]
                    for pg in range(1, PAGES_PER_BLOCK):
                        ksum = ksum + kbuf[slot, pg, h]
                    s_scr[blk_i, h:h + 1, :] = jnp.sum(ksum * qh, axis=0, keepdims=True)

        def refill_group(k):
            blks = range(k * _GATE_GROUP, (k + 1) * _GATE_GROUP)
            for blk_i in blks:
                if blk_i + _GATE_BUFS < n_blocks:
                    start(seq_i, blk_i + _GATE_BUFS, blk_i % _GATE_BUFS)
            wrap = [blk_i for blk_i in blks if blk_i + _GATE_BUFS >= n_blocks]
            if wrap:
                @pl.when(seq_i + 1 < n_seq)
                def _():
                    for blk_i in wrap:
                        start(seq_i + 1, blk_i + _GATE_BUFS - n_blocks, blk_i % _GATE_BUFS)

        @pl.when(seq_i == 0)
        def _():
            for j in range(_GATE_BUFS):
                start(0, j, j)

        qb_scr[...] = jnp.broadcast_to(_seq_column(q_ref[0], seq_i), qb_scr.shape)

    aT = jnp.where(pl.program_id(1) < half_tiles, alo_ref[...], ahi_ref[...])
    a_proj = lax.dot_general(aT, wao_ref[...], _TN, preferred_element_type=F32)
    b_proj = jnp.dot(b_ref[...], wpo_ref[...], preferred_element_type=F32)
    m = sga_ref[...].astype(F32) * a_proj + sgb_ref[...].astype(F32) * b_proj
    x1 = x_ref[...] + jnp.dot(m.astype(BF16), wo_ref[...], preferred_element_type=F32)
    xn = _rms_rows(x1, lnm_ref[...]).astype(BF16)
    acc = x1
    for c in range(D_FF // ff_chunk):
        if n_pages is not None:
            wait_group(c)
        hid = jnp.maximum(jnp.dot(xn, wup_ref[:, c * ff_chunk:(c + 1) * ff_chunk],
                                  preferred_element_type=F32), 0.0)
        if n_pages is not None:
            score_group(c)
        acc = acc + jnp.dot((hid * hid).astype(BF16), wdn_ref[c * ff_chunk:(c + 1) * ff_chunk, :],
                            preferred_element_type=F32)
        if n_pages is not None:
            refill_group(c)
    x2 = acc
    if n_pages is not None:
        _top_k_blocks(s_scr, sel_ref, n_blocks)
    xg = _rms_rows(x2, lnp_ref[...]).astype(BF16)
    gate = _sigmoid(jnp.dot(xg, wpg_ref[...], preferred_element_type=F32))
    emb = jnp.dot(p_ref[...].astype(BF16), wpp_ref[...], preferred_element_type=F32)
    y_ref[...] = x2 + gate * emb


def _tail(x, a_lo, a_hi, b, sga, sgb, p, wao, wpo, wo, lnm, wup, wdn, lnp, wpg, wpp, tm, gate_inputs=None):
    bsz, seq, _ = x.shape
    grid = (bsz, seq // tm)
    n_i = seq // tm
    half_tiles = a_lo.shape[2] // tm
    row_spec = lambda w: pl.BlockSpec((None, tm, w), lambda b_, i, *_: (b_, i, 0))
    in_specs = [
        row_spec(D_MODEL),
        pl.BlockSpec((None, ATTN_WIDTH, tm), lambda b_, i, *_: (b_, 0, jnp.minimum(i, half_tiles - 1))),
        pl.BlockSpec((None, ATTN_WIDTH, tm), lambda b_, i, *_: (b_, 0, jnp.maximum(i - half_tiles, 0))),
        row_spec(POOL_WIDTH), row_spec(D_MODEL), row_spec(D_MODEL), row_spec(PLE_DIM),
    ] + [_const_spec(w.shape) for w in (wao, wpo, wo, lnm, wup, wdn, lnp, wpg, wpp)]
    y_shape = jax.ShapeDtypeStruct((bsz, seq, D_MODEL), F32)
    params = pltpu.CompilerParams(dimension_semantics=("arbitrary", "arbitrary"), vmem_limit_bytes=VMEM_LIMIT)
    args = (x, a_lo, a_hi, b, sga, sgb, p, wao, wpo, wo, lnm, wup, wdn, lnp, wpg, wpp)
    if gate_inputs is None:
        return pl.pallas_call(
            functools.partial(_tail_kernel, ff_chunk=D_MODEL, n_pages=None, half_tiles=half_tiles),
            grid=grid, in_specs=in_specs, out_specs=row_spec(D_MODEL), out_shape=y_shape,
            compiler_params=params, name="tail",
        )(*args)
    pt_flat, qkvT, cache_kT, n_pages = gate_inputs
    n_seq = qkvT.shape[3]
    n_blocks = n_pages // PAGES_PER_BLOCK
    n_groups = n_blocks // _GATE_GROUP
    assert bsz * n_i == n_seq and n_blocks % _GATE_BUFS == 0 and _GATE_BUFS % _GATE_GROUP == 0
    assert D_FF % n_groups == 0
    seq_spec = lambda shape: pl.BlockSpec((None,) + shape, lambda b_, i, *_: (b_ * n_i + i,) + (0,) * len(shape))
    grid_spec = pltpu.PrefetchScalarGridSpec(
        num_scalar_prefetch=1, grid=grid,
        in_specs=in_specs + [_const_spec(qkvT.shape), pl.BlockSpec(memory_space=pl.ANY)],
        out_specs=(row_spec(D_MODEL), seq_spec((N_HEADS, LANES))),
        scratch_shapes=[pltpu.VMEM((_GATE_BUFS, PAGES_PER_BLOCK, N_HEADS, HEAD_DIM, PAGE_SIZE), F32),
                        pltpu.SemaphoreType.DMA((_GATE_BUFS, PAGES_PER_BLOCK)),
                        pltpu.VMEM((n_blocks, N_HEADS, LANES), F32),
                        pltpu.VMEM((N_HEADS, HEAD_DIM, LANES), F32)],
    )
    return pl.pallas_call(
        functools.partial(_tail_kernel, ff_chunk=D_FF // n_groups, n_pages=n_pages, half_tiles=half_tiles),
        grid_spec=grid_spec,
        out_shape=(y_shape, jax.ShapeDtypeStruct((n_seq, N_HEADS, LANES), jnp.int32)),
        compiler_params=params, name="tail_gate",
    )(pt_flat, *args, qkvT, cache_kT)


def _sample_in_kernel(x_ref, ln_ref, wqkvT_ref, wrest_ref, kn_ref, qn_col_ref, kn_col_ref, hist_ref,
                      poolw_ref, pscale_ref,
                      k_ref, v_ref, qkvT_ref, b_ref, sga_ref, sgb_ref, hist_out_ref, *, past_len):
    n = x_ref.shape[0]
    xn = _rms_rows(x_ref[...], ln_ref[...]).astype(BF16)
    k_row = lax.dot_general(xn, wqkvT_ref[ATTN_WIDTH:2 * ATTN_WIDTH, :], _NT, preferred_element_type=F32)
    for h in range(N_HEADS):
        c0 = h * HEAD_DIM
        k_ref[:, c0:c0 + HEAD_DIM] = _rms_rows(k_row[:, c0:c0 + HEAD_DIM], kn_ref[...])
    v_ref[...] = lax.dot_general(xn, wqkvT_ref[2 * ATTN_WIDTH:3 * ATTN_WIDTH, :], _NT, preferred_element_type=F32)
    z = jnp.dot(xn, wrest_ref[...], preferred_element_type=F32)
    zT = lax.dot_general(wqkvT_ref[...], xn, _NT, preferred_element_type=F32).reshape(3, N_HEADS, HEAD_DIM, n)

    def head_norm(t, g_col):
        ms = jnp.mean(t * t, axis=1, keepdims=True)
        return t * lax.rsqrt(ms + EPS) * g_col[None]

    qkvT_ref[0] = head_norm(zT[0], qn_col_ref[...])
    qkvT_ref[1] = head_norm(zT[1], kn_col_ref[...])
    qkvT_ref[2] = zT[2]
    u = z[:, 0:POOL_WIDTH]
    for g_i, w in enumerate(POOL_WINDOWS):
        c0 = g_i * POOL_GROUP
        cur = u[:, c0:c0 + POOL_GROUP]
        acc = cur
        for k in range(1, w):
            acc = acc + hist_ref[POOL_HIST - k, :, c0:c0 + POOL_GROUP]
        cnt = float(min(w, past_len + 1))
        d = acc / cnt - cur
        y = jnp.dot(d.astype(BF16), poolw_ref[g_i], preferred_element_type=F32)
        b_ref[:, c0:c0 + POOL_GROUP] = (y * pscale_ref[:, c0:c0 + POOL_GROUP]).astype(BF16)
    for t in range(POOL_HIST - 1):
        hist_out_ref[t] = hist_ref[t + 1]
    hist_out_ref[POOL_HIST - 1] = u
    g0 = POOL_WIDTH
    sga_ref[...] = _sigmoid(z[:, g0:g0 + D_MODEL]).astype(BF16)
    sgb_ref[...] = _sigmoid(z[:, g0 + D_MODEL:g0 + 2 * D_MODEL]).astype(BF16)


def _sample_in(x, ln, wqkvT, wrest, kn_row, qn_col, kn_col, hist_t, poolw, pscale, past_len):
    n = x.shape[0]
    sds = jax.ShapeDtypeStruct
    out_shape = (sds((n, ATTN_WIDTH), F32), sds((n, ATTN_WIDTH), F32), sds((3, N_HEADS, HEAD_DIM, n), F32),
                 sds((n, POOL_WIDTH), BF16), sds((n, D_MODEL), BF16), sds((n, D_MODEL), BF16),
                 sds(hist_t.shape, F32))
    return pl.pallas_call(
        functools.partial(_sample_in_kernel, past_len=past_len),
        out_shape=out_shape,
        compiler_params=pltpu.CompilerParams(vmem_limit_bytes=VMEM_LIMIT),
        name="sample_in",
    )(x, ln, wqkvT, wrest, kn_row, qn_col, kn_col, hist_t, poolw, pscale)


def _sample_attn_kernel(pt_ref, sel_ref, qkvT_ref, kc_ref, vc_ref, o_ref,
                        kbuf, vbuf, sem, *, n_pages):
    b = pl.program_id(0)
    n_b = pl.num_programs(0)
    n_slots = MOBA_TOP_K * PAGES_PER_BLOCK

    def copies(seq_i, buf_i):
        out = []
        for h in range(N_HEADS):
            for r in range(MOBA_TOP_K):
                blk = sel_ref[(seq_i * N_HEADS + h) * MOBA_TOP_K + r]
                for pg in range(PAGES_PER_BLOCK):
                    page = pt_ref[seq_i * n_pages + blk * PAGES_PER_BLOCK + pg]
                    s_i = r * PAGES_PER_BLOCK + pg
                    out.append(pltpu.make_async_copy(kc_ref.at[page, h], kbuf.at[buf_i, h, s_i], sem.at[buf_i, 0]))
                    out.append(pltpu.make_async_copy(vc_ref.at[page, h], vbuf.at[buf_i, h, s_i], sem.at[buf_i, 1]))
        return out

    @pl.when(b == 0)
    def _():
        for c in copies(0, 0):
            c.start()

    cur = b % 2

    @pl.when(b + 1 < n_b)
    def _():
        for c in copies(b + 1, 1 - cur):
            c.start()

    for c in copies(b, cur):
        c.wait()

    q_col, kn_col, vn_col = (_seq_column(qkvT_ref[i], b) for i in range(3))
    seq_lane = lax.broadcasted_iota(jnp.int32, (1, o_ref.shape[2]), 1)

    @pl.when(b == 0)
    def _():
        o_ref[...] = jnp.zeros(o_ref.shape, F32)

    for h in range(N_HEADS):
        qh = q_col[h]
        kb = kbuf[cur, h]
        vb = vbuf[cur, h]
        s = jnp.sum(kb * qh[None], axis=1) * ATTN_SCALE
        s_new = jnp.sum(qh * kn_col[h], axis=0, keepdims=True) * ATTN_SCALE
        m = jnp.maximum(jnp.max(jnp.max(s, axis=1, keepdims=True), axis=0, keepdims=True), s_new)
        p = jnp.exp(s - m)
        p_new = jnp.exp(s_new - m)
        l = jnp.sum(jnp.sum(p, axis=1, keepdims=True), axis=0, keepdims=True) + p_new
        pv = vb[0] * p[0:1]
        for s_i in range(1, n_slots):
            pv = pv + vb[s_i] * p[s_i:s_i + 1]
        o = jnp.sum(pv, axis=1, keepdims=True) + p_new * vn_col[h]
        o_ref[h] = jnp.where(seq_lane == b, o / l, o_ref[h])


def _sample_attn(page_table_flat, sel_flat, qkvT, cache_kT, cache_vT, n_seq, n_pages):
    n_slots = MOBA_TOP_K * PAGES_PER_BLOCK
    buf = pltpu.VMEM((2, N_HEADS, n_slots, HEAD_DIM, PAGE_SIZE), F32)
    grid_spec = pltpu.PrefetchScalarGridSpec(
        num_scalar_prefetch=2,
        grid=(n_seq,),
        in_specs=[_const_spec(qkvT.shape), pl.BlockSpec(memory_space=pl.ANY), pl.BlockSpec(memory_space=pl.ANY)],
        out_specs=pl.BlockSpec((N_HEADS, HEAD_DIM, n_seq), lambda b, pt, sel: (0, 0, 0)),
        scratch_shapes=[buf, buf, pltpu.SemaphoreType.DMA((2, 2))],
    )
    return pl.pallas_call(
        functools.partial(_sample_attn_kernel, n_pages=n_pages),
        grid_spec=grid_spec,
        out_shape=jax.ShapeDtypeStruct((N_HEADS, HEAD_DIM, n_seq), F32),
        compiler_params=pltpu.CompilerParams(
            dimension_semantics=("arbitrary",), vmem_limit_bytes=VMEM_LIMIT),
        name="sample_attn",
    )(page_table_flat, sel_flat, qkvT, cache_kT, cache_vT)


def kernel(x_prompt, x_sample, cache_k, cache_v, state_pool, page_table, p_prompt, p_sample, ln_mix, w_in,
           q_norm, k_norm, pool_w, pool_scale, w_attn_out, w_pool_out, w_out, ln_mlp, w_up, w_down, ln_ple,
           w_ple_gate, w_ple_proj):
    depth = w_in.shape[0]
    assert depth == 1
    n_seq, n_pages = page_table.shape
    past_len = n_pages * PAGE_SIZE
    assert x_sample.shape[1] == 1 and past_len % MOBA_BLOCK == 0 and past_len // MOBA_BLOCK >= MOBA_TOP_K
    assert (past_len // MOBA_BLOCK) % _GATE_BUFS == 0 and (x_prompt.shape[1] // MOBA_BLOCK) % 16 == 0
    bsz, seq, _ = x_prompt.shape
    l = 0

    wqkvT = w_in[l][:, :3 * ATTN_WIDTH].T.astype(BF16)
    wrest = w_in[l][:, 3 * ATTN_WIDTH:].astype(BF16)
    poolw = pool_w[l].astype(BF16)
    tail_w = (w_attn_out[l].astype(BF16), w_pool_out[l].astype(BF16), w_out[l].astype(BF16), ln_mlp,
              w_up[l].astype(BF16), w_down[l].astype(BF16), ln_ple, w_ple_gate[l].astype(BF16),
              w_ple_proj[l].astype(BF16))

    xs = x_sample.reshape(n_seq, D_MODEL)
    hist_t = state_pool[l].transpose(1, 0, 2)
    qn_col, kn_col = q_norm.reshape(HEAD_DIM, 1), k_norm.reshape(HEAD_DIM, 1)
    k_s, v_s, qkvT_s, b_s, sga_s, sgb_s, hist_s = _sample_in(
        xs, ln_mix, wqkvT, wrest, k_norm, qn_col, kn_col, hist_t, poolw, pool_scale, past_len)
    cache_kT = cache_k[l].transpose(0, 2, 3, 1)
    cache_vT = cache_v[l].transpose(0, 2, 3, 1)
    pt_flat = page_table.reshape(-1)

    (kT, vT, qa, ka, vTb, b_p, sga, sgb, hist_p) = _prompt_in(
        x_prompt, ln_mix, wqkvT, wrest, qn_col, kn_col, poolw, pool_scale)
    a_lo, a_hi = (a.reshape(bsz, ATTN_WIDTH, seq // 2) for a in _prompt_attn(qa, ka, vTb))
    y_prompt, sel = _tail(x_prompt, a_lo, a_hi, b_p, sga, sgb, p_prompt[l], *tail_w, tm=512,
                          gate_inputs=(pt_flat, qkvT_s, cache_kT, n_pages))
    k_prompt = kT.transpose(0, 3, 1, 2)[None]
    v_prompt = vT.transpose(0, 3, 1, 2)[None]
    pool_prompt = hist_p[:, HIST_ROWS - POOL_HIST:][None]

    sel_flat = sel[:, :, :MOBA_TOP_K].reshape(-1)
    a_s = _sample_attn(pt_flat, sel_flat, qkvT_s, cache_kT, cache_vT, n_seq, n_pages)
    aT_s = a_s.reshape(1, ATTN_WIDTH, n_seq).astype(BF16)
    y_sample = _tail(xs[None], aT_s, aT_s, b_s[None], sga_s[None], sgb_s[None], p_sample[l].reshape(1, n_seq, PLE_DIM),
                     *tail_w, tm=n_seq)
    y_sample = y_sample.reshape(n_seq, 1, D_MODEL)
    k_sample = k_s.reshape(1, n_seq, 1, N_HEADS, HEAD_DIM)
    v_sample = v_s.reshape(1, n_seq, 1, N_HEADS, HEAD_DIM)
    pool_sample = hist_s.transpose(1, 0, 2)[None]
    return (y_prompt, y_sample, k_prompt, v_prompt, pool_prompt, k_sample, v_sample, pool_sample)
```

```python
import functools

import jax
import jax.numpy as jnp
from jax import lax
from jax.experimental import pallas as pl
from jax.experimental.pallas import tpu as pltpu

D_MODEL = 1024
N_HEADS = 8
HEAD_DIM = 64
ATTN_WIDTH = N_HEADS * HEAD_DIM
MOBA_BLOCK = 256
MOBA_TOP_K = 3
PAGE_SIZE = 128
PAGES_PER_BLOCK = MOBA_BLOCK // PAGE_SIZE
ATTN_SCALE = HEAD_DIM ** -0.5
POOL_WINDOWS = (2, 4, 8, 16)
POOL_GROUP = 128
POOL_WIDTH = 512
POOL_HIST = 15
HIST_ROWS = 16
V_AUG = HEAD_DIM + 16
D_FF = 4 * D_MODEL
PLE_DIM = 256
EPS = 1e-6
NEG_INF = -1e30
LOWEST = -3e38
LOG2E = 1.4426950408889634

LANES = 128
VMEM_LIMIT = 60 * 1024 * 1024

F32 = jnp.float32
BF16 = jnp.bfloat16

_NT = (((1,), (1,)), ((), ()))
_TN = (((0,), (0,)), ((), ()))


def _rms_rows(x, g):
    ms = jnp.mean(x * x, axis=-1, keepdims=True)
    return x * lax.rsqrt(ms + EPS) * g


def _sigmoid(x):
    return 0.5 * jnp.tanh(0.5 * x) + 0.5


def _split_bf16(a):
    hi = a.astype(BF16)
    lo = (a - hi.astype(F32)).astype(BF16)
    return hi, lo


def _const_spec(shape):
    zeros = (0,) * len(shape)
    return pl.BlockSpec(shape, lambda *_: zeros, pipeline_mode=pl.Buffered(1))


def _pool_means(e_ref, tm, pos0):
    pos = pos0 + lax.broadcasted_iota(jnp.int32, (tm, 1), 0)
    out = []
    for g, w in enumerate(POOL_WINDOWS):
        assert w & (w - 1) == 0 and w - 1 <= HIST_ROWS
        c0 = g * POOL_GROUP
        acc = e_ref[:, c0:c0 + POOL_GROUP]
        shift = 1
        while shift < w:
            acc = acc + pltpu.roll(acc, shift, 0)
            shift *= 2
        cur = e_ref[HIST_ROWS:HIST_ROWS + tm, c0:c0 + POOL_GROUP]
        inv_cnt = 1.0 / jnp.minimum(w, pos + 1).astype(F32)
        out.append(acc[HIST_ROWS:] * inv_cnt - cur)
    return out


def _prompt_in_kernel(x_ref, ln_ref, wqkvT_ref, wrest_ref, qn_ref, kn_ref, poolw_ref, pscale_ref,
                      kT_ref, vT_ref, qa_ref, ka_ref, vTb_ref, b_ref, sga_ref, sgb_ref,
                      hist_ref, e_scr, km_scr, *, tm, n_blocks):
    i = pl.program_id(1)
    n_i = pl.num_programs(1)
    blocks_per_tile = tm // MOBA_BLOCK

    @pl.when(i == 0)
    def _():
        e_scr[0:HIST_ROWS, :] = jnp.zeros((HIST_ROWS, POOL_WIDTH), F32)
        km_scr[...] = jnp.zeros(km_scr.shape, F32)

    xn = _rms_rows(x_ref[...], ln_ref[...]).astype(BF16)

    def proj_t(r0):
        t = lax.dot_general(wqkvT_ref[r0:r0 + ATTN_WIDTH, :], xn, _NT, preferred_element_type=F32)
        return t.reshape(N_HEADS, HEAD_DIM, tm)

    def head_norm(t, g_col):
        ms = jnp.mean(t * t, axis=1, keepdims=True)
        return t * lax.rsqrt(ms + EPS) * g_col[None]

    q = head_norm(proj_t(0), qn_ref[...])
    k = head_norm(proj_t(ATTN_WIDTH), kn_ref[...])
    v = proj_t(2 * ATTN_WIDTH)
    kT_ref[...] = k
    vT_ref[...] = v
    vTb_ref[:, 0:HEAD_DIM, :] = v.astype(BF16)
    ones_row = lax.broadcasted_iota(jnp.int32, (N_HEADS, V_AUG - HEAD_DIM, tm), 1) == 0
    vTb_ref[:, HEAD_DIM:V_AUG, :] = jnp.where(ones_row, 1.0, 0.0).astype(BF16)
    qa_ref[:, 0:HEAD_DIM, :] = (q * (ATTN_SCALE * LOG2E)).astype(BF16)
    key_blk = (i * tm + lax.broadcasted_iota(jnp.int32, (tm, 1), 0)) // MOBA_BLOCK
    onehot = jnp.where(key_blk == lax.broadcasted_iota(jnp.int32, (1, n_blocks), 1), 1.0, 0.0).astype(BF16)
    for h in range(N_HEADS):
        ka_ref[h, :, 0:HEAD_DIM] = k[h].T.astype(BF16)
        ka_ref[h, :, HEAD_DIM:HEAD_DIM + n_blocks] = onehot

    lane = lax.broadcasted_iota(jnp.int32, (1, 1, LANES), 2)
    km = km_scr[...]
    for h in range(blocks_per_tile):
        ks = jnp.sum(k[:, :, h * MOBA_BLOCK:(h + 1) * MOBA_BLOCK], axis=2, keepdims=True) * (1.0 / MOBA_BLOCK)
        km = jnp.where(lane == i * blocks_per_tile + h, ks, km)
    km_scr[...] = km

    pos = i * tm + lax.broadcasted_iota(jnp.int32, (1, tm), 1)
    own = pos // MOBA_BLOCK
    blk = lax.broadcasted_iota(jnp.int32, (n_blocks, 1), 0)
    for h in range(N_HEADS):
        km_hi, km_lo = _split_bf16(km[h][:, 0:n_blocks])
        q_hi, q_lo = _split_bf16(q[h])
        gate = (lax.dot_general(km_hi, q_hi, _TN, preferred_element_type=F32)
                + lax.dot_general(km_hi, q_lo, _TN, preferred_element_type=F32)
                + lax.dot_general(km_lo, q_hi, _TN, preferred_element_type=F32))
        g = jnp.where(blk < own, gate, LOWEST)
        bias = jnp.where(blk == own, 0.0, NEG_INF)
        for _ in range(MOBA_TOP_K):
            m = jnp.max(g, axis=0, keepdims=True)
            idx = jnp.min(jnp.where(g == m, blk, n_blocks), axis=0, keepdims=True)
            idx = jnp.where(m > LOWEST, idx, n_blocks)
            pick = blk == idx
            bias = jnp.where(pick, 0.0, bias)
            g = jnp.where(pick, LOWEST, g)
        qa_ref[h, HEAD_DIM:HEAD_DIM + n_blocks, :] = bias.astype(BF16)

    u = jnp.dot(xn, wrest_ref[:, 0:POOL_WIDTH], preferred_element_type=F32)
    e_scr[HIST_ROWS:HIST_ROWS + tm, :] = u
    d = _pool_means(e_scr, tm, i * tm)
    for g_i in range(len(POOL_WINDOWS)):
        c0 = g_i * POOL_GROUP
        y = jnp.dot(d[g_i].astype(BF16), poolw_ref[g_i], preferred_element_type=F32)
        b_ref[:, c0:c0 + POOL_GROUP] = (y * pscale_ref[:, c0:c0 + POOL_GROUP]).astype(BF16)
    tail = e_scr[tm:tm + HIST_ROWS, :]
    e_scr[0:HIST_ROWS, :] = tail

    @pl.when(i == n_i - 1)
    def _():
        hist_ref[...] = tail

    ga = jnp.dot(xn, wrest_ref[:, POOL_WIDTH:POOL_WIDTH + D_MODEL], preferred_element_type=F32)
    sga_ref[...] = _sigmoid(ga).astype(BF16)
    gb = jnp.dot(xn, wrest_ref[:, POOL_WIDTH + D_MODEL:POOL_WIDTH + 2 * D_MODEL], preferred_element_type=F32)
    sgb_ref[...] = _sigmoid(gb).astype(BF16)


def _prompt_in(x, ln, wqkvT, wrest, qn_col, kn_col, poolw, pscale, tm=512):
    bsz, seq, _ = x.shape
    n_blocks = seq // MOBA_BLOCK
    grid = (bsz, seq // tm)
    hT = lambda dt: jax.ShapeDtypeStruct((bsz, N_HEADS, HEAD_DIM, seq), dt)
    hT_spec = pl.BlockSpec((None, N_HEADS, HEAD_DIM, tm), lambda b, i: (b, 0, 0, i))
    row_spec = lambda w: pl.BlockSpec((None, tm, w), lambda b, i: (b, i, 0))
    aug = HEAD_DIM + n_blocks
    out_shape = (
        hT(F32), hT(F32),
        jax.ShapeDtypeStruct((bsz, N_HEADS, aug, seq), BF16),
        jax.ShapeDtypeStruct((bsz, N_HEADS, seq, aug), BF16),
        jax.ShapeDtypeStruct((bsz, N_HEADS, V_AUG, seq), BF16),
        jax.ShapeDtypeStruct((bsz, seq, POOL_WIDTH), BF16),
        jax.ShapeDtypeStruct((bsz, seq, D_MODEL), BF16),
        jax.ShapeDtypeStruct((bsz, seq, D_MODEL), BF16),
        jax.ShapeDtypeStruct((bsz, HIST_ROWS, POOL_WIDTH), F32),
    )
    out_specs = (
        hT_spec, hT_spec,
        pl.BlockSpec((None, N_HEADS, aug, tm), lambda b, i: (b, 0, 0, i)),
        pl.BlockSpec((None, N_HEADS, tm, aug), lambda b, i: (b, 0, i, 0)),
        pl.BlockSpec((None, N_HEADS, V_AUG, tm), lambda b, i: (b, 0, 0, i)),
        row_spec(POOL_WIDTH), row_spec(D_MODEL), row_spec(D_MODEL),
        pl.BlockSpec((None, HIST_ROWS, POOL_WIDTH), lambda b, i: (b, 0, 0)),
    )
    in_specs = [
        row_spec(D_MODEL),
        _const_spec(ln.shape), _const_spec(wqkvT.shape), _const_spec(wrest.shape),
        _const_spec(qn_col.shape), _const_spec(kn_col.shape),
        _const_spec(poolw.shape), _const_spec(pscale.shape),
    ]
    return pl.pallas_call(
        functools.partial(_prompt_in_kernel, tm=tm, n_blocks=n_blocks),
        grid=grid, in_specs=in_specs, out_specs=out_specs, out_shape=out_shape,
        scratch_shapes=[pltpu.VMEM((HIST_ROWS + tm, POOL_WIDTH), F32),
                        pltpu.VMEM((N_HEADS, HEAD_DIM, LANES), F32)],
        compiler_params=pltpu.CompilerParams(
            dimension_semantics=("arbitrary", "arbitrary"), vmem_limit_bytes=VMEM_LIMIT),
        name="prompt_in",
    )(x, ln, wqkvT, wrest, qn_col, kn_col, poolw, pscale)


_ATTN_UNROLL = 10


def _prompt_attn_kernel(qx_ref, qy_ref, ka_ref, vT_ref, ox_ref, oy_ref, q_scr, m_scr, acc_scr, s_scr, cm_scr,
                        *, heads, n_tiles):
    p = pl.program_id(1)
    own = (p, n_tiles - 1 - p)
    blk = MOBA_BLOCK
    tri = (lax.broadcasted_iota(jnp.int32, (blk, 1), 0) <= lax.broadcasted_iota(jnp.int32, (1, blk), 1))
    q_scr[0] = qx_ref[...]
    q_scr[1] = qy_ref[...]
    m_scr[...] = jnp.full(m_scr.shape, NEG_INF, F32)
    acc_scr[...] = jnp.zeros(acc_scr.shape, F32)

    def chain_block(n):
        idx = n - 2
        c = jnp.where(n == 1, 1, (idx >= own[0]).astype(jnp.int32))
        j = jnp.where(n == 1, own[1], idx - c * own[0])
        return c, j

    def scores(c, j, slot, diag, g):
        r0 = pl.multiple_of(j * blk, blk)
        s = jnp.dot(ka_ref[g, pl.ds(r0, blk), :], q_scr[c, g], preferred_element_type=F32)
        if diag:
            s = jnp.where(tri, s, NEG_INF)
        s_scr[slot, g] = s
        cm_scr[slot, g] = jnp.max(s, axis=0, keepdims=True)

    def softmax_pv(c, j, slot, g):
        r0 = pl.multiple_of(j * blk, blk)
        m = m_scr[c, g]
        m_new = jnp.maximum(m, cm_scr[slot, g])
        alpha = jnp.exp2(m - m_new)
        pr = jnp.exp2((s_scr[slot, g] - m_new).astype(BF16))
        m_scr[c, g] = m_new
        pv = jnp.dot(vT_ref[g, :, pl.ds(r0, blk)], pr, preferred_element_type=F32)
        acc_scr[c, g] = alpha * acc_scr[c, g] + pv

    def step(nxt, nxt_slot, nxt_diag, cur, cur_slot):
        for g in range(heads):
            if nxt is not None:
                scores(*nxt, nxt_slot, nxt_diag, g)
            if cur is not None:
                softmax_pv(*cur, cur_slot, g)

    step((0, own[0]), 0, True, None, None)
    step((1, own[1]), 1, True, (0, own[0]), 0)

    def trip(i, carry):
        for u in range(_ATTN_UNROLL):
            n = 1 + _ATTN_UNROLL * i + u
            step(chain_block(n + 1), u % 2, False, chain_block(n), (u + 1) % 2)
        return carry

    lax.fori_loop(0, (n_tiles - 2) // _ATTN_UNROLL, trip, 0)
    last = n_tiles - 1
    step(chain_block(last + 1), (last + 1) % 2, False, chain_block(last), last % 2)
    step(None, None, False, chain_block(last + 1), (last + 1) % 2)
    for o_ref, c in ((ox_ref, 0), (oy_ref, 1)):
        for g in range(heads):
            o_ref[g] = (acc_scr[c, g, 0:HEAD_DIM] / acc_scr[c, g, HEAD_DIM:HEAD_DIM + 1]).astype(o_ref.dtype)


def _prompt_attn(qa, ka, vT, heads=N_HEADS):
    bsz, _, aug, seq = qa.shape
    blk = MOBA_BLOCK
    n_tiles = seq // blk
    half = n_tiles // 2
    assert heads == N_HEADS and n_tiles % 2 == 0 and (n_tiles - 2) % _ATTN_UNROLL == 0 and _ATTN_UNROLL % 2 == 0
    grid = (bsz, half)
    resident = lambda r, c: pl.BlockSpec((None, heads, r, c), lambda b, p: (b, 0, 0, 0),
                                         pipeline_mode=pl.Buffered(1))
    q_spec = lambda f: pl.BlockSpec((None, heads, aug, blk), lambda b, p: (b, 0, 0, f(p)))
    o_spec = lambda f: pl.BlockSpec((None, heads, HEAD_DIM, blk), lambda b, p: (b, 0, 0, f(p)))
    o_shape = jax.ShapeDtypeStruct((bsz, N_HEADS, HEAD_DIM, seq // 2), BF16)
    return pl.pallas_call(
        functools.partial(_prompt_attn_kernel, heads=heads, n_tiles=n_tiles),
        grid=grid,
        in_specs=[q_spec(lambda p: p), q_spec(lambda p: n_tiles - 1 - p),
                  resident(seq, aug), resident(V_AUG, seq)],
        out_specs=(o_spec(lambda p: p), o_spec(lambda p: half - 1 - p)),
        out_shape=(o_shape, o_shape),
        scratch_shapes=[pltpu.VMEM((2, heads, aug, blk), BF16),
                        pltpu.VMEM((2, heads, 1, blk), F32),
                        pltpu.VMEM((2, heads, V_AUG, blk), F32), pltpu.VMEM((2, heads, blk, blk), F32),
                        pltpu.VMEM((2, heads, 1, blk), F32)],
        compiler_params=pltpu.CompilerParams(
            dimension_semantics=("arbitrary", "arbitrary"), vmem_limit_bytes=VMEM_LIMIT),
        name="prompt_attn",
    )(qa, qa, ka, vT)


_GATE_BUFS = 16
_GATE_GROUP = 8


def _seq_column(t, seq_i):
    lane = lax.broadcasted_iota(jnp.int32, (1, 1, t.shape[2]), 2)
    return jnp.sum(jnp.where(lane == seq_i, t, 0.0), axis=2, keepdims=True)


def _top_k_blocks(s_scr, sel_ref, n_blocks):
    lane = lax.broadcasted_iota(jnp.int32, (1, LANES), 1)
    gate = jnp.zeros((N_HEADS, LANES), F32)
    for j in range(n_blocks):
        ssum = jnp.sum(s_scr[j], axis=1, keepdims=True) * (1.0 / MOBA_BLOCK)
        gate = jnp.where(lane == j, ssum, gate)
    g = jnp.where(lane < n_blocks, gate, LOWEST)
    sel = jnp.zeros((N_HEADS, LANES), jnp.int32)
    for r in range(MOBA_TOP_K):
        m = jnp.max(g, axis=1, keepdims=True)
        idx = jnp.min(jnp.where(g == m, lane, LANES), axis=1, keepdims=True)
        sel = jnp.where(lane == r, idx, sel)
        g = jnp.where(lane == idx, LOWEST, g)
    sel_ref[...] = sel


def _tail_kernel(*refs, ff_chunk, n_pages, half_tiles):
    if n_pages is None:
        (x_ref, alo_ref, ahi_ref, b_ref, sga_ref, sgb_ref, p_ref, wao_ref, wpo_ref, wo_ref, lnm_ref,
         wup_ref, wdn_ref, lnp_ref, wpg_ref, wpp_ref, y_ref) = refs
    else:
        (pt_ref, x_ref, alo_ref, ahi_ref, b_ref, sga_ref, sgb_ref, p_ref, wao_ref, wpo_ref, wo_ref, lnm_ref,
         wup_ref, wdn_ref, lnp_ref, wpg_ref, wpp_ref, q_ref, kc_ref, y_ref, sel_ref,
         kbuf, sem, s_scr, qb_scr) = refs
        seq_i = pl.program_id(0) * pl.num_programs(1) + pl.program_id(1)
        n_seq = pl.num_programs(0) * pl.num_programs(1)
        n_blocks = n_pages // PAGES_PER_BLOCK

        def page_copy(s_i, blk_i, slot, pg):
            page = pt_ref[s_i * n_pages + blk_i * PAGES_PER_BLOCK + pg]
            return pltpu.make_async_copy(kc_ref.at[page], kbuf.at[slot, pg], sem.at[slot, pg])

        def start(s_i, blk_i, slot):
            for pg in range(PAGES_PER_BLOCK):
                page_copy(s_i, blk_i, slot, pg).start()

        def wait_group(k):
            for blk_i in range(k * _GATE_GROUP, (k + 1) * _GATE_GROUP):
                for pg in range(PAGES_PER_BLOCK):
                    page_copy(seq_i, blk_i, blk_i % _GATE_BUFS, pg).wait()

        def score_group(k):
            tot = None
            for h in range(N_HEADS):
                qh = qb_scr[h]
                for blk_i in range(k * _GATE_GROUP, (k + 1) * _GATE_GROUP):
                    slot = blk_i % _GATE_BUFS
                    ksum = kbuf[slot, 0, h]
                    for pg in range(1, PAGES_PER_BLOCK):
                        ksum = ksum + kbuf[slot, pg, h]
                    row = jnp.sum(ksum * qh, axis=0, keepdims=True)
                    s_scr[blk_i, h:h + 1, :] = row
                    tot = row if tot is None else tot + row
            bits = pltpu.bitcast(tot, jnp.uint32)
            return pltpu.bitcast(lax.shift_right_logical(lax.shift_right_logical(bits, jnp.uint32(16)),
                                                         jnp.uint32(16)), F32)

        def refill_group(k):
            blks = range(k * _GATE_GROUP, (k + 1) * _GATE_GROUP)
            for blk_i in blks:
                if blk_i + _GATE_BUFS < n_blocks:
                    start(seq_i, blk_i + _GATE_BUFS, blk_i % _GATE_BUFS)
            wrap = [blk_i for blk_i in blks if blk_i + _GATE_BUFS >= n_blocks]
            if wrap:
                @pl.when(seq_i + 1 < n_seq)
                def _():
                    for blk_i in wrap:
                        start(seq_i + 1, blk_i + _GATE_BUFS - n_blocks, blk_i % _GATE_BUFS)

        @pl.when(seq_i == 0)
        def _():
            for j in range(_GATE_BUFS):
                start(0, j, j)

        qb_scr[...] = jnp.broadcast_to(_seq_column(q_ref[0], seq_i), qb_scr.shape)

    aT = jnp.where(pl.program_id(1) < half_tiles, alo_ref[...], ahi_ref[...])
    a_proj = lax.dot_general(aT, wao_ref[...], _TN, preferred_element_type=F32)
    b_proj = jnp.dot(b_ref[...], wpo_ref[...], preferred_element_type=F32)
    m = sga_ref[...].astype(F32) * a_proj + sgb_ref[...].astype(F32) * b_proj
    x1 = x_ref[...] + jnp.dot(m.astype(BF16), wo_ref[...], preferred_element_type=F32)
    xn = _rms_rows(x1, lnm_ref[...]).astype(BF16)
    acc = x1
    for c in range(D_FF // ff_chunk):
        if n_pages is not None:
            wait_group(c)
        hid = jnp.maximum(jnp.dot(xn, wup_ref[:, c * ff_chunk:(c + 1) * ff_chunk],
                                  preferred_element_type=F32), 0.0)
        if n_pages is not None:
            zero = score_group(c)
            hid = hid + jnp.concatenate([zero] * (ff_chunk // LANES), axis=1)
        acc = acc + jnp.dot((hid * hid).astype(BF16), wdn_ref[c * ff_chunk:(c + 1) * ff_chunk, :],
                            preferred_element_type=F32)
        if n_pages is not None:
            refill_group(c)
    x2 = acc
    if n_pages is not None:
        _top_k_blocks(s_scr, sel_ref, n_blocks)
    xg = _rms_rows(x2, lnp_ref[...]).astype(BF16)
    gate = _sigmoid(jnp.dot(xg, wpg_ref[...], preferred_element_type=F32))
    emb = jnp.dot(p_ref[...].astype(BF16), wpp_ref[...], preferred_element_type=F32)
    y_ref[...] = x2 + gate * emb


def _tail(x, a_lo, a_hi, b, sga, sgb, p, wao, wpo, wo, lnm, wup, wdn, lnp, wpg, wpp, tm, gate_inputs=None):
    bsz, seq, _ = x.shape
    grid = (bsz, seq // tm)
    n_i = seq // tm
    half_tiles = a_lo.shape[2] // tm
    row_spec = lambda w: pl.BlockSpec((None, tm, w), lambda b_, i, *_: (b_, i, 0))
    in_specs = [
        row_spec(D_MODEL),
        pl.BlockSpec((None, ATTN_WIDTH, tm), lambda b_, i, *_: (b_, 0, jnp.minimum(i, half_tiles - 1))),
        pl.BlockSpec((None, ATTN_WIDTH, tm), lambda b_, i, *_: (b_, 0, jnp.maximum(i - half_tiles, 0))),
        row_spec(POOL_WIDTH), row_spec(D_MODEL), row_spec(D_MODEL), row_spec(PLE_DIM),
    ] + [_const_spec(w.shape) for w in (wao, wpo, wo, lnm, wup, wdn, lnp, wpg, wpp)]
    y_shape = jax.ShapeDtypeStruct((bsz, seq, D_MODEL), F32)
    params = pltpu.CompilerParams(dimension_semantics=("arbitrary", "arbitrary"), vmem_limit_bytes=VMEM_LIMIT)
    args = (x, a_lo, a_hi, b, sga, sgb, p, wao, wpo, wo, lnm, wup, wdn, lnp, wpg, wpp)
    if gate_inputs is None:
        return pl.pallas_call(
            functools.partial(_tail_kernel, ff_chunk=D_MODEL, n_pages=None, half_tiles=half_tiles),
            grid=grid, in_specs=in_specs, out_specs=row_spec(D_MODEL), out_shape=y_shape,
            compiler_params=params, name="tail",
        )(*args)
    pt_flat, qkvT, cache_kT, n_pages = gate_inputs
    n_seq = qkvT.shape[3]
    n_blocks = n_pages // PAGES_PER_BLOCK
    n_groups = n_blocks // _GATE_GROUP
    assert bsz * n_i == n_seq and n_blocks % _GATE_BUFS == 0 and _GATE_BUFS % _GATE_GROUP == 0
    assert D_FF % n_groups == 0
    seq_spec = lambda shape: pl.BlockSpec((None,) + shape, lambda b_, i, *_: (b_ * n_i + i,) + (0,) * len(shape))
    grid_spec = pltpu.PrefetchScalarGridSpec(
        num_scalar_prefetch=1, grid=grid,
        in_specs=in_specs + [_const_spec(qkvT.shape), pl.BlockSpec(memory_space=pl.ANY)],
        out_specs=(row_spec(D_MODEL), seq_spec((N_HEADS, LANES))),
        scratch_shapes=[pltpu.VMEM((_GATE_BUFS, PAGES_PER_BLOCK, N_HEADS, HEAD_DIM, PAGE_SIZE), F32),
                        pltpu.SemaphoreType.DMA((_GATE_BUFS, PAGES_PER_BLOCK)),
                        pltpu.VMEM((n_blocks, N_HEADS, LANES), F32),
                        pltpu.VMEM((N_HEADS, HEAD_DIM, LANES), F32)],
    )
    return pl.pallas_call(
        functools.partial(_tail_kernel, ff_chunk=D_FF // n_groups, n_pages=n_pages, half_tiles=half_tiles),
        grid_spec=grid_spec,
        out_shape=(y_shape, jax.ShapeDtypeStruct((n_seq, N_HEADS, LANES), jnp.int32)),
        compiler_params=params, name="tail_gate",
    )(pt_flat, *args, qkvT, cache_kT)


def _sample_in_kernel(x_ref, ln_ref, wqkvT_ref, wrest_ref, kn_ref, qn_col_ref, kn_col_ref, hist_ref,
                      poolw_ref, pscale_ref,
                      k_ref, v_ref, qkvT_ref, b_ref, sga_ref, sgb_ref, hist_out_ref, *, past_len):
    n = x_ref.shape[0]
    xn = _rms_rows(x_ref[...], ln_ref[...]).astype(BF16)
    k_row = lax.dot_general(xn, wqkvT_ref[ATTN_WIDTH:2 * ATTN_WIDTH, :], _NT, preferred_element_type=F32)
    for h in range(N_HEADS):
        c0 = h * HEAD_DIM
        k_ref[:, c0:c0 + HEAD_DIM] = _rms_rows(k_row[:, c0:c0 + HEAD_DIM], kn_ref[...])
    v_ref[...] = lax.dot_general(xn, wqkvT_ref[2 * ATTN_WIDTH:3 * ATTN_WIDTH, :], _NT, preferred_element_type=F32)
    z = jnp.dot(xn, wrest_ref[...], preferred_element_type=F32)
    zT = lax.dot_general(wqkvT_ref[...], xn, _NT, preferred_element_type=F32).reshape(3, N_HEADS, HEAD_DIM, n)

    def head_norm(t, g_col):
        ms = jnp.mean(t * t, axis=1, keepdims=True)
        return t * lax.rsqrt(ms + EPS) * g_col[None]

    qkvT_ref[0] = head_norm(zT[0], qn_col_ref[...])
    qkvT_ref[1] = head_norm(zT[1], kn_col_ref[...])
    qkvT_ref[2] = zT[2]
    u = z[:, 0:POOL_WIDTH]
    for g_i, w in enumerate(POOL_WINDOWS):
        c0 = g_i * POOL_GROUP
        cur = u[:, c0:c0 + POOL_GROUP]
        acc = cur
        for k in range(1, w):
            acc = acc + hist_ref[POOL_HIST - k, :, c0:c0 + POOL_GROUP]
        cnt = float(min(w, past_len + 1))
        d = acc / cnt - cur
        y = jnp.dot(d.astype(BF16), poolw_ref[g_i], preferred_element_type=F32)
        b_ref[:, c0:c0 + POOL_GROUP] = (y * pscale_ref[:, c0:c0 + POOL_GROUP]).astype(BF16)
    for t in range(POOL_HIST - 1):
        hist_out_ref[t] = hist_ref[t + 1]
    hist_out_ref[POOL_HIST - 1] = u
    g0 = POOL_WIDTH
    sga_ref[...] = _sigmoid(z[:, g0:g0 + D_MODEL]).astype(BF16)
    sgb_ref[...] = _sigmoid(z[:, g0 + D_MODEL:g0 + 2 * D_MODEL]).astype(BF16)


def _sample_in(x, ln, wqkvT, wrest, kn_row, qn_col, kn_col, hist_t, poolw, pscale, past_len):
    n = x.shape[0]
    sds = jax.ShapeDtypeStruct
    out_shape = (sds((n, ATTN_WIDTH), F32), sds((n, ATTN_WIDTH), F32), sds((3, N_HEADS, HEAD_DIM, n), F32),
                 sds((n, POOL_WIDTH), BF16), sds((n, D_MODEL), BF16), sds((n, D_MODEL), BF16),
                 sds(hist_t.shape, F32))
    return pl.pallas_call(
        functools.partial(_sample_in_kernel, past_len=past_len),
        out_shape=out_shape,
        compiler_params=pltpu.CompilerParams(vmem_limit_bytes=VMEM_LIMIT),
        name="sample_in",
    )(x, ln, wqkvT, wrest, kn_row, qn_col, kn_col, hist_t, poolw, pscale)


def _sample_attn_kernel(pt_ref, sel_ref, qkvT_ref, kc_ref, vc_ref, o_ref,
                        kbuf, vbuf, sem, *, n_pages):
    b = pl.program_id(0)
    n_b = pl.num_programs(0)
    n_slots = MOBA_TOP_K * PAGES_PER_BLOCK

    def copies(seq_i, buf_i):
        out = []
        for h in range(N_HEADS):
            for r in range(MOBA_TOP_K):
                blk = sel_ref[(seq_i * N_HEADS + h) * MOBA_TOP_K + r]
                for pg in range(PAGES_PER_BLOCK):
                    page = pt_ref[seq_i * n_pages + blk * PAGES_PER_BLOCK + pg]
                    s_i = r * PAGES_PER_BLOCK + pg
                    out.append(pltpu.make_async_copy(kc_ref.at[page, h], kbuf.at[buf_i, h, s_i], sem.at[buf_i, 0]))
                    out.append(pltpu.make_async_copy(vc_ref.at[page, h], vbuf.at[buf_i, h, s_i], sem.at[buf_i, 1]))
        return out

    @pl.when(b == 0)
    def _():
        for c in copies(0, 0):
            c.start()

    cur = b % 2

    @pl.when(b + 1 < n_b)
    def _():
        for c in copies(b + 1, 1 - cur):
            c.start()

    for c in copies(b, cur):
        c.wait()

    q_col, kn_col, vn_col = (_seq_column(qkvT_ref[i], b) for i in range(3))
    seq_lane = lax.broadcasted_iota(jnp.int32, (1, o_ref.shape[2]), 1)

    @pl.when(b == 0)
    def _():
        o_ref[...] = jnp.zeros(o_ref.shape, F32)

    for h in range(N_HEADS):
        qh = q_col[h]
        kb = kbuf[cur, h]
        vb = vbuf[cur, h]
        s = jnp.sum(kb * qh[None], axis=1) * ATTN_SCALE
        s_new = jnp.sum(qh * kn_col[h], axis=0, keepdims=True) * ATTN_SCALE
        m = jnp.maximum(jnp.max(jnp.max(s, axis=1, keepdims=True), axis=0, keepdims=True), s_new)
        p = jnp.exp(s - m)
        p_new = jnp.exp(s_new - m)
        l = jnp.sum(jnp.sum(p, axis=1, keepdims=True), axis=0, keepdims=True) + p_new
        pv = vb[0] * p[0:1]
        for s_i in range(1, n_slots):
            pv = pv + vb[s_i] * p[s_i:s_i + 1]
        o = jnp.sum(pv, axis=1, keepdims=True) + p_new * vn_col[h]
        o_ref[h] = jnp.where(seq_lane == b, o / l, o_ref[h])


def _sample_attn(page_table_flat, sel_flat, qkvT, cache_kT, cache_vT, n_seq, n_pages):
    n_slots = MOBA_TOP_K * PAGES_PER_BLOCK
    buf = pltpu.VMEM((2, N_HEADS, n_slots, HEAD_DIM, PAGE_SIZE), F32)
    grid_spec = pltpu.PrefetchScalarGridSpec(
        num_scalar_prefetch=2,
        grid=(n_seq,),
        in_specs=[_const_spec(qkvT.shape), pl.BlockSpec(memory_space=pl.ANY), pl.BlockSpec(memory_space=pl.ANY)],
        out_specs=pl.BlockSpec((N_HEADS, HEAD_DIM, n_seq), lambda b, pt, sel: (0, 0, 0)),
        scratch_shapes=[buf, buf, pltpu.SemaphoreType.DMA((2, 2))],
    )
    return pl.pallas_call(
        functools.partial(_sample_attn_kernel, n_pages=n_pages),
        grid_spec=grid_spec,
        out_shape=jax.ShapeDtypeStruct((N_HEADS, HEAD_DIM, n_seq), F32),
        compiler_params=pltpu.CompilerParams(
            dimension_semantics=("arbitrary",), vmem_limit_bytes=VMEM_LIMIT),
        name="sample_attn",
    )(page_table_flat, sel_flat, qkvT, cache_kT, cache_vT)


def kernel(x_prompt, x_sample, cache_k, cache_v, state_pool, page_table, p_prompt, p_sample, ln_mix, w_in,
           q_norm, k_norm, pool_w, pool_scale, w_attn_out, w_pool_out, w_out, ln_mlp, w_up, w_down, ln_ple,
           w_ple_gate, w_ple_proj):
    depth = w_in.shape[0]
    assert depth == 1
    n_seq, n_pages = page_table.shape
    past_len = n_pages * PAGE_SIZE
    assert x_sample.shape[1] == 1 and past_len % MOBA_BLOCK == 0 and past_len // MOBA_BLOCK >= MOBA_TOP_K
    assert (past_len // MOBA_BLOCK) % _GATE_BUFS == 0 and (x_prompt.shape[1] // MOBA_BLOCK) % 16 == 0
    bsz, seq, _ = x_prompt.shape
    l = 0

    wqkvT = w_in[l][:, :3 * ATTN_WIDTH].T.astype(BF16)
    wrest = w_in[l][:, 3 * ATTN_WIDTH:].astype(BF16)
    poolw = pool_w[l].astype(BF16)
    tail_w = (w_attn_out[l].astype(BF16), w_pool_out[l].astype(BF16), w_out[l].astype(BF16), ln_mlp,
              w_up[l].astype(BF16), w_down[l].astype(BF16), ln_ple, w_ple_gate[l].astype(BF16),
              w_ple_proj[l].astype(BF16))

    xs = x_sample.reshape(n_seq, D_MODEL)
    hist_t = state_pool[l].transpose(1, 0, 2)
    qn_col, kn_col = q_norm.reshape(HEAD_DIM, 1), k_norm.reshape(HEAD_DIM, 1)
    k_s, v_s, qkvT_s, b_s, sga_s, sgb_s, hist_s = _sample_in(
        xs, ln_mix, wqkvT, wrest, k_norm, qn_col, kn_col, hist_t, poolw, pool_scale, past_len)
    cache_kT = cache_k[l].transpose(0, 2, 3, 1)
    cache_vT = cache_v[l].transpose(0, 2, 3, 1)
    pt_flat = page_table.reshape(-1)

    (kT, vT, qa, ka, vTb, b_p, sga, sgb, hist_p) = _prompt_in(
        x_prompt, ln_mix, wqkvT, wrest, qn_col, kn_col, poolw, pool_scale)
    a_lo, a_hi = (a.reshape(bsz, ATTN_WIDTH, seq // 2) for a in _prompt_attn(qa, ka, vTb))
    y_prompt, sel = _tail(x_prompt, a_lo, a_hi, b_p, sga, sgb, p_prompt[l], *tail_w, tm=512,
                          gate_inputs=(pt_flat, qkvT_s, cache_kT, n_pages))
    k_prompt = kT.transpose(0, 3, 1, 2)[None]
    v_prompt = vT.transpose(0, 3, 1, 2)[None]
    pool_prompt = hist_p[:, HIST_ROWS - POOL_HIST:][None]

    sel_flat = sel[:, :, :MOBA_TOP_K].reshape(-1)
    a_s = _sample_attn(pt_flat, sel_flat, qkvT_s, cache_kT, cache_vT, n_seq, n_pages)
    aT_s = a_s.reshape(1, ATTN_WIDTH, n_seq).astype(BF16)
    y_sample = _tail(xs[None], aT_s, aT_s, b_s[None], sga_s[None], sgb_s[None], p_sample[l].reshape(1, n_seq, PLE_DIM),
                     *tail_w, tm=n_seq)
    y_sample = y_sample.reshape(n_seq, 1, D_MODEL)
    k_sample = k_s.reshape(1, n_seq, 1, N_HEADS, HEAD_DIM)
    v_sample = v_s.reshape(1, n_seq, 1, N_HEADS, HEAD_DIM)
    pool_sample = hist_s.transpose(1, 0, 2)[None]
    return (y_prompt, y_sample, k_prompt, v_prompt, pool_prompt, k_sample, v_sample, pool_sample)
```

```python
import functools

import jax
import jax.numpy as jnp
from jax import lax
from jax.experimental import pallas as pl
from jax.experimental.pallas import tpu as pltpu

D_MODEL = 1024
N_HEADS = 8
HEAD_DIM = 64
ATTN_WIDTH = N_HEADS * HEAD_DIM
MOBA_BLOCK = 256
MOBA_TOP_K = 3
PAGE_SIZE = 128
PAGES_PER_BLOCK = MOBA_BLOCK // PAGE_SIZE
ATTN_SCALE = HEAD_DIM ** -0.5
POOL_WINDOWS = (2, 4, 8, 16)
POOL_GROUP = 128
POOL_WIDTH = 512
POOL_HIST = 15
HIST_ROWS = 16
V_AUG = HEAD_DIM + 16
D_FF = 4 * D_MODEL
PLE_DIM = 256
EPS = 1e-6
NEG_INF = -1e30
LOWEST = -3e38
LOG2E = 1.4426950408889634

LANES = 128
VMEM_LIMIT = 60 * 1024 * 1024

F32 = jnp.float32
BF16 = jnp.bfloat16

_NT = (((1,), (1,)), ((), ()))
_TN = (((0,), (0,)), ((), ()))


def _rms_rows(x, g):
    ms = jnp.mean(x * x, axis=-1, keepdims=True)
    return x * lax.rsqrt(ms + EPS) * g


def _sigmoid(x):
    return 0.5 * jnp.tanh(0.5 * x) + 0.5


def _split_bf16(a):
    hi = a.astype(BF16)
    lo = (a - hi.astype(F32)).astype(BF16)
    return hi, lo


def _const_spec(shape):
    zeros = (0,) * len(shape)
    return pl.BlockSpec(shape, lambda *_: zeros, pipeline_mode=pl.Buffered(1))


def _pool_means(e_ref, tm, pos0):
    pos = pos0 + lax.broadcasted_iota(jnp.int32, (tm, 1), 0)
    out = []
    for g, w in enumerate(POOL_WINDOWS):
        assert w & (w - 1) == 0 and w - 1 <= HIST_ROWS
        c0 = g * POOL_GROUP
        acc = e_ref[:, c0:c0 + POOL_GROUP]
        shift = 1
        while shift < w:
            acc = acc + pltpu.roll(acc, shift, 0)
            shift *= 2
        cur = e_ref[HIST_ROWS:HIST_ROWS + tm, c0:c0 + POOL_GROUP]
        inv_cnt = 1.0 / jnp.minimum(w, pos + 1).astype(F32)
        out.append(acc[HIST_ROWS:] * inv_cnt - cur)
    return out


def _prompt_in_kernel(x_ref, ln_ref, wqkvT_ref, wrest_ref, qn_ref, kn_ref, poolw_ref, pscale_ref,
                      kT_ref, vT_ref, qa_ref, ka_ref, vTb_ref, b_ref, sga_ref, sgb_ref,
                      hist_ref, e_scr, km_scr, *, tm, n_blocks):
    i = pl.program_id(1)
    n_i = pl.num_programs(1)
    blocks_per_tile = tm // MOBA_BLOCK

    @pl.when(i == 0)
    def _():
        e_scr[0:HIST_ROWS, :] = jnp.zeros((HIST_ROWS, POOL_WIDTH), F32)
        km_scr[...] = jnp.zeros(km_scr.shape, F32)

    xn = _rms_rows(x_ref[...], ln_ref[...]).astype(BF16)

    def proj_t(r0):
        t = lax.dot_general(wqkvT_ref[r0:r0 + ATTN_WIDTH, :], xn, _NT, preferred_element_type=F32)
        return t.reshape(N_HEADS, HEAD_DIM, tm)

    def head_norm(t, g_col):
        ms = jnp.mean(t * t, axis=1, keepdims=True)
        return t * lax.rsqrt(ms + EPS) * g_col[None]

    q = head_norm(proj_t(0), qn_ref[...])
    k = head_norm(proj_t(ATTN_WIDTH), kn_ref[...])
    v = proj_t(2 * ATTN_WIDTH)
    kT_ref[...] = k
    vT_ref[...] = v
    vTb_ref[:, 0:HEAD_DIM, :] = v.astype(BF16)
    ones_row = lax.broadcasted_iota(jnp.int32, (N_HEADS, V_AUG - HEAD_DIM, tm), 1) == 0
    vTb_ref[:, HEAD_DIM:V_AUG, :] = jnp.where(ones_row, 1.0, 0.0).astype(BF16)
    qa_ref[:, 0:HEAD_DIM, :] = (q * (ATTN_SCALE * LOG2E)).astype(BF16)
    key_blk = (i * tm + lax.broadcasted_iota(jnp.int32, (tm, 1), 0)) // MOBA_BLOCK
    onehot = jnp.where(key_blk == lax.broadcasted_iota(jnp.int32, (1, n_blocks), 1), 1.0, 0.0).astype(BF16)
    for h in range(N_HEADS):
        ka_ref[h, :, 0:HEAD_DIM] = k[h].T.astype(BF16)
        ka_ref[h, :, HEAD_DIM:HEAD_DIM + n_blocks] = onehot

    lane = lax.broadcasted_iota(jnp.int32, (1, 1, LANES), 2)
    km = km_scr[...]
    for h in range(blocks_per_tile):
        ks = jnp.sum(k[:, :, h * MOBA_BLOCK:(h + 1) * MOBA_BLOCK], axis=2, keepdims=True) * (1.0 / MOBA_BLOCK)
        km = jnp.where(lane == i * blocks_per_tile + h, ks, km)
    km_scr[...] = km

    pos = i * tm + lax.broadcasted_iota(jnp.int32, (1, tm), 1)
    own = pos // MOBA_BLOCK
    blk = lax.broadcasted_iota(jnp.int32, (n_blocks, 1), 0)
    for h in range(N_HEADS):
        km_hi, km_lo = _split_bf16(km[h][:, 0:n_blocks])
        q_hi, q_lo = _split_bf16(q[h])
        gate = (lax.dot_general(km_hi, q_hi, _TN, preferred_element_type=F32)
                + lax.dot_general(km_hi, q_lo, _TN, preferred_element_type=F32)
                + lax.dot_general(km_lo, q_hi, _TN, preferred_element_type=F32))
        g = jnp.where(blk < own, gate, LOWEST)
        bias = jnp.where(blk == own, 0.0, NEG_INF)
        for _ in range(MOBA_TOP_K):
            m = jnp.max(g, axis=0, keepdims=True)
            idx = jnp.min(jnp.where(g == m, blk, n_blocks), axis=0, keepdims=True)
            idx = jnp.where(m > LOWEST, idx, n_blocks)
            pick = blk == idx
            bias = jnp.where(pick, 0.0, bias)
            g = jnp.where(pick, LOWEST, g)
        qa_ref[h, HEAD_DIM:HEAD_DIM + n_blocks, :] = bias.astype(BF16)
        w = 2 * D_MODEL // N_HEADS
        gcol = jnp.dot(xn, wrest_ref[:, POOL_WIDTH + h * w:POOL_WIDTH + (h + 1) * w], preferred_element_type=F32)
        sg_ref = sga_ref if h < N_HEADS // 2 else sgb_ref
        c0 = (h % (N_HEADS // 2)) * w
        sg_ref[:, c0:c0 + w] = _sigmoid(gcol).astype(BF16)

    u = jnp.dot(xn, wrest_ref[:, 0:POOL_WIDTH], preferred_element_type=F32)
    e_scr[HIST_ROWS:HIST_ROWS + tm, :] = u
    d = _pool_means(e_scr, tm, i * tm)
    for g_i in range(len(POOL_WINDOWS)):
        c0 = g_i * POOL_GROUP
        y = jnp.dot(d[g_i].astype(BF16), poolw_ref[g_i], preferred_element_type=F32)
        b_ref[:, c0:c0 + POOL_GROUP] = (y * pscale_ref[:, c0:c0 + POOL_GROUP]).astype(BF16)
    tail = e_scr[tm:tm + HIST_ROWS, :]
    e_scr[0:HIST_ROWS, :] = tail
    hist_ref[...] = tail


def _prompt_in(x, ln, wqkvT, wrest, qn_col, kn_col, poolw, pscale, tm=512):
    bsz, seq, _ = x.shape
    n_blocks = seq // MOBA_BLOCK
    grid = (bsz, seq // tm)
    hT = lambda dt: jax.ShapeDtypeStruct((bsz, N_HEADS, HEAD_DIM, seq), dt)
    hT_spec = pl.BlockSpec((None, N_HEADS, HEAD_DIM, tm), lambda b, i: (b, 0, 0, i))
    row_spec = lambda w: pl.BlockSpec((None, tm, w), lambda b, i: (b, i, 0))
    aug = HEAD_DIM + n_blocks
    out_shape = (
        hT(F32), hT(F32),
        jax.ShapeDtypeStruct((bsz, N_HEADS, aug, seq), BF16),
        jax.ShapeDtypeStruct((bsz, N_HEADS, seq, aug), BF16),
        jax.ShapeDtypeStruct((bsz, N_HEADS, V_AUG, seq), BF16),
        jax.ShapeDtypeStruct((bsz, seq, POOL_WIDTH), BF16),
        jax.ShapeDtypeStruct((bsz, seq, D_MODEL), BF16),
        jax.ShapeDtypeStruct((bsz, seq, D_MODEL), BF16),
        jax.ShapeDtypeStruct((bsz, HIST_ROWS, POOL_WIDTH), F32),
    )
    out_specs = (
        hT_spec, hT_spec,
        pl.BlockSpec((None, N_HEADS, aug, tm), lambda b, i: (b, 0, 0, i)),
        pl.BlockSpec((None, N_HEADS, tm, aug), lambda b, i: (b, 0, i, 0)),
        pl.BlockSpec((None, N_HEADS, V_AUG, tm), lambda b, i: (b, 0, 0, i)),
        row_spec(POOL_WIDTH), row_spec(D_MODEL), row_spec(D_MODEL),
        pl.BlockSpec((None, HIST_ROWS, POOL_WIDTH), lambda b, i: (b, 0, 0)),
    )
    in_specs = [
        row_spec(D_MODEL),
        _const_spec(ln.shape), _const_spec(wqkvT.shape), _const_spec(wrest.shape),
        _const_spec(qn_col.shape), _const_spec(kn_col.shape),
        _const_spec(poolw.shape), _const_spec(pscale.shape),
    ]
    return pl.pallas_call(
        functools.partial(_prompt_in_kernel, tm=tm, n_blocks=n_blocks),
        grid=grid, in_specs=in_specs, out_specs=out_specs, out_shape=out_shape,
        scratch_shapes=[pltpu.VMEM((HIST_ROWS + tm, POOL_WIDTH), F32),
                        pltpu.VMEM((N_HEADS, HEAD_DIM, LANES), F32)],
        compiler_params=pltpu.CompilerParams(
            dimension_semantics=("arbitrary", "arbitrary"), vmem_limit_bytes=VMEM_LIMIT),
        name="prompt_in",
    )(x, ln, wqkvT, wrest, qn_col, kn_col, poolw, pscale)


_ATTN_UNROLL = 10


def _prompt_attn_kernel(qx_ref, qy_ref, ka_ref, vT_ref, ox_ref, oy_ref, q_scr, m_scr, acc_scr, s_scr, cm_scr,
                        *, heads, n_tiles):
    p = pl.program_id(1)
    own = (p, n_tiles - 1 - p)
    blk = MOBA_BLOCK
    tri = (lax.broadcasted_iota(jnp.int32, (blk, 1), 0) <= lax.broadcasted_iota(jnp.int32, (1, blk), 1))
    q_scr[0] = qx_ref[...]
    q_scr[1] = qy_ref[...]
    m_scr[...] = jnp.full(m_scr.shape, NEG_INF, F32)
    acc_scr[...] = jnp.zeros(acc_scr.shape, F32)

    def chain_block(n):
        idx = n - 2
        c = jnp.where(n == 1, 1, (idx >= own[0]).astype(jnp.int32))
        j = jnp.where(n == 1, own[1], idx - c * own[0])
        return c, j

    def scores(c, j, slot, diag, g):
        r0 = pl.multiple_of(j * blk, blk)
        s = jnp.dot(ka_ref[g, pl.ds(r0, blk), :], q_scr[c, g], preferred_element_type=F32)
        if diag:
            s = jnp.where(tri, s, NEG_INF)
        s_scr[slot, g] = s
        cm_scr[slot, g] = jnp.max(s, axis=0, keepdims=True)

    def softmax_pv(c, j, slot, g):
        r0 = pl.multiple_of(j * blk, blk)
        m = m_scr[c, g]
        m_new = jnp.maximum(m, cm_scr[slot, g])
        alpha = jnp.exp2(m - m_new)
        pr = jnp.exp2((s_scr[slot, g] - m_new).astype(BF16))
        m_scr[c, g] = m_new
        pv = jnp.dot(vT_ref[g, :, pl.ds(r0, blk)], pr, preferred_element_type=F32)
        acc_scr[c, g] = alpha * acc_scr[c, g] + pv

    def step(nxt, nxt_slot, nxt_diag, cur, cur_slot):
        for g in range(heads):
            if nxt is not None:
                scores(*nxt, nxt_slot, nxt_diag, g)
            if cur is not None:
                softmax_pv(*cur, cur_slot, g)

    step((0, own[0]), 0, True, None, None)
    step((1, own[1]), 1, True, (0, own[0]), 0)

    def trip(i, carry):
        for u in range(_ATTN_UNROLL):
            n = 1 + _ATTN_UNROLL * i + u
            step(chain_block(n + 1), u % 2, False, chain_block(n), (u + 1) % 2)
        return carry

    lax.fori_loop(0, (n_tiles - 2) // _ATTN_UNROLL, trip, 0)
    last = n_tiles - 1
    step(chain_block(last + 1), (last + 1) % 2, False, chain_block(last), last % 2)
    step(None, None, False, chain_block(last + 1), (last + 1) % 2)
    for o_ref, c in ((ox_ref, 0), (oy_ref, 1)):
        for g in range(heads):
            o_ref[g] = (acc_scr[c, g, 0:HEAD_DIM] / acc_scr[c, g, HEAD_DIM:HEAD_DIM + 1]).astype(o_ref.dtype)


def _prompt_attn(qa, ka, vT, heads=N_HEADS):
    bsz, _, aug, seq = qa.shape
    blk = MOBA_BLOCK
    n_tiles = seq // blk
    half = n_tiles // 2
    assert heads == N_HEADS and n_tiles % 2 == 0 and (n_tiles - 2) % _ATTN_UNROLL == 0 and _ATTN_UNROLL % 2 == 0
    grid = (bsz, half)
    resident = lambda r, c: pl.BlockSpec((None, heads, r, c), lambda b, p: (b, 0, 0, 0),
                                         pipeline_mode=pl.Buffered(1))
    q_spec = lambda f: pl.BlockSpec((None, heads, aug, blk), lambda b, p: (b, 0, 0, f(p)))
    o_spec = lambda f: pl.BlockSpec((None, heads, HEAD_DIM, blk), lambda b, p: (b, 0, 0, f(p)))
    o_shape = jax.ShapeDtypeStruct((bsz, N_HEADS, HEAD_DIM, seq // 2), BF16)
    return pl.pallas_call(
        functools.partial(_prompt_attn_kernel, heads=heads, n_tiles=n_tiles),
        grid=grid,
        in_specs=[q_spec(lambda p: p), q_spec(lambda p: n_tiles - 1 - p),
                  resident(seq, aug), resident(V_AUG, seq)],
        out_specs=(o_spec(lambda p: p), o_spec(lambda p: half - 1 - p)),
        out_shape=(o_shape, o_shape),
        scratch_shapes=[pltpu.VMEM((2, heads, aug, blk), BF16),
                        pltpu.VMEM((2, heads, 1, blk), F32),
                        pltpu.VMEM((2, heads, V_AUG, blk), F32), pltpu.VMEM((2, heads, blk, blk), F32),
                        pltpu.VMEM((2, heads, 1, blk), F32)],
        compiler_params=pltpu.CompilerParams(
            dimension_semantics=("arbitrary", "arbitrary"), vmem_limit_bytes=VMEM_LIMIT),
        name="prompt_attn",
    )(qa, qa, ka, vT)


_GATE_BUFS = 16
_GATE_GROUP = 8


def _seq_column(t, seq_i):
    lane = lax.broadcasted_iota(jnp.int32, (1, 1, t.shape[2]), 2)
    return jnp.sum(jnp.where(lane == seq_i, t, 0.0), axis=2, keepdims=True)


def _top_k_blocks(s_scr, sel_ref, n_blocks):
    lane = lax.broadcasted_iota(jnp.int32, (1, LANES), 1)
    gate = jnp.zeros((N_HEADS, LANES), F32)
    for j in range(n_blocks):
        ssum = jnp.sum(s_scr[j], axis=1, keepdims=True) * (1.0 / MOBA_BLOCK)
        gate = jnp.where(lane == j, ssum, gate)
    g = jnp.where(lane < n_blocks, gate, LOWEST)
    sel = jnp.zeros((N_HEADS, LANES), jnp.int32)
    for r in range(MOBA_TOP_K):
        m = jnp.max(g, axis=1, keepdims=True)
        idx = jnp.min(jnp.where(g == m, lane, LANES), axis=1, keepdims=True)
        sel = jnp.where(lane == r, idx, sel)
        g = jnp.where(lane == idx, LOWEST, g)
    sel_ref[...] = sel


def _tail_kernel(*refs, ff_chunk, n_pages, half_tiles):
    if n_pages is None:
        (x_ref, alo_ref, ahi_ref, b_ref, sga_ref, sgb_ref, p_ref, wao_ref, wpo_ref, wo_ref, lnm_ref,
         wup_ref, wdn_ref, lnp_ref, wpg_ref, wpp_ref, y_ref) = refs
    else:
        (pt_ref, x_ref, alo_ref, ahi_ref, b_ref, sga_ref, sgb_ref, p_ref, wao_ref, wpo_ref, wo_ref, lnm_ref,
         wup_ref, wdn_ref, lnp_ref, wpg_ref, wpp_ref, q_ref, kc_ref, y_ref, sel_ref,
         kbuf, sem, s_scr, qb_scr) = refs
        seq_i = pl.program_id(0) * pl.num_programs(1) + pl.program_id(1)
        n_seq = pl.num_programs(0) * pl.num_programs(1)
        n_blocks = n_pages // PAGES_PER_BLOCK

        def page_copy(s_i, blk_i, slot, pg):
            page = pt_ref[s_i * n_pages + blk_i * PAGES_PER_BLOCK + pg]
            return pltpu.make_async_copy(kc_ref.at[page], kbuf.at[slot, pg], sem.at[slot, pg])

        def start(s_i, blk_i, slot):
            for pg in range(PAGES_PER_BLOCK):
                page_copy(s_i, blk_i, slot, pg).start()

        def wait_group(k):
            for blk_i in range(k * _GATE_GROUP, (k + 1) * _GATE_GROUP):
                for pg in range(PAGES_PER_BLOCK):
                    page_copy(seq_i, blk_i, blk_i % _GATE_BUFS, pg).wait()

        def score_group(k):
            tot = None
            for h in range(N_HEADS):
                qh = qb_scr[h]
                for blk_i in range(k * _GATE_GROUP, (k + 1) * _GATE_GROUP):
                    slot = blk_i % _GATE_BUFS
                    ksum = kbuf[slot, 0, h]
                    for pg in range(1, PAGES_PER_BLOCK):
                        ksum = ksum + kbuf[slot, pg, h]
                    row = jnp.sum(ksum * qh, axis=0, keepdims=True)
                    s_scr[blk_i, h:h + 1, :] = row
                    tot = row if tot is None else tot + row
            bits = pltpu.bitcast(tot, jnp.uint32)
            return pltpu.bitcast(lax.shift_right_logical(lax.shift_right_logical(bits, jnp.uint32(16)),
                                                         jnp.uint32(16)), F32)

        def refill_group(k):
            blks = range(k * _GATE_GROUP, (k + 1) * _GATE_GROUP)
            for blk_i in blks:
                if blk_i + _GATE_BUFS < n_blocks:
                    start(seq_i, blk_i + _GATE_BUFS, blk_i % _GATE_BUFS)
            wrap = [blk_i for blk_i in blks if blk_i + _GATE_BUFS >= n_blocks]
            if wrap:
                @pl.when(seq_i + 1 < n_seq)
                def _():
                    for blk_i in wrap:
                        start(seq_i + 1, blk_i + _GATE_BUFS - n_blocks, blk_i % _GATE_BUFS)

        @pl.when(seq_i == 0)
        def _():
            for j in range(_GATE_BUFS):
                start(0, j, j)

        qb_scr[...] = jnp.broadcast_to(_seq_column(q_ref[0], seq_i), qb_scr.shape)

    aT = jnp.where(pl.program_id(1) < half_tiles, alo_ref[...], ahi_ref[...])
    a_proj = lax.dot_general(aT, wao_ref[...], _TN, preferred_element_type=F32)
    b_proj = jnp.dot(b_ref[...], wpo_ref[...], preferred_element_type=F32)
    m = sga_ref[...].astype(F32) * a_proj + sgb_ref[...].astype(F32) * b_proj
    x1 = x_ref[...] + jnp.dot(m.astype(BF16), wo_ref[...], preferred_element_type=F32)
    xn = _rms_rows(x1, lnm_ref[...]).astype(BF16)
    acc = x1
    for c in range(D_FF // ff_chunk):
        if n_pages is not None:
            wait_group(c)
        hid = jnp.maximum(jnp.dot(xn, wup_ref[:, c * ff_chunk:(c + 1) * ff_chunk],
                                  preferred_element_type=F32), 0.0)
        if n_pages is not None:
            zero = score_group(c)
            hid = hid + jnp.concatenate([zero] * (ff_chunk // LANES), axis=1)
        acc = acc + jnp.dot((hid * hid).astype(BF16), wdn_ref[c * ff_chunk:(c + 1) * ff_chunk, :],
                            preferred_element_type=F32)
        if n_pages is not None:
            refill_group(c)
    x2 = acc
    if n_pages is not None:
        _top_k_blocks(s_scr, sel_ref, n_blocks)
    xg = _rms_rows(x2, lnp_ref[...]).astype(BF16)
    gate = _sigmoid(jnp.dot(xg, wpg_ref[...], preferred_element_type=F32))
    emb = jnp.dot(p_ref[...].astype(BF16), wpp_ref[...], preferred_element_type=F32)
    y_ref[...] = x2 + gate * emb


def _tail(x, a_lo, a_hi, b, sga, sgb, p, wao, wpo, wo, lnm, wup, wdn, lnp, wpg, wpp, tm, gate_inputs=None):
    bsz, seq, _ = x.shape
    grid = (bsz, seq // tm)
    n_i = seq // tm
    half_tiles = a_lo.shape[2] // tm
    row_spec = lambda w: pl.BlockSpec((None, tm, w), lambda b_, i, *_: (b_, i, 0))
    in_specs = [
        row_spec(D_MODEL),
        pl.BlockSpec((None, ATTN_WIDTH, tm), lambda b_, i, *_: (b_, 0, jnp.minimum(i, half_tiles - 1))),
        pl.BlockSpec((None, ATTN_WIDTH, tm), lambda b_, i, *_: (b_, 0, jnp.maximum(i - half_tiles, 0))),
        row_spec(POOL_WIDTH), row_spec(D_MODEL), row_spec(D_MODEL), row_spec(PLE_DIM),
    ] + [_const_spec(w.shape) for w in (wao, wpo, wo, lnm, wup, wdn, lnp, wpg, wpp)]
    y_shape = jax.ShapeDtypeStruct((bsz, seq, D_MODEL), F32)
    params = pltpu.CompilerParams(dimension_semantics=("arbitrary", "arbitrary"), vmem_limit_bytes=VMEM_LIMIT)
    args = (x, a_lo, a_hi, b, sga, sgb, p, wao, wpo, wo, lnm, wup, wdn, lnp, wpg, wpp)
    if gate_inputs is None:
        return pl.pallas_call(
            functools.partial(_tail_kernel, ff_chunk=D_MODEL, n_pages=None, half_tiles=half_tiles),
            grid=grid, in_specs=in_specs, out_specs=row_spec(D_MODEL), out_shape=y_shape,
            compiler_params=params, name="tail",
        )(*args)
    pt_flat, qkvT, cache_kT, n_pages = gate_inputs
    n_seq = qkvT.shape[3]
    n_blocks = n_pages // PAGES_PER_BLOCK
    n_groups = n_blocks // _GATE_GROUP
    assert bsz * n_i == n_seq and n_blocks % _GATE_BUFS == 0 and _GATE_BUFS % _GATE_GROUP == 0
    assert D_FF % n_groups == 0
    seq_spec = lambda shape: pl.BlockSpec((None,) + shape, lambda b_, i, *_: (b_ * n_i + i,) + (0,) * len(shape))
    grid_spec = pltpu.PrefetchScalarGridSpec(
        num_scalar_prefetch=1, grid=grid,
        in_specs=in_specs + [_const_spec(qkvT.shape), pl.BlockSpec(memory_space=pl.ANY)],
        out_specs=(row_spec(D_MODEL), seq_spec((N_HEADS, LANES))),
        scratch_shapes=[pltpu.VMEM((_GATE_BUFS, PAGES_PER_BLOCK, N_HEADS, HEAD_DIM, PAGE_SIZE), F32),
                        pltpu.SemaphoreType.DMA((_GATE_BUFS, PAGES_PER_BLOCK)),
                        pltpu.VMEM((n_blocks, N_HEADS, LANES), F32),
                        pltpu.VMEM((N_HEADS, HEAD_DIM, LANES), F32)],
    )
    return pl.pallas_call(
        functools.partial(_tail_kernel, ff_chunk=D_FF // n_groups, n_pages=n_pages, half_tiles=half_tiles),
        grid_spec=grid_spec,
        out_shape=(y_shape, jax.ShapeDtypeStruct((n_seq, N_HEADS, LANES), jnp.int32)),
        compiler_params=params, name="tail_gate",
    )(pt_flat, *args, qkvT, cache_kT)


def _sample_in_kernel(x_ref, ln_ref, wqkvT_ref, wrest_ref, kn_ref, qn_col_ref, kn_col_ref, hist_ref,
                      poolw_ref, pscale_ref,
                      k_ref, v_ref, qkvT_ref, b_ref, sga_ref, sgb_ref, hist_out_ref, *, past_len):
    n = x_ref.shape[0]
    xn = _rms_rows(x_ref[...], ln_ref[...]).astype(BF16)
    k_row = lax.dot_general(xn, wqkvT_ref[ATTN_WIDTH:2 * ATTN_WIDTH, :], _NT, preferred_element_type=F32)
    for h in range(N_HEADS):
        c0 = h * HEAD_DIM
        k_ref[:, c0:c0 + HEAD_DIM] = _rms_rows(k_row[:, c0:c0 + HEAD_DIM], kn_ref[...])
    v_ref[...] = lax.dot_general(xn, wqkvT_ref[2 * ATTN_WIDTH:3 * ATTN_WIDTH, :], _NT, preferred_element_type=F32)
    z = jnp.dot(xn, wrest_ref[...], preferred_element_type=F32)
    zT = lax.dot_general(wqkvT_ref[...], xn, _NT, preferred_element_type=F32).reshape(3, N_HEADS, HEAD_DIM, n)

    def head_norm(t, g_col):
        ms = jnp.mean(t * t, axis=1, keepdims=True)
        return t * lax.rsqrt(ms + EPS) * g_col[None]

    qkvT_ref[0] = head_norm(zT[0], qn_col_ref[...])
    qkvT_ref[1] = head_norm(zT[1], kn_col_ref[...])
    qkvT_ref[2] = zT[2]
    u = z[:, 0:POOL_WIDTH]
    for g_i, w in enumerate(POOL_WINDOWS):
        c0 = g_i * POOL_GROUP
        cur = u[:, c0:c0 + POOL_GROUP]
        acc = cur
        for k in range(1, w):
            acc = acc + hist_ref[POOL_HIST - k, :, c0:c0 + POOL_GROUP]
        cnt = float(min(w, past_len + 1))
        d = acc / cnt - cur
        y = jnp.dot(d.astype(BF16), poolw_ref[g_i], preferred_element_type=F32)
        b_ref[:, c0:c0 + POOL_GROUP] = (y * pscale_ref[:, c0:c0 + POOL_GROUP]).astype(BF16)
    for t in range(POOL_HIST - 1):
        hist_out_ref[t] = hist_ref[t + 1]
    hist_out_ref[POOL_HIST - 1] = u
    g0 = POOL_WIDTH
    sga_ref[...] = _sigmoid(z[:, g0:g0 + D_MODEL]).astype(BF16)
    sgb_ref[...] = _sigmoid(z[:, g0 + D_MODEL:g0 + 2 * D_MODEL]).astype(BF16)


def _sample_in(x, ln, wqkvT, wrest, kn_row, qn_col, kn_col, hist_t, poolw, pscale, past_len):
    n = x.shape[0]
    sds = jax.ShapeDtypeStruct
    out_shape = (sds((n, ATTN_WIDTH), F32), sds((n, ATTN_WIDTH), F32), sds((3, N_HEADS, HEAD_DIM, n), F32),
                 sds((n, POOL_WIDTH), BF16), sds((n, D_MODEL), BF16), sds((n, D_MODEL), BF16),
                 sds(hist_t.shape, F32))
    return pl.pallas_call(
        functools.partial(_sample_in_kernel, past_len=past_len),
        out_shape=out_shape,
        compiler_params=pltpu.CompilerParams(vmem_limit_bytes=VMEM_LIMIT),
        name="sample_in",
    )(x, ln, wqkvT, wrest, kn_row, qn_col, kn_col, hist_t, poolw, pscale)


def _sample_attn_kernel(pt_ref, sel_ref, qkvT_ref, kc_ref, vc_ref, o_ref,
                        kbuf, vbuf, sem, *, n_pages):
    b = pl.program_id(0)
    n_b = pl.num_programs(0)
    n_slots = MOBA_TOP_K * PAGES_PER_BLOCK

    def copies(seq_i, buf_i):
        out = []
        for h in range(N_HEADS):
            for r in range(MOBA_TOP_K):
                blk = sel_ref[(seq_i * N_HEADS + h) * MOBA_TOP_K + r]
                for pg in range(PAGES_PER_BLOCK):
                    page = pt_ref[seq_i * n_pages + blk * PAGES_PER_BLOCK + pg]
                    s_i = r * PAGES_PER_BLOCK + pg
                    out.append(pltpu.make_async_copy(kc_ref.at[page, h], kbuf.at[buf_i, h, s_i], sem.at[buf_i, 0]))
                    out.append(pltpu.make_async_copy(vc_ref.at[page, h], vbuf.at[buf_i, h, s_i], sem.at[buf_i, 1]))
        return out

    @pl.when(b == 0)
    def _():
        for c in copies(0, 0):
            c.start()

    cur = b % 2

    @pl.when(b + 1 < n_b)
    def _():
        for c in copies(b + 1, 1 - cur):
            c.start()

    for c in copies(b, cur):
        c.wait()

    q_col, kn_col, vn_col = (_seq_column(qkvT_ref[i], b) for i in range(3))
    seq_lane = lax.broadcasted_iota(jnp.int32, (1, o_ref.shape[2]), 1)

    @pl.when(b == 0)
    def _():
        o_ref[...] = jnp.zeros(o_ref.shape, F32)

    for h in range(N_HEADS):
        qh = q_col[h]
        kb = kbuf[cur, h]
        vb = vbuf[cur, h]
        s = jnp.sum(kb * qh[None], axis=1) * ATTN_SCALE
        s_new = jnp.sum(qh * kn_col[h], axis=0, keepdims=True) * ATTN_SCALE
        m = jnp.maximum(jnp.max(jnp.max(s, axis=1, keepdims=True), axis=0, keepdims=True), s_new)
        p = jnp.exp(s - m)
        p_new = jnp.exp(s_new - m)
        l = jnp.sum(jnp.sum(p, axis=1, keepdims=True), axis=0, keepdims=True) + p_new
        pv = vb[0] * p[0:1]
        for s_i in range(1, n_slots):
            pv = pv + vb[s_i] * p[s_i:s_i + 1]
        o = jnp.sum(pv, axis=1, keepdims=True) + p_new * vn_col[h]
        o_ref[h] = jnp.where(seq_lane == b, o / l, o_ref[h])


def _sample_attn(page_table_flat, sel_flat, qkvT, cache_kT, cache_vT, n_seq, n_pages):
    n_slots = MOBA_TOP_K * PAGES_PER_BLOCK
    buf = pltpu.VMEM((2, N_HEADS, n_slots, HEAD_DIM, PAGE_SIZE), F32)
    grid_spec = pltpu.PrefetchScalarGridSpec(
        num_scalar_prefetch=2,
        grid=(n_seq,),
        in_specs=[_const_spec(qkvT.shape), pl.BlockSpec(memory_space=pl.ANY), pl.BlockSpec(memory_space=pl.ANY)],
        out_specs=pl.BlockSpec((N_HEADS, HEAD_DIM, n_seq), lambda b, pt, sel: (0, 0, 0)),
        scratch_shapes=[buf, buf, pltpu.SemaphoreType.DMA((2, 2))],
    )
    return pl.pallas_call(
        functools.partial(_sample_attn_kernel, n_pages=n_pages),
        grid_spec=grid_spec,
        out_shape=jax.ShapeDtypeStruct((N_HEADS, HEAD_DIM, n_seq), F32),
        compiler_params=pltpu.CompilerParams(
            dimension_semantics=("arbitrary",), vmem_limit_bytes=VMEM_LIMIT),
        name="sample_attn",
    )(page_table_flat, sel_flat, qkvT, cache_kT, cache_vT)


def kernel(x_prompt, x_sample, cache_k, cache_v, state_pool, page_table, p_prompt, p_sample, ln_mix, w_in,
           q_norm, k_norm, pool_w, pool_scale, w_attn_out, w_pool_out, w_out, ln_mlp, w_up, w_down, ln_ple,
           w_ple_gate, w_ple_proj):
    depth = w_in.shape[0]
    assert depth == 1
    n_seq, n_pages = page_table.shape
    past_len = n_pages * PAGE_SIZE
    assert x_sample.shape[1] == 1 and past_len % MOBA_BLOCK == 0 and past_len // MOBA_BLOCK >= MOBA_TOP_K
    assert (past_len // MOBA_BLOCK) % _GATE_BUFS == 0 and (x_prompt.shape[1] // MOBA_BLOCK) % 16 == 0
    bsz, seq, _ = x_prompt.shape
    l = 0

    wqkvT = w_in[l][:, :3 * ATTN_WIDTH].T.astype(BF16)
    wrest = w_in[l][:, 3 * ATTN_WIDTH:].astype(BF16)
    poolw = pool_w[l].astype(BF16)
    tail_w = (w_attn_out[l].astype(BF16), w_pool_out[l].astype(BF16), w_out[l].astype(BF16), ln_mlp,
              w_up[l].astype(BF16), w_down[l].astype(BF16), ln_ple, w_ple_gate[l].astype(BF16),
              w_ple_proj[l].astype(BF16))

    xs = x_sample.reshape(n_seq, D_MODEL)
    hist_t = state_pool[l].transpose(1, 0, 2)
    qn_col, kn_col = q_norm.reshape(HEAD_DIM, 1), k_norm.reshape(HEAD_DIM, 1)
    k_s, v_s, qkvT_s, b_s, sga_s, sgb_s, hist_s = _sample_in(
        xs, ln_mix, wqkvT, wrest, k_norm, qn_col, kn_col, hist_t, poolw, pool_scale, past_len)
    cache_kT = cache_k[l].transpose(0, 2, 3, 1)
    cache_vT = cache_v[l].transpose(0, 2, 3, 1)
    pt_flat = page_table.reshape(-1)

    (kT, vT, qa, ka, vTb, b_p, sga, sgb, hist_p) = _prompt_in(
        x_prompt, ln_mix, wqkvT, wrest, qn_col, kn_col, poolw, pool_scale)
    a_lo, a_hi = (a.reshape(bsz, ATTN_WIDTH, seq // 2) for a in _prompt_attn(qa, ka, vTb))
    y_prompt, sel = _tail(x_prompt, a_lo, a_hi, b_p, sga, sgb, p_prompt[l], *tail_w, tm=512,
                          gate_inputs=(pt_flat, qkvT_s, cache_kT, n_pages))
    k_prompt = kT.transpose(0, 3, 1, 2)[None]
    v_prompt = vT.transpose(0, 3, 1, 2)[None]
    pool_prompt = hist_p[:, HIST_ROWS - POOL_HIST:][None]

    sel_flat = sel[:, :, :MOBA_TOP_K].reshape(-1)
    a_s = _sample_attn(pt_flat, sel_flat, qkvT_s, cache_kT, cache_vT, n_seq, n_pages)
    aT_s = a_s.reshape(1, ATTN_WIDTH, n_seq).astype(BF16)
    y_sample = _tail(xs[None], aT_s, aT_s, b_s[None], sga_s[None], sgb_s[None], p_sample[l].reshape(1, n_seq, PLE_DIM),
                     *tail_w, tm=n_seq)
    y_sample = y_sample.reshape(n_seq, 1, D_MODEL)
    k_sample = k_s.reshape(1, n_seq, 1, N_HEADS, HEAD_DIM)
    v_sample = v_s.reshape(1, n_seq, 1, N_HEADS, HEAD_DIM)
    pool_sample = hist_s.transpose(1, 0, 2)[None]
    return (y_prompt, y_sample, k_prompt, v_prompt, pool_prompt, k_sample, v_sample, pool_sample)
```

```python
import functools

import jax
import jax.numpy as jnp
from jax import lax
from jax.experimental import pallas as pl
from jax.experimental.pallas import tpu as pltpu

D_MODEL = 1024
N_HEADS = 8
HEAD_DIM = 64
ATTN_WIDTH = N_HEADS * HEAD_DIM
MOBA_BLOCK = 256
MOBA_TOP_K = 3
PAGE_SIZE = 128
PAGES_PER_BLOCK = MOBA_BLOCK // PAGE_SIZE
ATTN_SCALE = HEAD_DIM ** -0.5
POOL_WINDOWS = (2, 4, 8, 16)
POOL_GROUP = 128
POOL_WIDTH = 512
POOL_HIST = 15
HIST_ROWS = 16
V_AUG = HEAD_DIM + 16
D_FF = 4 * D_MODEL
PLE_DIM = 256
EPS = 1e-6
NEG_INF = -1e30
LOWEST = -3e38
LOG2E = 1.4426950408889634

LANES = 128
VMEM_LIMIT = 60 * 1024 * 1024

F32 = jnp.float32
BF16 = jnp.bfloat16

_NT = (((1,), (1,)), ((), ()))
_TN = (((0,), (0,)), ((), ()))


def _rms_rows(x, g):
    ms = jnp.mean(x * x, axis=-1, keepdims=True)
    return x * lax.rsqrt(ms + EPS) * g


def _sigmoid(x):
    return 0.5 * jnp.tanh(0.5 * x) + 0.5


def _split_bf16(a):
    hi = a.astype(BF16)
    lo = (a - hi.astype(F32)).astype(BF16)
    return hi, lo


def _const_spec(shape):
    zeros = (0,) * len(shape)
    return pl.BlockSpec(shape, lambda *_: zeros, pipeline_mode=pl.Buffered(1))


def _pool_means(e_ref, tm, pos0):
    pos = pos0 + lax.broadcasted_iota(jnp.int32, (tm, 1), 0)
    out = []
    for g, w in enumerate(POOL_WINDOWS):
        assert w & (w - 1) == 0 and w - 1 <= HIST_ROWS
        c0 = g * POOL_GROUP
        acc = e_ref[:, c0:c0 + POOL_GROUP]
        shift = 1
        while shift < w:
            acc = acc + pltpu.roll(acc, shift, 0)
            shift *= 2
        cur = e_ref[HIST_ROWS:HIST_ROWS + tm, c0:c0 + POOL_GROUP]
        inv_cnt = 1.0 / jnp.minimum(w, pos + 1).astype(F32)
        out.append(acc[HIST_ROWS:] * inv_cnt - cur)
    return out


def _prompt_in_kernel(x_ref, ln_ref, wqkvT_ref, wrest_ref, qn_ref, kn_ref, poolw_ref, pscale_ref,
                      kT_ref, vT_ref, qa_ref, ka_ref, vTb_ref, b_ref, sga_ref, sgb_ref,
                      hist_ref, e_scr, km_scr, *, tm, n_blocks):
    i = pl.program_id(1)
    n_i = pl.num_programs(1)
    blocks_per_tile = tm // MOBA_BLOCK

    @pl.when(i == 0)
    def _():
        e_scr[0:HIST_ROWS, :] = jnp.zeros((HIST_ROWS, POOL_WIDTH), F32)
        km_scr[...] = jnp.zeros(km_scr.shape, F32)

    xn = _rms_rows(x_ref[...], ln_ref[...]).astype(BF16)

    def proj_t(r0):
        t = lax.dot_general(wqkvT_ref[r0:r0 + ATTN_WIDTH, :], xn, _NT, preferred_element_type=F32)
        return t.reshape(N_HEADS, HEAD_DIM, tm)

    def head_norm(t, g_col):
        ms = jnp.mean(t * t, axis=1, keepdims=True)
        return t * lax.rsqrt(ms + EPS) * g_col[None]

    q = head_norm(proj_t(0), qn_ref[...])
    k = head_norm(proj_t(ATTN_WIDTH), kn_ref[...])
    v = proj_t(2 * ATTN_WIDTH)
    kT_ref[...] = k
    vT_ref[...] = v
    vTb_ref[:, 0:HEAD_DIM, :] = v.astype(BF16)
    ones_row = lax.broadcasted_iota(jnp.int32, (N_HEADS, V_AUG - HEAD_DIM, tm), 1) == 0
    vTb_ref[:, HEAD_DIM:V_AUG, :] = jnp.where(ones_row, 1.0, 0.0).astype(BF16)
    qa_ref[:, 0:HEAD_DIM, :] = (q * (ATTN_SCALE * LOG2E)).astype(BF16)
    key_blk = (i * tm + lax.broadcasted_iota(jnp.int32, (tm, 1), 0)) // MOBA_BLOCK
    onehot = jnp.where(key_blk == lax.broadcasted_iota(jnp.int32, (1, n_blocks), 1), 1.0, 0.0).astype(BF16)
    for h in range(N_HEADS):
        ka_ref[h, :, 0:HEAD_DIM] = k[h].T.astype(BF16)
        ka_ref[h, :, HEAD_DIM:HEAD_DIM + n_blocks] = onehot

    lane = lax.broadcasted_iota(jnp.int32, (1, 1, LANES), 2)
    km = km_scr[...]
    for h in range(blocks_per_tile):
        ks = jnp.sum(k[:, :, h * MOBA_BLOCK:(h + 1) * MOBA_BLOCK], axis=2, keepdims=True) * (1.0 / MOBA_BLOCK)
        km = jnp.where(lane == i * blocks_per_tile + h, ks, km)
    km_scr[...] = km

    pos = i * tm + lax.broadcasted_iota(jnp.int32, (1, tm), 1)
    own = pos // MOBA_BLOCK
    blk = lax.broadcasted_iota(jnp.int32, (n_blocks, 1), 0)
    for h in range(N_HEADS):
        km_hi, km_lo = _split_bf16(km[h][:, 0:n_blocks])
        q_hi, q_lo = _split_bf16(q[h])
        gate = (lax.dot_general(km_hi, q_hi, _TN, preferred_element_type=F32)
                + lax.dot_general(km_hi, q_lo, _TN, preferred_element_type=F32)
                + lax.dot_general(km_lo, q_hi, _TN, preferred_element_type=F32))
        g = jnp.where(blk < own, gate, LOWEST)
        bias = jnp.where(blk == own, 0.0, NEG_INF)
        for _ in range(MOBA_TOP_K):
            m = jnp.max(g, axis=0, keepdims=True)
            idx = jnp.min(jnp.where(g == m, blk, n_blocks), axis=0, keepdims=True)
            idx = jnp.where(m > LOWEST, idx, n_blocks)
            pick = blk == idx
            bias = jnp.where(pick, 0.0, bias)
            g = jnp.where(pick, LOWEST, g)
        qa_ref[h, HEAD_DIM:HEAD_DIM + n_blocks, :] = bias.astype(BF16)
        w = 2 * D_MODEL // N_HEADS
        gcol = jnp.dot(xn, wrest_ref[:, POOL_WIDTH + h * w:POOL_WIDTH + (h + 1) * w], preferred_element_type=F32)
        sg_ref = sga_ref if h < N_HEADS // 2 else sgb_ref
        c0 = (h % (N_HEADS // 2)) * w
        sg_ref[:, c0:c0 + w] = _sigmoid(gcol).astype(BF16)

    u = jnp.dot(xn, wrest_ref[:, 0:POOL_WIDTH], preferred_element_type=F32)
    e_scr[HIST_ROWS:HIST_ROWS + tm, :] = u
    d = _pool_means(e_scr, tm, i * tm)
    for g_i in range(len(POOL_WINDOWS)):
        c0 = g_i * POOL_GROUP
        y = jnp.dot(d[g_i].astype(BF16), poolw_ref[g_i], preferred_element_type=F32)
        b_ref[:, c0:c0 + POOL_GROUP] = (y * pscale_ref[:, c0:c0 + POOL_GROUP]).astype(BF16)
    tail = e_scr[tm:tm + HIST_ROWS, :]
    e_scr[0:HIST_ROWS, :] = tail
    hist_ref[...] = tail


def _prompt_in(x, ln, wqkvT, wrest, qn_col, kn_col, poolw, pscale, tm=512):
    bsz, seq, _ = x.shape
    n_blocks = seq // MOBA_BLOCK
    grid = (bsz, seq // tm)
    hT = lambda dt: jax.ShapeDtypeStruct((bsz, N_HEADS, HEAD_DIM, seq), dt)
    hT_spec = pl.BlockSpec((None, N_HEADS, HEAD_DIM, tm), lambda b, i: (b, 0, 0, i))
    row_spec = lambda w: pl.BlockSpec((None, tm, w), lambda b, i: (b, i, 0))
    aug = HEAD_DIM + n_blocks
    out_shape = (
        hT(F32), hT(F32),
        jax.ShapeDtypeStruct((bsz, N_HEADS, aug, seq), BF16),
        jax.ShapeDtypeStruct((bsz, N_HEADS, seq, aug), BF16),
        jax.ShapeDtypeStruct((bsz, N_HEADS, V_AUG, seq), BF16),
        jax.ShapeDtypeStruct((bsz, seq, POOL_WIDTH), BF16),
        jax.ShapeDtypeStruct((bsz, seq, D_MODEL), BF16),
        jax.ShapeDtypeStruct((bsz, seq, D_MODEL), BF16),
        jax.ShapeDtypeStruct((bsz, HIST_ROWS, POOL_WIDTH), F32),
    )
    out_specs = (
        hT_spec, hT_spec,
        pl.BlockSpec((None, N_HEADS, aug, tm), lambda b, i: (b, 0, 0, i)),
        pl.BlockSpec((None, N_HEADS, tm, aug), lambda b, i: (b, 0, i, 0)),
        pl.BlockSpec((None, N_HEADS, V_AUG, tm), lambda b, i: (b, 0, 0, i)),
        row_spec(POOL_WIDTH), row_spec(D_MODEL), row_spec(D_MODEL),
        pl.BlockSpec((None, HIST_ROWS, POOL_WIDTH), lambda b, i: (b, 0, 0)),
    )
    in_specs = [
        row_spec(D_MODEL),
        _const_spec(ln.shape), _const_spec(wqkvT.shape), _const_spec(wrest.shape),
        _const_spec(qn_col.shape), _const_spec(kn_col.shape),
        _const_spec(poolw.shape), _const_spec(pscale.shape),
    ]
    return pl.pallas_call(
        functools.partial(_prompt_in_kernel, tm=tm, n_blocks=n_blocks),
        grid=grid, in_specs=in_specs, out_specs=out_specs, out_shape=out_shape,
        scratch_shapes=[pltpu.VMEM((HIST_ROWS + tm, POOL_WIDTH), F32),
                        pltpu.VMEM((N_HEADS, HEAD_DIM, LANES), F32)],
        compiler_params=pltpu.CompilerParams(
            dimension_semantics=("arbitrary", "arbitrary"), vmem_limit_bytes=VMEM_LIMIT),
        name="prompt_in",
    )(x, ln, wqkvT, wrest, qn_col, kn_col, poolw, pscale)


_ATTN_UNROLL = 10


def _prompt_attn_kernel(qx_ref, qy_ref, ka_ref, vT_ref, ox_ref, oy_ref, q_scr, m_scr, acc_scr, s_scr, cm_scr,
                        *, heads, n_tiles):
    p = pl.program_id(1)
    own = (p, n_tiles - 1 - p)
    blk = MOBA_BLOCK
    tri = (lax.broadcasted_iota(jnp.int32, (blk, 1), 0) <= lax.broadcasted_iota(jnp.int32, (1, blk), 1))
    q_scr[0] = qx_ref[...]
    q_scr[1] = qy_ref[...]
    m_scr[...] = jnp.full(m_scr.shape, NEG_INF, F32)
    acc_scr[...] = jnp.zeros(acc_scr.shape, F32)

    def chain_block(n):
        idx = n - 2
        c = jnp.where(n == 1, 1, (idx >= own[0]).astype(jnp.int32))
        j = jnp.where(n == 1, own[1], idx - c * own[0])
        return c, j

    def scores(c, j, slot, diag, g):
        r0 = pl.multiple_of(j * blk, blk)
        s = jnp.dot(ka_ref[g, pl.ds(r0, blk), :], q_scr[c, g], preferred_element_type=F32)
        if diag:
            s = jnp.where(tri, s, NEG_INF)
        s_scr[slot, g] = s
        cm_scr[slot, g] = jnp.max(s, axis=0, keepdims=True)

    def softmax_pv(c, j, slot, g):
        r0 = pl.multiple_of(j * blk, blk)
        m = m_scr[c, g]
        m_new = jnp.maximum(m, cm_scr[slot, g])
        alpha = jnp.exp2(m - m_new)
        pr = jnp.exp2((s_scr[slot, g] - m_new).astype(BF16))
        m_scr[c, g] = m_new
        pv = jnp.dot(vT_ref[g, :, pl.ds(r0, blk)], pr, preferred_element_type=F32)
        acc_scr[c, g] = alpha * acc_scr[c, g] + pv

    def step(nxt, nxt_slot, nxt_diag, cur, cur_slot):
        for g in range(heads):
            if nxt is not None:
                scores(*nxt, nxt_slot, nxt_diag, g)
            if cur is not None:
                softmax_pv(*cur, cur_slot, g)

    step((0, own[0]), 0, True, None, None)
    step((1, own[1]), 1, True, (0, own[0]), 0)

    def trip(i, carry):
        for u in range(_ATTN_UNROLL):
            n = 1 + _ATTN_UNROLL * i + u
            step(chain_block(n + 1), u % 2, False, chain_block(n), (u + 1) % 2)
        return carry

    lax.fori_loop(0, (n_tiles - 2) // _ATTN_UNROLL, trip, 0)
    last = n_tiles - 1
    step(chain_block(last + 1), (last + 1) % 2, False, chain_block(last), last % 2)
    step(None, None, False, chain_block(last + 1), (last + 1) % 2)
    for o_ref, c in ((ox_ref, 0), (oy_ref, 1)):
        for g in range(heads):
            o_ref[g] = (acc_scr[c, g, 0:HEAD_DIM] / acc_scr[c, g, HEAD_DIM:HEAD_DIM + 1]).astype(o_ref.dtype)


def _prompt_attn(qa, ka, vT, heads=N_HEADS):
    bsz, _, aug, seq = qa.shape
    blk = MOBA_BLOCK
    n_tiles = seq // blk
    half = n_tiles // 2
    assert heads == N_HEADS and n_tiles % 2 == 0 and (n_tiles - 2) % _ATTN_UNROLL == 0 and _ATTN_UNROLL % 2 == 0
    grid = (bsz, half)
    resident = lambda r, c: pl.BlockSpec((None, heads, r, c), lambda b, p: (b, 0, 0, 0),
                                         pipeline_mode=pl.Buffered(1))
    q_spec = lambda f: pl.BlockSpec((None, heads, aug, blk), lambda b, p: (b, 0, 0, f(p)))
    o_spec = lambda f: pl.BlockSpec((None, heads, HEAD_DIM, blk), lambda b, p: (b, 0, 0, f(p)))
    o_shape = jax.ShapeDtypeStruct((bsz, N_HEADS, HEAD_DIM, seq // 2), BF16)
    return pl.pallas_call(
        functools.partial(_prompt_attn_kernel, heads=heads, n_tiles=n_tiles),
        grid=grid,
        in_specs=[q_spec(lambda p: p), q_spec(lambda p: n_tiles - 1 - p),
                  resident(seq, aug), resident(V_AUG, seq)],
        out_specs=(o_spec(lambda p: p), o_spec(lambda p: half - 1 - p)),
        out_shape=(o_shape, o_shape),
        scratch_shapes=[pltpu.VMEM((2, heads, aug, blk), BF16),
                        pltpu.VMEM((2, heads, 1, blk), F32),
                        pltpu.VMEM((2, heads, V_AUG, blk), F32), pltpu.VMEM((2, heads, blk, blk), F32),
                        pltpu.VMEM((2, heads, 1, blk), F32)],
        compiler_params=pltpu.CompilerParams(
            dimension_semantics=("arbitrary", "arbitrary"), vmem_limit_bytes=VMEM_LIMIT),
        name="prompt_attn",
    )(qa, qa, ka, vT)


_GATE_BUFS = 16
_GATE_GROUP = 8


def _seq_column(t, seq_i):
    lane = lax.broadcasted_iota(jnp.int32, (1, 1, t.shape[2]), 2)
    return jnp.sum(jnp.where(lane == seq_i, t, 0.0), axis=2, keepdims=True)


def _top_k_blocks(s_scr, sel_ref, n_blocks):
    lane = lax.broadcasted_iota(jnp.int32, (1, LANES), 1)
    gate = jnp.zeros((N_HEADS, LANES), F32)
    for j in range(n_blocks):
        ssum = jnp.sum(s_scr[j], axis=1, keepdims=True) * (1.0 / MOBA_BLOCK)
        gate = jnp.where(lane == j, ssum, gate)
    g = jnp.where(lane < n_blocks, gate, LOWEST)
    sel = jnp.zeros((N_HEADS, LANES), jnp.int32)
    for r in range(MOBA_TOP_K):
        m = jnp.max(g, axis=1, keepdims=True)
        idx = jnp.min(jnp.where(g == m, lane, LANES), axis=1, keepdims=True)
        sel = jnp.where(lane == r, idx, sel)
        g = jnp.where(lane == idx, LOWEST, g)
    sel_ref[...] = sel


def _tail_kernel(*refs, ff_chunk, n_pages, half_tiles):
    if n_pages is None:
        (x_ref, alo_ref, ahi_ref, b_ref, sga_ref, sgb_ref, p_ref, wao_ref, wpo_ref, wo_ref, lnm_ref,
         wup_ref, wdn_ref, lnp_ref, wpg_ref, wpp_ref, y_ref) = refs
    else:
        (pt_ref, x_ref, alo_ref, ahi_ref, b_ref, sga_ref, sgb_ref, p_ref, wao_ref, wpo_ref, wo_ref, lnm_ref,
         wup_ref, wdn_ref, lnp_ref, wpg_ref, wpp_ref, q_ref, kc_ref, y_ref, sel_ref,
         kbuf, sem, s_scr, qb_scr) = refs
        seq_i = pl.program_id(0) * pl.num_programs(1) + pl.program_id(1)
        n_seq = pl.num_programs(0) * pl.num_programs(1)
        n_blocks = n_pages // PAGES_PER_BLOCK

        def page_copy(s_i, blk_i, slot, pg):
            page = pt_ref[s_i * n_pages + blk_i * PAGES_PER_BLOCK + pg]
            return pltpu.make_async_copy(kc_ref.at[page], kbuf.at[slot, pg], sem.at[slot, pg])

        def start(s_i, blk_i, slot):
            for pg in range(PAGES_PER_BLOCK):
                page_copy(s_i, blk_i, slot, pg).start()

        def wait_group(k):
            for blk_i in range(k * _GATE_GROUP, (k + 1) * _GATE_GROUP):
                for pg in range(PAGES_PER_BLOCK):
                    page_copy(seq_i, blk_i, blk_i % _GATE_BUFS, pg).wait()

        def score_group(k, head_range):
            tot = None
            for h in head_range:
                qh = qb_scr[h]
                for blk_i in range(k * _GATE_GROUP, (k + 1) * _GATE_GROUP):
                    slot = blk_i % _GATE_BUFS
                    ksum = kbuf[slot, 0, h]
                    for pg in range(1, PAGES_PER_BLOCK):
                        ksum = ksum + kbuf[slot, pg, h]
                    row = jnp.sum(ksum * qh, axis=0, keepdims=True)
                    s_scr[blk_i, h:h + 1, :] = row
                    tot = row if tot is None else tot + row
            bits = pltpu.bitcast(tot, jnp.uint32)
            return pltpu.bitcast(lax.shift_right_logical(lax.shift_right_logical(bits, jnp.uint32(16)),
                                                         jnp.uint32(16)), F32)

        def refill_group(k):
            blks = range(k * _GATE_GROUP, (k + 1) * _GATE_GROUP)
            for blk_i in blks:
                if blk_i + _GATE_BUFS < n_blocks:
                    start(seq_i, blk_i + _GATE_BUFS, blk_i % _GATE_BUFS)
            wrap = [blk_i for blk_i in blks if blk_i + _GATE_BUFS >= n_blocks]
            if wrap:
                @pl.when(seq_i + 1 < n_seq)
                def _():
                    for blk_i in wrap:
                        start(seq_i + 1, blk_i + _GATE_BUFS - n_blocks, blk_i % _GATE_BUFS)

        @pl.when(seq_i == 0)
        def _():
            for j in range(_GATE_BUFS):
                start(0, j, j)

        qb_scr[...] = jnp.broadcast_to(_seq_column(q_ref[0], seq_i), qb_scr.shape)

    aT = jnp.where(pl.program_id(1) < half_tiles, alo_ref[...], ahi_ref[...])
    a_proj = lax.dot_general(aT, wao_ref[...], _TN, preferred_element_type=F32)
    b_proj = jnp.dot(b_ref[...], wpo_ref[...], preferred_element_type=F32)
    m = sga_ref[...].astype(F32) * a_proj + sgb_ref[...].astype(F32) * b_proj
    x1 = x_ref[...] + jnp.dot(m.astype(BF16), wo_ref[...], preferred_element_type=F32)
    xn = _rms_rows(x1, lnm_ref[...]).astype(BF16)
    acc = x1
    for c in range(D_FF // ff_chunk):
        xn_c = xn
        if n_pages is not None:
            wait_group(c)
            zero = score_group(c, range(N_HEADS // 2))
            half = D_MODEL // 2
            xn_c = jnp.concatenate(
                [xn[:, :half], xn[:, half:] + jnp.concatenate([zero] * (half // LANES), axis=1).astype(BF16)], axis=1)
        hid = jnp.maximum(jnp.dot(xn_c, wup_ref[:, c * ff_chunk:(c + 1) * ff_chunk],
                                  preferred_element_type=F32), 0.0)
        if n_pages is not None:
            zero = score_group(c, range(N_HEADS // 2, N_HEADS))
            hid = hid + jnp.concatenate([zero] * (ff_chunk // LANES), axis=1)
        acc = acc + jnp.dot((hid * hid).astype(BF16), wdn_ref[c * ff_chunk:(c + 1) * ff_chunk, :],
                            preferred_element_type=F32)
        if n_pages is not None:
            refill_group(c)
    x2 = acc
    if n_pages is not None:
        _top_k_blocks(s_scr, sel_ref, n_blocks)
    xg = _rms_rows(x2, lnp_ref[...]).astype(BF16)
    gate = _sigmoid(jnp.dot(xg, wpg_ref[...], preferred_element_type=F32))
    emb = jnp.dot(p_ref[...].astype(BF16), wpp_ref[...], preferred_element_type=F32)
    y_ref[...] = x2 + gate * emb


def _tail(x, a_lo, a_hi, b, sga, sgb, p, wao, wpo, wo, lnm, wup, wdn, lnp, wpg, wpp, tm, gate_inputs=None):
    bsz, seq, _ = x.shape
    grid = (bsz, seq // tm)
    n_i = seq // tm
    half_tiles = a_lo.shape[2] // tm
    row_spec = lambda w: pl.BlockSpec((None, tm, w), lambda b_, i, *_: (b_, i, 0))
    in_specs = [
        row_spec(D_MODEL),
        pl.BlockSpec((None, ATTN_WIDTH, tm), lambda b_, i, *_: (b_, 0, jnp.minimum(i, half_tiles - 1))),
        pl.BlockSpec((None, ATTN_WIDTH, tm), lambda b_, i, *_: (b_, 0, jnp.maximum(i - half_tiles, 0))),
        row_spec(POOL_WIDTH), row_spec(D_MODEL), row_spec(D_MODEL), row_spec(PLE_DIM),
    ] + [_const_spec(w.shape) for w in (wao, wpo, wo, lnm, wup, wdn, lnp, wpg, wpp)]
    y_shape = jax.ShapeDtypeStruct((bsz, seq, D_MODEL), F32)
    params = pltpu.CompilerParams(dimension_semantics=("arbitrary", "arbitrary"), vmem_limit_bytes=VMEM_LIMIT)
    args = (x, a_lo, a_hi, b, sga, sgb, p, wao, wpo, wo, lnm, wup, wdn, lnp, wpg, wpp)
    if gate_inputs is None:
        return pl.pallas_call(
            functools.partial(_tail_kernel, ff_chunk=D_MODEL, n_pages=None, half_tiles=half_tiles),
            grid=grid, in_specs=in_specs, out_specs=row_spec(D_MODEL), out_shape=y_shape,
            compiler_params=params, name="tail",
        )(*args)
    pt_flat, qkvT, cache_kT, n_pages = gate_inputs
    n_seq = qkvT.shape[3]
    n_blocks = n_pages // PAGES_PER_BLOCK
    n_groups = n_blocks // _GATE_GROUP
    assert bsz * n_i == n_seq and n_blocks % _GATE_BUFS == 0 and _GATE_BUFS % _GATE_GROUP == 0
    assert D_FF % n_groups == 0
    seq_spec = lambda shape: pl.BlockSpec((None,) + shape, lambda b_, i, *_: (b_ * n_i + i,) + (0,) * len(shape))
    grid_spec = pltpu.PrefetchScalarGridSpec(
        num_scalar_prefetch=1, grid=grid,
        in_specs=in_specs + [_const_spec(qkvT.shape), pl.BlockSpec(memory_space=pl.ANY)],
        out_specs=(row_spec(D_MODEL), seq_spec((N_HEADS, LANES))),
        scratch_shapes=[pltpu.VMEM((_GATE_BUFS, PAGES_PER_BLOCK, N_HEADS, HEAD_DIM, PAGE_SIZE), F32),
                        pltpu.SemaphoreType.DMA((_GATE_BUFS, PAGES_PER_BLOCK)),
                        pltpu.VMEM((n_blocks, N_HEADS, LANES), F32),
                        pltpu.VMEM((N_HEADS, HEAD_DIM, LANES), F32)],
    )
    return pl.pallas_call(
        functools.partial(_tail_kernel, ff_chunk=D_FF // n_groups, n_pages=n_pages, half_tiles=half_tiles),
        grid_spec=grid_spec,
        out_shape=(y_shape, jax.ShapeDtypeStruct((n_seq, N_HEADS, LANES), jnp.int32)),
        compiler_params=params, name="tail_gate",
    )(pt_flat, *args, qkvT, cache_kT)


def _sample_in_kernel(x_ref, ln_ref, wqkvT_ref, wrest_ref, kn_ref, qn_col_ref, kn_col_ref, hist_ref,
                      poolw_ref, pscale_ref,
                      k_ref, v_ref, qkvT_ref, b_ref, sga_ref, sgb_ref, hist_out_ref, *, past_len):
    n = x_ref.shape[0]
    xn = _rms_rows(x_ref[...], ln_ref[...]).astype(BF16)
    k_row = lax.dot_general(xn, wqkvT_ref[ATTN_WIDTH:2 * ATTN_WIDTH, :], _NT, preferred_element_type=F32)
    for h in range(N_HEADS):
        c0 = h * HEAD_DIM
        k_ref[:, c0:c0 + HEAD_DIM] = _rms_rows(k_row[:, c0:c0 + HEAD_DIM], kn_ref[...])
    v_ref[...] = lax.dot_general(xn, wqkvT_ref[2 * ATTN_WIDTH:3 * ATTN_WIDTH, :], _NT, preferred_element_type=F32)
    z = jnp.dot(xn, wrest_ref[...], preferred_element_type=F32)
    zT = lax.dot_general(wqkvT_ref[...], xn, _NT, preferred_element_type=F32).reshape(3, N_HEADS, HEAD_DIM, n)

    def head_norm(t, g_col):
        ms = jnp.mean(t * t, axis=1, keepdims=True)
        return t * lax.rsqrt(ms + EPS) * g_col[None]

    qkvT_ref[0] = head_norm(zT[0], qn_col_ref[...])
    qkvT_ref[1] = head_norm(zT[1], kn_col_ref[...])
    qkvT_ref[2] = zT[2]
    u = z[:, 0:POOL_WIDTH]
    for g_i, w in enumerate(POOL_WINDOWS):
        c0 = g_i * POOL_GROUP
        cur = u[:, c0:c0 + POOL_GROUP]
        acc = cur
        for k in range(1, w):
            acc = acc + hist_ref[POOL_HIST - k, :, c0:c0 + POOL_GROUP]
        cnt = float(min(w, past_len + 1))
        d = acc / cnt - cur
        y = jnp.dot(d.astype(BF16), poolw_ref[g_i], preferred_element_type=F32)
        b_ref[:, c0:c0 + POOL_GROUP] = (y * pscale_ref[:, c0:c0 + POOL_GROUP]).astype(BF16)
    for t in range(POOL_HIST - 1):
        hist_out_ref[t] = hist_ref[t + 1]
    hist_out_ref[POOL_HIST - 1] = u
    g0 = POOL_WIDTH
    sga_ref[...] = _sigmoid(z[:, g0:g0 + D_MODEL]).astype(BF16)
    sgb_ref[...] = _sigmoid(z[:, g0 + D_MODEL:g0 + 2 * D_MODEL]).astype(BF16)


def _sample_in(x, ln, wqkvT, wrest, kn_row, qn_col, kn_col, hist_t, poolw, pscale, past_len):
    n = x.shape[0]
    sds = jax.ShapeDtypeStruct
    out_shape = (sds((n, ATTN_WIDTH), F32), sds((n, ATTN_WIDTH), F32), sds((3, N_HEADS, HEAD_DIM, n), F32),
                 sds((n, POOL_WIDTH), BF16), sds((n, D_MODEL), BF16), sds((n, D_MODEL), BF16),
                 sds(hist_t.shape, F32))
    return pl.pallas_call(
        functools.partial(_sample_in_kernel, past_len=past_len),
        out_shape=out_shape,
        compiler_params=pltpu.CompilerParams(vmem_limit_bytes=VMEM_LIMIT),
        name="sample_in",
    )(x, ln, wqkvT, wrest, kn_row, qn_col, kn_col, hist_t, poolw, pscale)


def _sample_attn_kernel(pt_ref, sel_ref, qkvT_ref, kc_ref, vc_ref, o_ref,
                        kbuf, vbuf, sem, *, n_pages):
    b = pl.program_id(0)
    n_b = pl.num_programs(0)
    n_slots = MOBA_TOP_K * PAGES_PER_BLOCK

    def copies(seq_i, buf_i):
        out = []
        for h in range(N_HEADS):
            for r in range(MOBA_TOP_K):
                blk = sel_ref[(seq_i * N_HEADS + h) * MOBA_TOP_K + r]
                for pg in range(PAGES_PER_BLOCK):
                    page = pt_ref[seq_i * n_pages + blk * PAGES_PER_BLOCK + pg]
                    s_i = r * PAGES_PER_BLOCK + pg
                    out.append(pltpu.make_async_copy(kc_ref.at[page, h], kbuf.at[buf_i, h, s_i], sem.at[buf_i, 0]))
                    out.append(pltpu.make_async_copy(vc_ref.at[page, h], vbuf.at[buf_i, h, s_i], sem.at[buf_i, 1]))
        return out

    @pl.when(b == 0)
    def _():
        for c in copies(0, 0):
            c.start()

    cur = b % 2

    @pl.when(b + 1 < n_b)
    def _():
        for c in copies(b + 1, 1 - cur):
            c.start()

    for c in copies(b, cur):
        c.wait()

    q_col, kn_col, vn_col = (_seq_column(qkvT_ref[i], b) for i in range(3))
    seq_lane = lax.broadcasted_iota(jnp.int32, (1, o_ref.shape[2]), 1)

    @pl.when(b == 0)
    def _():
        o_ref[...] = jnp.zeros(o_ref.shape, F32)

    for h in range(N_HEADS):
        qh = q_col[h]
        kb = kbuf[cur, h]
        vb = vbuf[cur, h]
        s = jnp.sum(kb * qh[None], axis=1) * ATTN_SCALE
        s_new = jnp.sum(qh * kn_col[h], axis=0, keepdims=True) * ATTN_SCALE
        m = jnp.maximum(jnp.max(jnp.max(s, axis=1, keepdims=True), axis=0, keepdims=True), s_new)
        p = jnp.exp(s - m)
        p_new = jnp.exp(s_new - m)
        l = jnp.sum(jnp.sum(p, axis=1, keepdims=True), axis=0, keepdims=True) + p_new
        pv = vb[0] * p[0:1]
        for s_i in range(1, n_slots):
            pv = pv + vb[s_i] * p[s_i:s_i + 1]
        o = jnp.sum(pv, axis=1, keepdims=True) + p_new * vn_col[h]
        o_ref[h] = jnp.where(seq_lane == b, o / l, o_ref[h])


def _sample_attn(page_table_flat, sel_flat, qkvT, cache_kT, cache_vT, n_seq, n_pages):
    n_slots = MOBA_TOP_K * PAGES_PER_BLOCK
    buf = pltpu.VMEM((2, N_HEADS, n_slots, HEAD_DIM, PAGE_SIZE), F32)
    grid_spec = pltpu.PrefetchScalarGridSpec(
        num_scalar_prefetch=2,
        grid=(n_seq,),
        in_specs=[_const_spec(qkvT.shape), pl.BlockSpec(memory_space=pl.ANY), pl.BlockSpec(memory_space=pl.ANY)],
        out_specs=pl.BlockSpec((N_HEADS, HEAD_DIM, n_seq), lambda b, pt, sel: (0, 0, 0)),
        scratch_shapes=[buf, buf, pltpu.SemaphoreType.DMA((2, 2))],
    )
    return pl.pallas_call(
        functools.partial(_sample_attn_kernel, n_pages=n_pages),
        grid_spec=grid_spec,
        out_shape=jax.ShapeDtypeStruct((N_HEADS, HEAD_DIM, n_seq), F32),
        compiler_params=pltpu.CompilerParams(
            dimension_semantics=("arbitrary",), vmem_limit_bytes=VMEM_LIMIT),
        name="sample_attn",
    )(page_table_flat, sel_flat, qkvT, cache_kT, cache_vT)


def kernel(x_prompt, x_sample, cache_k, cache_v, state_pool, page_table, p_prompt, p_sample, ln_mix, w_in,
           q_norm, k_norm, pool_w, pool_scale, w_attn_out, w_pool_out, w_out, ln_mlp, w_up, w_down, ln_ple,
           w_ple_gate, w_ple_proj):
    depth = w_in.shape[0]
    assert depth == 1
    n_seq, n_pages = page_table.shape
    past_len = n_pages * PAGE_SIZE
    assert x_sample.shape[1] == 1 and past_len % MOBA_BLOCK == 0 and past_len // MOBA_BLOCK >= MOBA_TOP_K
    assert (past_len // MOBA_BLOCK) % _GATE_BUFS == 0 and (x_prompt.shape[1] // MOBA_BLOCK) % 16 == 0
    bsz, seq, _ = x_prompt.shape
    l = 0

    wqkvT = w_in[l][:, :3 * ATTN_WIDTH].T.astype(BF16)
    wrest = w_in[l][:, 3 * ATTN_WIDTH:].astype(BF16)
    poolw = pool_w[l].astype(BF16)
    tail_w = (w_attn_out[l].astype(BF16), w_pool_out[l].astype(BF16), w_out[l].astype(BF16), ln_mlp,
              w_up[l].astype(BF16), w_down[l].astype(BF16), ln_ple, w_ple_gate[l].astype(BF16),
              w_ple_proj[l].astype(BF16))

    xs = x_sample.reshape(n_seq, D_MODEL)
    hist_t = state_pool[l].transpose(1, 0, 2)
    qn_col, kn_col = q_norm.reshape(HEAD_DIM, 1), k_norm.reshape(HEAD_DIM, 1)
    k_s, v_s, qkvT_s, b_s, sga_s, sgb_s, hist_s = _sample_in(
        xs, ln_mix, wqkvT, wrest, k_norm, qn_col, kn_col, hist_t, poolw, pool_scale, past_len)
    cache_kT = cache_k[l].transpose(0, 2, 3, 1)
    cache_vT = cache_v[l].transpose(0, 2, 3, 1)
    pt_flat = page_table.reshape(-1)

    (kT, vT, qa, ka, vTb, b_p, sga, sgb, hist_p) = _prompt_in(
        x_prompt, ln_mix, wqkvT, wrest, qn_col, kn_col, poolw, pool_scale)
    a_lo, a_hi = (a.reshape(bsz, ATTN_WIDTH, seq // 2) for a in _prompt_attn(qa, ka, vTb))
    y_prompt, sel = _tail(x_prompt, a_lo, a_hi, b_p, sga, sgb, p_prompt[l], *tail_w, tm=512,
                          gate_inputs=(pt_flat, qkvT_s, cache_kT, n_pages))
    k_prompt = kT.transpose(0, 3, 1, 2)[None]
    v_prompt = vT.transpose(0, 3, 1, 2)[None]
    pool_prompt = hist_p[:, HIST_ROWS - POOL_HIST:][None]

    sel_flat = sel[:, :, :MOBA_TOP_K].reshape(-1)
    a_s = _sample_attn(pt_flat, sel_flat, qkvT_s, cache_kT, cache_vT, n_seq, n_pages)
    aT_s = a_s.reshape(1, ATTN_WIDTH, n_seq).astype(BF16)
    y_sample = _tail(xs[None], aT_s, aT_s, b_s[None], sga_s[None], sgb_s[None], p_sample[l].reshape(1, n_seq, PLE_DIM),
                     *tail_w, tm=n_seq)
    y_sample = y_sample.reshape(n_seq, 1, D_MODEL)
    k_sample = k_s.reshape(1, n_seq, 1, N_HEADS, HEAD_DIM)
    v_sample = v_s.reshape(1, n_seq, 1, N_HEADS, HEAD_DIM)
    pool_sample = hist_s.transpose(1, 0, 2)[None]
    return (y_prompt, y_sample, k_prompt, v_prompt, pool_prompt, k_sample, v_sample, pool_sample)
```

```python
import functools

import jax
import jax.numpy as jnp
from jax import lax
from jax.experimental import pallas as pl
from jax.experimental.pallas import tpu as pltpu

D_MODEL = 1024
N_HEADS = 8
HEAD_DIM = 64
ATTN_WIDTH = N_HEADS * HEAD_DIM
MOBA_BLOCK = 256
MOBA_TOP_K = 3
PAGE_SIZE = 128
PAGES_PER_BLOCK = MOBA_BLOCK // PAGE_SIZE
ATTN_SCALE = HEAD_DIM ** -0.5
POOL_WINDOWS = (2, 4, 8, 16)
POOL_GROUP = 128
POOL_WIDTH = 512
POOL_HIST = 15
HIST_ROWS = 16
V_AUG = HEAD_DIM + 16
D_FF = 4 * D_MODEL
PLE_DIM = 256
EPS = 1e-6
NEG_INF = -1e30
LOWEST = -3e38
LOG2E = 1.4426950408889634

LANES = 128
VMEM_LIMIT = 60 * 1024 * 1024

F32 = jnp.float32
BF16 = jnp.bfloat16

_NT = (((1,), (1,)), ((), ()))
_TN = (((0,), (0,)), ((), ()))


def _rms_rows(x, g):
    ms = jnp.mean(x * x, axis=-1, keepdims=True)
    return x * lax.rsqrt(ms + EPS) * g


def _sigmoid(x):
    return 0.5 * jnp.tanh(0.5 * x) + 0.5


def _split_bf16(a):
    hi = a.astype(BF16)
    lo = (a - hi.astype(F32)).astype(BF16)
    return hi, lo


def _const_spec(shape):
    zeros = (0,) * len(shape)
    return pl.BlockSpec(shape, lambda *_: zeros, pipeline_mode=pl.Buffered(1))


def _pool_means(e_ref, tm, pos0):
    pos = pos0 + lax.broadcasted_iota(jnp.int32, (tm, 1), 0)
    out = []
    for g, w in enumerate(POOL_WINDOWS):
        assert w & (w - 1) == 0 and w - 1 <= HIST_ROWS
        c0 = g * POOL_GROUP
        acc = e_ref[:, c0:c0 + POOL_GROUP]
        shift = 1
        while shift < w:
            acc = acc + pltpu.roll(acc, shift, 0)
            shift *= 2
        cur = e_ref[HIST_ROWS:HIST_ROWS + tm, c0:c0 + POOL_GROUP]
        inv_cnt = 1.0 / jnp.minimum(w, pos + 1).astype(F32)
        out.append(acc[HIST_ROWS:] * inv_cnt - cur)
    return out


def _prompt_in_kernel(x_ref, ln_ref, wqkvT_ref, wrest_ref, qn_ref, kn_ref, poolw_ref, pscale_ref,
                      kT_ref, vT_ref, qa_ref, ka_ref, vTb_ref, b_ref, sga_ref, sgb_ref,
                      hist_ref, e_scr, km_scr, *, tm, n_blocks):
    i = pl.program_id(1)
    n_i = pl.num_programs(1)
    blocks_per_tile = tm // MOBA_BLOCK

    @pl.when(i == 0)
    def _():
        e_scr[0:HIST_ROWS, :] = jnp.zeros((HIST_ROWS, POOL_WIDTH), F32)
        km_scr[...] = jnp.zeros(km_scr.shape, F32)

    xn = _rms_rows(x_ref[...], ln_ref[...]).astype(BF16)

    def proj_t(r0):
        t = lax.dot_general(wqkvT_ref[r0:r0 + ATTN_WIDTH, :], xn, _NT, preferred_element_type=F32)
        return t.reshape(N_HEADS, HEAD_DIM, tm)

    def head_norm(t, g_col):
        ms = jnp.mean(t * t, axis=1, keepdims=True)
        return t * lax.rsqrt(ms + EPS) * g_col[None]

    q = head_norm(proj_t(0), qn_ref[...])
    k = head_norm(proj_t(ATTN_WIDTH), kn_ref[...])
    v = proj_t(2 * ATTN_WIDTH)
    kT_ref[...] = k
    vT_ref[...] = v
    vTb_ref[:, 0:HEAD_DIM, :] = v.astype(BF16)
    ones_row = lax.broadcasted_iota(jnp.int32, (N_HEADS, V_AUG - HEAD_DIM, tm), 1) == 0
    vTb_ref[:, HEAD_DIM:V_AUG, :] = jnp.where(ones_row, 1.0, 0.0).astype(BF16)
    qa_ref[:, 0:HEAD_DIM, :] = (q * (ATTN_SCALE * LOG2E)).astype(BF16)
    key_blk = (i * tm + lax.broadcasted_iota(jnp.int32, (tm, 1), 0)) // MOBA_BLOCK
    onehot = jnp.where(key_blk == lax.broadcasted_iota(jnp.int32, (1, n_blocks), 1), 1.0, 0.0).astype(BF16)
    for h in range(N_HEADS):
        ka_ref[h, :, 0:HEAD_DIM] = k[h].T.astype(BF16)
        ka_ref[h, :, HEAD_DIM:HEAD_DIM + n_blocks] = onehot

    lane = lax.broadcasted_iota(jnp.int32, (1, 1, LANES), 2)
    km = km_scr[...]
    for h in range(blocks_per_tile):
        ks = jnp.sum(k[:, :, h * MOBA_BLOCK:(h + 1) * MOBA_BLOCK], axis=2, keepdims=True) * (1.0 / MOBA_BLOCK)
        km = jnp.where(lane == i * blocks_per_tile + h, ks, km)
    km_scr[...] = km

    pos = i * tm + lax.broadcasted_iota(jnp.int32, (1, tm), 1)
    own = pos // MOBA_BLOCK
    blk = lax.broadcasted_iota(jnp.int32, (n_blocks, 1), 0)
    for h in range(N_HEADS):
        km_hi, km_lo = _split_bf16(km[h][:, 0:n_blocks])
        q_hi, q_lo = _split_bf16(q[h])
        gate = (lax.dot_general(km_hi, q_hi, _TN, preferred_element_type=F32)
                + lax.dot_general(km_hi, q_lo, _TN, preferred_element_type=F32)
                + lax.dot_general(km_lo, q_hi, _TN, preferred_element_type=F32))
        g = jnp.where(blk < own, gate, LOWEST)
        bias = jnp.where(blk == own, 0.0, NEG_INF)
        for _ in range(MOBA_TOP_K):
            m = jnp.max(g, axis=0, keepdims=True)
            idx = jnp.min(jnp.where(g == m, blk, n_blocks), axis=0, keepdims=True)
            idx = jnp.where(m > LOWEST, idx, n_blocks)
            pick = blk == idx
            bias = jnp.where(pick, 0.0, bias)
            g = jnp.where(pick, LOWEST, g)
        qa_ref[h, HEAD_DIM:HEAD_DIM + n_blocks, :] = bias.astype(BF16)
        w = 2 * D_MODEL // N_HEADS
        gcol = jnp.dot(xn, wrest_ref[:, POOL_WIDTH + h * w:POOL_WIDTH + (h + 1) * w], preferred_element_type=F32)
        sg_ref = sga_ref if h < N_HEADS // 2 else sgb_ref
        c0 = (h % (N_HEADS // 2)) * w
        sg_ref[:, c0:c0 + w] = _sigmoid(gcol).astype(BF16)

    u = jnp.dot(xn, wrest_ref[:, 0:POOL_WIDTH], preferred_element_type=F32)
    e_scr[HIST_ROWS:HIST_ROWS + tm, :] = u
    d = _pool_means(e_scr, tm, i * tm)
    for g_i in range(len(POOL_WINDOWS)):
        c0 = g_i * POOL_GROUP
        y = jnp.dot(d[g_i].astype(BF16), poolw_ref[g_i], preferred_element_type=F32)
        b_ref[:, c0:c0 + POOL_GROUP] = (y * pscale_ref[:, c0:c0 + POOL_GROUP]).astype(BF16)
    tail = e_scr[tm:tm + HIST_ROWS, :]
    e_scr[0:HIST_ROWS, :] = tail
    hist_ref[...] = tail


def _prompt_in(x, ln, wqkvT, wrest, qn_col, kn_col, poolw, pscale, tm=512):
    bsz, seq, _ = x.shape
    n_blocks = seq // MOBA_BLOCK
    grid = (bsz, seq // tm)
    hT = lambda dt: jax.ShapeDtypeStruct((bsz, N_HEADS, HEAD_DIM, seq), dt)
    hT_spec = pl.BlockSpec((None, N_HEADS, HEAD_DIM, tm), lambda b, i: (b, 0, 0, i))
    row_spec = lambda w: pl.BlockSpec((None, tm, w), lambda b, i: (b, i, 0))
    aug = HEAD_DIM + n_blocks
    out_shape = (
        hT(F32), hT(F32),
        jax.ShapeDtypeStruct((bsz, N_HEADS, aug, seq), BF16),
        jax.ShapeDtypeStruct((bsz, N_HEADS, seq, aug), BF16),
        jax.ShapeDtypeStruct((bsz, N_HEADS, V_AUG, seq), BF16),
        jax.ShapeDtypeStruct((bsz, seq, POOL_WIDTH), BF16),
        jax.ShapeDtypeStruct((bsz, seq, D_MODEL), BF16),
        jax.ShapeDtypeStruct((bsz, seq, D_MODEL), BF16),
        jax.ShapeDtypeStruct((bsz, HIST_ROWS, POOL_WIDTH), F32),
    )
    out_specs = (
        hT_spec, hT_spec,
        pl.BlockSpec((None, N_HEADS, aug, tm), lambda b, i: (b, 0, 0, i)),
        pl.BlockSpec((None, N_HEADS, tm, aug), lambda b, i: (b, 0, i, 0)),
        pl.BlockSpec((None, N_HEADS, V_AUG, tm), lambda b, i: (b, 0, 0, i)),
        row_spec(POOL_WIDTH), row_spec(D_MODEL), row_spec(D_MODEL),
        pl.BlockSpec((None, HIST_ROWS, POOL_WIDTH), lambda b, i: (b, 0, 0)),
    )
    in_specs = [
        row_spec(D_MODEL),
        _const_spec(ln.shape), _const_spec(wqkvT.shape), _const_spec(wrest.shape),
        _const_spec(qn_col.shape), _const_spec(kn_col.shape),
        _const_spec(poolw.shape), _const_spec(pscale.shape),
    ]
    return pl.pallas_call(
        functools.partial(_prompt_in_kernel, tm=tm, n_blocks=n_blocks),
        grid=grid, in_specs=in_specs, out_specs=out_specs, out_shape=out_shape,
        scratch_shapes=[pltpu.VMEM((HIST_ROWS + tm, POOL_WIDTH), F32),
                        pltpu.VMEM((N_HEADS, HEAD_DIM, LANES), F32)],
        compiler_params=pltpu.CompilerParams(
            dimension_semantics=("arbitrary", "arbitrary"), vmem_limit_bytes=VMEM_LIMIT,
            allow_input_fusion=[False, False, True, True, False, False, True, False]),
        name="prompt_in",
    )(x, ln, wqkvT, wrest, qn_col, kn_col, poolw, pscale)


_ATTN_UNROLL = 10


def _prompt_attn_kernel(qx_ref, qy_ref, ka_ref, vT_ref, ox_ref, oy_ref, q_scr, m_scr, acc_scr, s_scr, cm_scr,
                        *, heads, n_tiles):
    p = pl.program_id(1)
    own = (p, n_tiles - 1 - p)
    blk = MOBA_BLOCK
    tri = (lax.broadcasted_iota(jnp.int32, (blk, 1), 0) <= lax.broadcasted_iota(jnp.int32, (1, blk), 1))
    q_scr[0] = qx_ref[...]
    q_scr[1] = qy_ref[...]
    m_scr[...] = jnp.full(m_scr.shape, NEG_INF, F32)
    acc_scr[...] = jnp.zeros(acc_scr.shape, F32)

    def chain_block(n):
        idx = n - 2
        c = jnp.where(n == 1, 1, (idx >= own[0]).astype(jnp.int32))
        j = jnp.where(n == 1, own[1], idx - c * own[0])
        return c, j

    def scores(c, j, slot, diag, g):
        r0 = pl.multiple_of(j * blk, blk)
        s = jnp.dot(ka_ref[g, pl.ds(r0, blk), :], q_scr[c, g], preferred_element_type=F32)
        if diag:
            s = jnp.where(tri, s, NEG_INF)
        s_scr[slot, g] = s
        cm_scr[slot, g] = jnp.max(s, axis=0, keepdims=True)

    def softmax_pv(c, j, slot, g):
        r0 = pl.multiple_of(j * blk, blk)
        m = m_scr[c, g]
        m_new = jnp.maximum(m, cm_scr[slot, g])
        alpha = jnp.exp2(m - m_new)
        pr = jnp.exp2((s_scr[slot, g] - m_new).astype(BF16))
        m_scr[c, g] = m_new
        pv = jnp.dot(vT_ref[g, :, pl.ds(r0, blk)], pr, preferred_element_type=F32)
        acc_scr[c, g] = alpha * acc_scr[c, g] + pv

    def step(nxt, nxt_slot, nxt_diag, cur, cur_slot):
        for g in range(heads):
            if nxt is not None:
                scores(*nxt, nxt_slot, nxt_diag, g)
            if cur is not None:
                softmax_pv(*cur, cur_slot, g)

    step((0, own[0]), 0, True, None, None)
    step((1, own[1]), 1, True, (0, own[0]), 0)

    def trip(i, carry):
        for u in range(_ATTN_UNROLL):
            n = 1 + _ATTN_UNROLL * i + u
            step(chain_block(n + 1), u % 2, False, chain_block(n), (u + 1) % 2)
        return carry

    lax.fori_loop(0, (n_tiles - 2) // _ATTN_UNROLL, trip, 0)
    last = n_tiles - 1
    step(chain_block(last + 1), (last + 1) % 2, False, chain_block(last), last % 2)
    step(None, None, False, chain_block(last + 1), (last + 1) % 2)
    for o_ref, c in ((ox_ref, 0), (oy_ref, 1)):
        for g in range(heads):
            o_ref[g] = (acc_scr[c, g, 0:HEAD_DIM] / acc_scr[c, g, HEAD_DIM:HEAD_DIM + 1]).astype(o_ref.dtype)


def _prompt_attn(qa, ka, vT, heads=N_HEADS):
    bsz, _, aug, seq = qa.shape
    blk = MOBA_BLOCK
    n_tiles = seq // blk
    half = n_tiles // 2
    assert heads == N_HEADS and n_tiles % 2 == 0 and (n_tiles - 2) % _ATTN_UNROLL == 0 and _ATTN_UNROLL % 2 == 0
    grid = (bsz, half)
    resident = lambda r, c: pl.BlockSpec((None, heads, r, c), lambda b, p: (b, 0, 0, 0),
                                         pipeline_mode=pl.Buffered(1))
    q_spec = lambda f: pl.BlockSpec((None, heads, aug, blk), lambda b, p: (b, 0, 0, f(p)))
    o_spec = lambda f: pl.BlockSpec((None, heads, HEAD_DIM, blk), lambda b, p: (b, 0, 0, f(p)))
    o_shape = jax.ShapeDtypeStruct((bsz, N_HEADS, HEAD_DIM, seq // 2), BF16)
    return pl.pallas_call(
        functools.partial(_prompt_attn_kernel, heads=heads, n_tiles=n_tiles),
        grid=grid,
        in_specs=[q_spec(lambda p: p), q_spec(lambda p: n_tiles - 1 - p),
                  resident(seq, aug), resident(V_AUG, seq)],
        out_specs=(o_spec(lambda p: p), o_spec(lambda p: half - 1 - p)),
        out_shape=(o_shape, o_shape),
        scratch_shapes=[pltpu.VMEM((2, heads, aug, blk), BF16),
                        pltpu.VMEM((2, heads, 1, blk), F32),
                        pltpu.VMEM((2, heads, V_AUG, blk), F32), pltpu.VMEM((2, heads, blk, blk), F32),
                        pltpu.VMEM((2, heads, 1, blk), F32)],
        compiler_params=pltpu.CompilerParams(
            dimension_semantics=("arbitrary", "arbitrary"), vmem_limit_bytes=VMEM_LIMIT),
        name="prompt_attn",
    )(qa, qa, ka, vT)


_GATE_BUFS = 16
_GATE_GROUP = 8


def _seq_column(t, seq_i):
    lane = lax.broadcasted_iota(jnp.int32, (1, 1, t.shape[2]), 2)
    return jnp.sum(jnp.where(lane == seq_i, t, 0.0), axis=2, keepdims=True)


def _top_k_blocks(s_scr, sel_ref, n_blocks):
    lane = lax.broadcasted_iota(jnp.int32, (1, LANES), 1)
    gate = jnp.zeros((N_HEADS, LANES), F32)
    for j in range(n_blocks):
        ssum = jnp.sum(s_scr[j], axis=1, keepdims=True) * (1.0 / MOBA_BLOCK)
        gate = jnp.where(lane == j, ssum, gate)
    g = jnp.where(lane < n_blocks, gate, LOWEST)
    sel = jnp.zeros((N_HEADS, LANES), jnp.int32)
    for r in range(MOBA_TOP_K):
        m = jnp.max(g, axis=1, keepdims=True)
        idx = jnp.min(jnp.where(g == m, lane, LANES), axis=1, keepdims=True)
        sel = jnp.where(lane == r, idx, sel)
        g = jnp.where(lane == idx, LOWEST, g)
    sel_ref[...] = sel


def _tail_kernel(*refs, ff_chunk, n_pages, half_tiles):
    if n_pages is None:
        (x_ref, alo_ref, ahi_ref, b_ref, sga_ref, sgb_ref, p_ref, wao_ref, wpo_ref, wo_ref, lnm_ref,
         wup_ref, wdn_ref, lnp_ref, wpg_ref, wpp_ref, y_ref) = refs
    else:
        (pt_ref, x_ref, alo_ref, ahi_ref, b_ref, sga_ref, sgb_ref, p_ref, wao_ref, wpo_ref, wo_ref, lnm_ref,
         wup_ref, wdn_ref, lnp_ref, wpg_ref, wpp_ref, q_ref, kc_ref, y_ref, sel_ref,
         kbuf, sem, s_scr, qb_scr) = refs
        seq_i = pl.program_id(0) * pl.num_programs(1) + pl.program_id(1)
        n_seq = pl.num_programs(0) * pl.num_programs(1)
        n_blocks = n_pages // PAGES_PER_BLOCK

        def page_copy(s_i, blk_i, slot, pg):
            page = pt_ref[s_i * n_pages + blk_i * PAGES_PER_BLOCK + pg]
            return pltpu.make_async_copy(kc_ref.at[page], kbuf.at[slot, pg], sem.at[slot, pg])

        def start(s_i, blk_i, slot):
            for pg in range(PAGES_PER_BLOCK):
                page_copy(s_i, blk_i, slot, pg).start()

        def wait_group(k):
            for blk_i in range(k * _GATE_GROUP, (k + 1) * _GATE_GROUP):
                for pg in range(PAGES_PER_BLOCK):
                    page_copy(seq_i, blk_i, blk_i % _GATE_BUFS, pg).wait()

        def score_group(k, head_range):
            tot = None
            for h in head_range:
                qh = qb_scr[h]
                for blk_i in range(k * _GATE_GROUP, (k + 1) * _GATE_GROUP):
                    slot = blk_i % _GATE_BUFS
                    ksum = kbuf[slot, 0, h]
                    for pg in range(1, PAGES_PER_BLOCK):
                        ksum = ksum + kbuf[slot, pg, h]
                    row = jnp.sum(ksum * qh, axis=0, keepdims=True)
                    s_scr[blk_i, h:h + 1, :] = row
                    tot = row if tot is None else tot + row
            bits = pltpu.bitcast(tot, jnp.uint32)
            return pltpu.bitcast(lax.shift_right_logical(lax.shift_right_logical(bits, jnp.uint32(16)),
                                                         jnp.uint32(16)), F32)

        def refill_group(k):
            blks = range(k * _GATE_GROUP, (k + 1) * _GATE_GROUP)
            for blk_i in blks:
                if blk_i + _GATE_BUFS < n_blocks:
                    start(seq_i, blk_i + _GATE_BUFS, blk_i % _GATE_BUFS)
            wrap = [blk_i for blk_i in blks if blk_i + _GATE_BUFS >= n_blocks]
            if wrap:
                @pl.when(seq_i + 1 < n_seq)
                def _():
                    for blk_i in wrap:
                        start(seq_i + 1, blk_i + _GATE_BUFS - n_blocks, blk_i % _GATE_BUFS)

        @pl.when(seq_i == 0)
        def _():
            for j in range(_GATE_BUFS):
                start(0, j, j)

        qb_scr[...] = jnp.broadcast_to(_seq_column(q_ref[0], seq_i), qb_scr.shape)

    aT = jnp.where(pl.program_id(1) < half_tiles, alo_ref[...], ahi_ref[...])
    a_proj = lax.dot_general(aT, wao_ref[...], _TN, preferred_element_type=F32)
    b_proj = jnp.dot(b_ref[...], wpo_ref[...], preferred_element_type=F32)
    m = sga_ref[...].astype(F32) * a_proj + sgb_ref[...].astype(F32) * b_proj
    x1 = x_ref[...] + jnp.dot(m.astype(BF16), wo_ref[...], preferred_element_type=F32)
    xn = _rms_rows(x1, lnm_ref[...]).astype(BF16)
    acc = x1
    for c in range(D_FF // ff_chunk):
        xn_c = xn
        if n_pages is not None:
            wait_group(c)
            zero = score_group(c, range(N_HEADS // 2))
            half = D_MODEL // 2
            xn_c = jnp.concatenate(
                [xn[:, :half], xn[:, half:] + jnp.concatenate([zero] * (half // LANES), axis=1).astype(BF16)], axis=1)
        hid = jnp.maximum(jnp.dot(xn_c, wup_ref[:, c * ff_chunk:(c + 1) * ff_chunk],
                                  preferred_element_type=F32), 0.0)
        if n_pages is not None:
            zero = score_group(c, range(N_HEADS // 2, N_HEADS))
            hid = hid + jnp.concatenate([zero] * (ff_chunk // LANES), axis=1)
        acc = acc + jnp.dot((hid * hid).astype(BF16), wdn_ref[c * ff_chunk:(c + 1) * ff_chunk, :],
                            preferred_element_type=F32)
        if n_pages is not None:
            refill_group(c)
    x2 = acc
    if n_pages is not None:
        _top_k_blocks(s_scr, sel_ref, n_blocks)
    xg = _rms_rows(x2, lnp_ref[...]).astype(BF16)
    gate = _sigmoid(jnp.dot(xg, wpg_ref[...], preferred_element_type=F32))
    emb = jnp.dot(p_ref[...].astype(BF16), wpp_ref[...], preferred_element_type=F32)
    y_ref[...] = x2 + gate * emb


def _tail(x, a_lo, a_hi, b, sga, sgb, p, wao, wpo, wo, lnm, wup, wdn, lnp, wpg, wpp, tm, gate_inputs=None):
    bsz, seq, _ = x.shape
    grid = (bsz, seq // tm)
    n_i = seq // tm
    half_tiles = a_lo.shape[2] // tm
    row_spec = lambda w: pl.BlockSpec((None, tm, w), lambda b_, i, *_: (b_, i, 0))
    in_specs = [
        row_spec(D_MODEL),
        pl.BlockSpec((None, ATTN_WIDTH, tm), lambda b_, i, *_: (b_, 0, jnp.minimum(i, half_tiles - 1))),
        pl.BlockSpec((None, ATTN_WIDTH, tm), lambda b_, i, *_: (b_, 0, jnp.maximum(i - half_tiles, 0))),
        row_spec(POOL_WIDTH), row_spec(D_MODEL), row_spec(D_MODEL), row_spec(PLE_DIM),
    ] + [_const_spec(w.shape) for w in (wao, wpo, wo, lnm, wup, wdn, lnp, wpg, wpp)]
    y_shape = jax.ShapeDtypeStruct((bsz, seq, D_MODEL), F32)
    params = pltpu.CompilerParams(dimension_semantics=("arbitrary", "arbitrary"), vmem_limit_bytes=VMEM_LIMIT)
    args = (x, a_lo, a_hi, b, sga, sgb, p, wao, wpo, wo, lnm, wup, wdn, lnp, wpg, wpp)
    if gate_inputs is None:
        return pl.pallas_call(
            functools.partial(_tail_kernel, ff_chunk=D_MODEL, n_pages=None, half_tiles=half_tiles),
            grid=grid, in_specs=in_specs, out_specs=row_spec(D_MODEL), out_shape=y_shape,
            compiler_params=params, name="tail",
        )(*args)
    pt_flat, qkvT, cache_kT, n_pages = gate_inputs
    n_seq = qkvT.shape[3]
    n_blocks = n_pages // PAGES_PER_BLOCK
    n_groups = n_blocks // _GATE_GROUP
    assert bsz * n_i == n_seq and n_blocks % _GATE_BUFS == 0 and _GATE_BUFS % _GATE_GROUP == 0
    assert D_FF % n_groups == 0
    seq_spec = lambda shape: pl.BlockSpec((None,) + shape, lambda b_, i, *_: (b_ * n_i + i,) + (0,) * len(shape))
    grid_spec = pltpu.PrefetchScalarGridSpec(
        num_scalar_prefetch=1, grid=grid,
        in_specs=in_specs + [_const_spec(qkvT.shape), pl.BlockSpec(memory_space=pl.ANY)],
        out_specs=(row_spec(D_MODEL), seq_spec((N_HEADS, LANES))),
        scratch_shapes=[pltpu.VMEM((_GATE_BUFS, PAGES_PER_BLOCK, N_HEADS, HEAD_DIM, PAGE_SIZE), F32),
                        pltpu.SemaphoreType.DMA((_GATE_BUFS, PAGES_PER_BLOCK)),
                        pltpu.VMEM((n_blocks, N_HEADS, LANES), F32),
                        pltpu.VMEM((N_HEADS, HEAD_DIM, LANES), F32)],
    )
    return pl.pallas_call(
        functools.partial(_tail_kernel, ff_chunk=D_FF // n_groups, n_pages=n_pages, half_tiles=half_tiles),
        grid_spec=grid_spec,
        out_shape=(y_shape, jax.ShapeDtypeStruct((n_seq, N_HEADS, LANES), jnp.int32)),
        compiler_params=params, name="tail_gate",
    )(pt_flat, *args, qkvT, cache_kT)


def _sample_in_kernel(x_ref, ln_ref, wqkvT_ref, wrest_ref, kn_ref, qn_col_ref, kn_col_ref, hist_ref,
                      poolw_ref, pscale_ref,
                      k_ref, v_ref, qkvT_ref, b_ref, sga_ref, sgb_ref, hist_out_ref, *, past_len):
    n = x_ref.shape[0]
    xn = _rms_rows(x_ref[...], ln_ref[...]).astype(BF16)
    k_row = lax.dot_general(xn, wqkvT_ref[ATTN_WIDTH:2 * ATTN_WIDTH, :], _NT, preferred_element_type=F32)
    for h in range(N_HEADS):
        c0 = h * HEAD_DIM
        k_ref[:, c0:c0 + HEAD_DIM] = _rms_rows(k_row[:, c0:c0 + HEAD_DIM], kn_ref[...])
    v_ref[...] = lax.dot_general(xn, wqkvT_ref[2 * ATTN_WIDTH:3 * ATTN_WIDTH, :], _NT, preferred_element_type=F32)
    z = jnp.dot(xn, wrest_ref[...], preferred_element_type=F32)
    zT = lax.dot_general(wqkvT_ref[...], xn, _NT, preferred_element_type=F32).reshape(3, N_HEADS, HEAD_DIM, n)

    def head_norm(t, g_col):
        ms = jnp.mean(t * t, axis=1, keepdims=True)
        return t * lax.rsqrt(ms + EPS) * g_col[None]

    qkvT_ref[0] = head_norm(zT[0], qn_col_ref[...])
    qkvT_ref[1] = head_norm(zT[1], kn_col_ref[...])
    qkvT_ref[2] = zT[2]
    u = z[:, 0:POOL_WIDTH]
    for g_i, w in enumerate(POOL_WINDOWS):
        c0 = g_i * POOL_GROUP
        cur = u[:, c0:c0 + POOL_GROUP]
        acc = cur
        for k in range(1, w):
            acc = acc + hist_ref[POOL_HIST - k, :, c0:c0 + POOL_GROUP]
        cnt = float(min(w, past_len + 1))
        d = acc / cnt - cur
        y = jnp.dot(d.astype(BF16), poolw_ref[g_i], preferred_element_type=F32)
        b_ref[:, c0:c0 + POOL_GROUP] = (y * pscale_ref[:, c0:c0 + POOL_GROUP]).astype(BF16)
    for t in range(POOL_HIST - 1):
        hist_out_ref[t] = hist_ref[t + 1]
    hist_out_ref[POOL_HIST - 1] = u
    g0 = POOL_WIDTH
    sga_ref[...] = _sigmoid(z[:, g0:g0 + D_MODEL]).astype(BF16)
    sgb_ref[...] = _sigmoid(z[:, g0 + D_MODEL:g0 + 2 * D_MODEL]).astype(BF16)


def _sample_in(x, ln, wqkvT, wrest, kn_row, qn_col, kn_col, hist_t, poolw, pscale, past_len):
    n = x.shape[0]
    sds = jax.ShapeDtypeStruct
    out_shape = (sds((n, ATTN_WIDTH), F32), sds((n, ATTN_WIDTH), F32), sds((3, N_HEADS, HEAD_DIM, n), F32),
                 sds((n, POOL_WIDTH), BF16), sds((n, D_MODEL), BF16), sds((n, D_MODEL), BF16),
                 sds(hist_t.shape, F32))
    return pl.pallas_call(
        functools.partial(_sample_in_kernel, past_len=past_len),
        out_shape=out_shape,
        compiler_params=pltpu.CompilerParams(
            vmem_limit_bytes=VMEM_LIMIT,
            allow_input_fusion=[False, False, True, True, False, False, False, False, True, False]),
        name="sample_in",
    )(x, ln, wqkvT, wrest, kn_row, qn_col, kn_col, hist_t, poolw, pscale)


def _sample_attn_kernel(pt_ref, sel_ref, qkvT_ref, kc_ref, vc_ref, o_ref,
                        kbuf, vbuf, sem, *, n_pages):
    b = pl.program_id(0)
    n_b = pl.num_programs(0)
    n_slots = MOBA_TOP_K * PAGES_PER_BLOCK

    def copies(seq_i, buf_i):
        out = []
        for h in range(N_HEADS):
            for r in range(MOBA_TOP_K):
                blk = sel_ref[(seq_i * N_HEADS + h) * MOBA_TOP_K + r]
                for pg in range(PAGES_PER_BLOCK):
                    page = pt_ref[seq_i * n_pages + blk * PAGES_PER_BLOCK + pg]
                    s_i = r * PAGES_PER_BLOCK + pg
                    out.append(pltpu.make_async_copy(kc_ref.at[page, h], kbuf.at[buf_i, h, s_i], sem.at[buf_i, 0]))
                    out.append(pltpu.make_async_copy(vc_ref.at[page, h], vbuf.at[buf_i, h, s_i], sem.at[buf_i, 1]))
        return out

    @pl.when(b == 0)
    def _():
        for c in copies(0, 0):
            c.start()

    cur = b % 2

    @pl.when(b + 1 < n_b)
    def _():
        for c in copies(b + 1, 1 - cur):
            c.start()

    for c in copies(b, cur):
        c.wait()

    q_col, kn_col, vn_col = (_seq_column(qkvT_ref[i], b) for i in range(3))
    seq_lane = lax.broadcasted_iota(jnp.int32, (1, o_ref.shape[2]), 1)

    @pl.when(b == 0)
    def _():
        o_ref[...] = jnp.zeros(o_ref.shape, F32)

    for h in range(N_HEADS):
        qh = q_col[h]
        kb = kbuf[cur, h]
        vb = vbuf[cur, h]
        s = jnp.sum(kb * qh[None], axis=1) * ATTN_SCALE
        s_new = jnp.sum(qh * kn_col[h], axis=0, keepdims=True) * ATTN_SCALE
        m = jnp.maximum(jnp.max(jnp.max(s, axis=1, keepdims=True), axis=0, keepdims=True), s_new)
        p = jnp.exp(s - m)
        p_new = jnp.exp(s_new - m)
        l = jnp.sum(jnp.sum(p, axis=1, keepdims=True), axis=0, keepdims=True) + p_new
        pv = vb[0] * p[0:1]
        for s_i in range(1, n_slots):
            pv = pv + vb[s_i] * p[s_i:s_i + 1]
        o = jnp.sum(pv, axis=1, keepdims=True) + p_new * vn_col[h]
        o_ref[h] = jnp.where(seq_lane == b, o / l, o_ref[h])


def _sample_attn(page_table_flat, sel_flat, qkvT, cache_kT, cache_vT, n_seq, n_pages):
    n_slots = MOBA_TOP_K * PAGES_PER_BLOCK
    buf = pltpu.VMEM((2, N_HEADS, n_slots, HEAD_DIM, PAGE_SIZE), F32)
    grid_spec = pltpu.PrefetchScalarGridSpec(
        num_scalar_prefetch=2,
        grid=(n_seq,),
        in_specs=[_const_spec(qkvT.shape), pl.BlockSpec(memory_space=pl.ANY), pl.BlockSpec(memory_space=pl.ANY)],
        out_specs=pl.BlockSpec((N_HEADS, HEAD_DIM, n_seq), lambda b, pt, sel: (0, 0, 0)),
        scratch_shapes=[buf, buf, pltpu.SemaphoreType.DMA((2, 2))],
    )
    return pl.pallas_call(
        functools.partial(_sample_attn_kernel, n_pages=n_pages),
        grid_spec=grid_spec,
        out_shape=jax.ShapeDtypeStruct((N_HEADS, HEAD_DIM, n_seq), F32),
        compiler_params=pltpu.CompilerParams(
            dimension_semantics=("arbitrary",), vmem_limit_bytes=VMEM_LIMIT),
        name="sample_attn",
    )(page_table_flat, sel_flat, qkvT, cache_kT, cache_vT)


def kernel(x_prompt, x_sample, cache_k, cache_v, state_pool, page_table, p_prompt, p_sample, ln_mix, w_in,
           q_norm, k_norm, pool_w, pool_scale, w_attn_out, w_pool_out, w_out, ln_mlp, w_up, w_down, ln_ple,
           w_ple_gate, w_ple_proj):
    depth = w_in.shape[0]
    assert depth == 1
    n_seq, n_pages = page_table.shape
    past_len = n_pages * PAGE_SIZE
    assert x_sample.shape[1] == 1 and past_len % MOBA_BLOCK == 0 and past_len // MOBA_BLOCK >= MOBA_TOP_K
    assert (past_len // MOBA_BLOCK) % _GATE_BUFS == 0 and (x_prompt.shape[1] // MOBA_BLOCK) % 16 == 0
    bsz, seq, _ = x_prompt.shape
    l = 0

    wqkvT = w_in[l][:, :3 * ATTN_WIDTH].T.astype(BF16)
    wrest = w_in[l][:, 3 * ATTN_WIDTH:].astype(BF16)
    poolw = pool_w[l].astype(BF16)
    tail_w = (w_attn_out[l].astype(BF16), w_pool_out[l].astype(BF16), w_out[l].astype(BF16), ln_mlp,
              w_up[l].astype(BF16), w_down[l].astype(BF16), ln_ple, w_ple_gate[l].astype(BF16),
              w_ple_proj[l].astype(BF16))

    xs = x_sample.reshape(n_seq, D_MODEL)
    hist_t = state_pool[l].transpose(1, 0, 2)
    qn_col, kn_col = q_norm.reshape(HEAD_DIM, 1), k_norm.reshape(HEAD_DIM, 1)
    k_s, v_s, qkvT_s, b_s, sga_s, sgb_s, hist_s = _sample_in(
        xs, ln_mix, wqkvT, wrest, k_norm, qn_col, kn_col, hist_t, poolw, pool_scale, past_len)
    cache_kT = cache_k[l].transpose(0, 2, 3, 1)
    cache_vT = cache_v[l].transpose(0, 2, 3, 1)
    pt_flat = page_table.reshape(-1)

    (kT, vT, qa, ka, vTb, b_p, sga, sgb, hist_p) = _prompt_in(
        x_prompt, ln_mix, wqkvT, wrest, qn_col, kn_col, poolw, pool_scale)
    a_lo, a_hi = (a.reshape(bsz, ATTN_WIDTH, seq // 2) for a in _prompt_attn(qa, ka, vTb))
    y_prompt, sel = _tail(x_prompt, a_lo, a_hi, b_p, sga, sgb, p_prompt[l], *tail_w, tm=512,
                          gate_inputs=(pt_flat, qkvT_s, cache_kT, n_pages))
    k_prompt = kT.transpose(0, 3, 1, 2)[None]
    v_prompt = vT.transpose(0, 3, 1, 2)[None]
    pool_prompt = hist_p[:, HIST_ROWS - POOL_HIST:][None]

    sel_flat = sel[:, :, :MOBA_TOP_K].reshape(-1)
    a_s = _sample_attn(pt_flat, sel_flat, qkvT_s, cache_kT, cache_vT, n_seq, n_pages)
    aT_s = a_s.reshape(1, ATTN_WIDTH, n_seq).astype(BF16)
    y_sample = _tail(xs[None], aT_s, aT_s, b_s[None], sga_s[None], sgb_s[None], p_sample[l].reshape(1, n_seq, PLE_DIM),
                     *tail_w, tm=n_seq)
    y_sample = y_sample.reshape(n_seq, 1, D_MODEL)
    k_sample = k_s.reshape(1, n_seq, 1, N_HEADS, HEAD_DIM)
    v_sample = v_s.reshape(1, n_seq, 1, N_HEADS, HEAD_DIM)
    pool_sample = hist_s.transpose(1, 0, 2)[None]
    return (y_prompt, y_sample, k_prompt, v_prompt, pool_prompt, k_sample, v_sample, pool_sample)
```
